```python
import jax, jax.numpy as jnp
from jax import lax
import numpy as np

D_MODEL = 2048
BATCH = 8
SEQ = 8192
DEPTH = 4

CHUNK = 64
HEAD_DIM = 128
MIX_WIDTH = D_MODEL
N_HEADS_FOX = MIX_WIDTH // (2 * HEAD_DIM)
N_HEADS_SB = MIX_WIDTH // (2 * HEAD_DIM)
WIDTH_FOX = N_HEADS_FOX * HEAD_DIM
WIDTH_SB = N_HEADS_SB * HEAD_DIM
IN_COLS = 3 * WIDTH_FOX + N_HEADS_FOX + 3 * WIDTH_SB
D_FF = ((8 * D_MODEL // 3 + 255) // 256) * 256
CONV_WIDTH = 3
Q_BLOCK = 128
EPS = 1e-6

kernel_name = "fox_stickbreak_hybrid_convffn"


def rms_norm(x, g):
    x32 = x.astype(jnp.float32)
    y = x32 * lax.rsqrt(jnp.mean(x32 * x32, axis=-1, keepdims=True) + EPS)
    return (y * g.astype(jnp.float32)).astype(x.dtype)


def to_heads(t, n_heads):
    b, s, _ = t.shape
    return t.reshape(b, s, n_heads, HEAD_DIM).transpose(0, 2, 1, 3)


def from_heads(t):
    b, h, s, d = t.shape
    return t.transpose(0, 2, 1, 3).reshape(b, s, h * d)


def to_blocks(t):
    b, h, s = t.shape[:3]
    nb = s // Q_BLOCK
    t = t.reshape((b, h, nb, Q_BLOCK) + t.shape[3:])
    return jnp.moveaxis(t, 2, 0)


def from_blocks(t):
    nb, b, h, qb, d = t.shape
    return jnp.moveaxis(t, 0, 2).reshape(b, h, nb * qb, d)


def forgetting_attention(q, k, v, c):
    s_len = k.shape[2]
    scale = HEAD_DIM ** -0.5
    k_pos = jnp.arange(s_len)
    k32 = k.astype(jnp.float32)
    c32 = c.astype(jnp.float32)

    def block(args):
        qi, ci, bi = args
        logits = jnp.einsum('bhqd,bhkd->bhqk', qi.astype(jnp.float32), k32) * scale
        logits = logits + ci[..., None] - c32[:, :, None, :]
        q_pos = bi * Q_BLOCK + jnp.arange(Q_BLOCK)
        mask = k_pos[None, :] <= q_pos[:, None]
        logits = jnp.where(mask, logits, -jnp.inf)
        p = jax.nn.softmax(logits, axis=-1)
        return jnp.einsum('bhqk,bhkd->bhqd', p.astype(v.dtype), v)

    nb = s_len // Q_BLOCK
    out = lax.map(block, (to_blocks(q), to_blocks(c32), jnp.arange(nb)))
    return from_blocks(out)


def stick_breaking_attention(q, k, v):
    s_len = k.shape[2]
    scale = HEAD_DIM ** -0.5
    k_pos = jnp.arange(s_len)
    k32 = k.astype(jnp.float32)

    def block(args):
        qi, bi = args
        z = jnp.einsum('bhqd,bhkd->bhqk', qi.astype(jnp.float32), k32) * scale
        q_pos = bi * Q_BLOCK + jnp.arange(Q_BLOCK)
        mask = k_pos[None, :] < q_pos[:, None]
        log_beta = jax.nn.log_sigmoid(z)
        log_one_minus = jnp.where(mask, log_beta - z, 0.0)
        key_axis = log_one_minus.ndim - 1
        suffix = lax.cumsum(log_one_minus, axis=key_axis, reverse=True) - log_one_minus
        weights = jnp.where(mask, jnp.exp(log_beta + suffix), 0.0)
        return jnp.einsum('bhqk,bhkd->bhqd', weights.astype(v.dtype), v)

    nb = s_len // Q_BLOCK
    out = lax.map(block, (to_blocks(q), jnp.arange(nb)))
    return from_blocks(out)


def hybrid_mixer(h, w_in_l, b_forget_l, q_norm_l, k_norm_l, out_norm_fox_l, out_norm_sb_l, w_out_l):
    proj = h @ w_in_l
    splits = [WIDTH_FOX, 2 * WIDTH_FOX, 3 * WIDTH_FOX, 3 * WIDTH_FOX + N_HEADS_FOX,
              3 * WIDTH_FOX + N_HEADS_FOX + WIDTH_SB, 3 * WIDTH_FOX + N_HEADS_FOX + 2 * WIDTH_SB]
    qa, ka, va, fa, qb, kb, vb = jnp.split(proj, splits, axis=-1)

    qa = rms_norm(to_heads(qa, N_HEADS_FOX), q_norm_l)
    ka = rms_norm(to_heads(ka, N_HEADS_FOX), k_norm_l)
    log_f = jax.nn.log_sigmoid((fa + b_forget_l).astype(jnp.float32))
    c = jnp.cumsum(log_f, axis=1).transpose(0, 2, 1)
    out_a = forgetting_attention(qa, ka, to_heads(va, N_HEADS_FOX), c)

    out_b = stick_breaking_attention(to_heads(qb, N_HEADS_SB), to_heads(kb, N_HEADS_SB),
                                     to_heads(vb, N_HEADS_SB))

    merged = jnp.concatenate([rms_norm(from_heads(out_a), out_norm_fox_l),
                              rms_norm(from_heads(out_b), out_norm_sb_l)], axis=-1)
    return merged @ w_out_l


def conv_ffn(h, w_up_l, conv_w_l, conv_b_l, w_down_l):
    u = h @ w_up_l
    c = u.shape[-1]
    u = lax.conv_general_dilated(
        u, conv_w_l[:, None, :].astype(u.dtype), window_strides=(1,),
        padding=[(CONV_WIDTH - 1, 0)], dimension_numbers=('NWC', 'WIO', 'NWC'),
        feature_group_count=c) + conv_b_l
    gate, val = jnp.split(u, 2, axis=-1)
    return (jax.nn.silu(gate) * val) @ w_down_l


def _fwd_setup_inputs(seed: int = 0) -> dict:
    key = jax.random.key(seed)
    ks = jax.random.split(key, 14)
    nrm = jax.random.normal
    f32 = jnp.float32
    return {
        "x": nrm(ks[0], (BATCH, SEQ, D_MODEL), f32),
        "attn_norm": 1.0 + 0.02 * nrm(ks[1], (DEPTH, D_MODEL), f32),
        "w_in": nrm(ks[2], (DEPTH, D_MODEL, IN_COLS), f32) * D_MODEL ** -0.5,
        "b_forget": 3.0 + 0.5 * nrm(ks[3], (DEPTH, N_HEADS_FOX), f32),
        "q_norm": 1.0 + 0.02 * nrm(ks[4], (DEPTH, HEAD_DIM), f32),
        "k_norm": 1.0 + 0.02 * nrm(ks[5], (DEPTH, HEAD_DIM), f32),
        "out_norm_fox": 1.0 + 0.02 * nrm(ks[6], (DEPTH, WIDTH_FOX), f32),
        "out_norm_sb": 1.0 + 0.02 * nrm(ks[7], (DEPTH, WIDTH_SB), f32),
        "w_out": nrm(ks[8], (DEPTH, MIX_WIDTH, D_MODEL), f32) * MIX_WIDTH ** -0.5,
        "ffn_norm": 1.0 + 0.02 * nrm(ks[9], (DEPTH, D_MODEL), f32),
        "w_up": nrm(ks[10], (DEPTH, D_MODEL, 2 * D_FF), f32) * D_MODEL ** -0.5,
        "conv_w": nrm(ks[11], (DEPTH, CONV_WIDTH, 2 * D_FF), f32) * CONV_WIDTH ** -0.5,
        "conv_b": 0.02 * nrm(ks[12], (DEPTH, 2 * D_FF), f32),
        "w_down": nrm(ks[13], (DEPTH, D_FF, D_MODEL), f32) * D_FF ** -0.5,
    }


def _fwd_reference(x, attn_norm, w_in, b_forget, q_norm, k_norm, out_norm_fox, out_norm_sb,
              w_out, ffn_norm, w_up, conv_w, conv_b, w_down):
    for layer in range(DEPTH):
        h = rms_norm(x, attn_norm[layer])
        x = x + hybrid_mixer(h, w_in[layer], b_forget[layer], q_norm[layer], k_norm[layer],
                             out_norm_fox[layer], out_norm_sb[layer], w_out[layer])
        h = rms_norm(x, ffn_norm[layer])
        x = x + conv_ffn(h, w_up[layer], conv_w[layer], conv_b[layer], w_down[layer])
    return x


import jax as _jax
import jax.numpy as _jnp

TWIN_FORMAT = 'train_step'
FWD_PARAMS = ['x', 'attn_norm', 'w_in', 'b_forget', 'q_norm', 'k_norm', 'out_norm_fox', 'out_norm_sb', 'w_out', 'ffn_norm', 'w_up', 'conv_w', 'conv_b', 'w_down']
TWIN_WEIGHTS = ['attn_norm', 'w_in', 'b_forget', 'q_norm', 'k_norm', 'out_norm_fox', 'out_norm_sb', 'w_out', 'ffn_norm', 'w_up', 'conv_w', 'conv_b', 'w_down']
TWIN_DIFF_INPUT = 'x'
TWIN_INPUTS = ['x', 'attn_norm', 'w_in', 'b_forget', 'q_norm', 'k_norm', 'out_norm_fox', 'out_norm_sb', 'w_out', 'ffn_norm', 'w_up', 'conv_w', 'conv_b', 'w_down', 'loss_target', 'm_attn_norm', 'm_w_in', 'm_b_forget', 'm_q_norm', 'm_k_norm', 'm_out_norm_fox', 'm_out_norm_sb', 'm_w_out', 'm_ffn_norm', 'm_w_up', 'm_conv_w', 'm_conv_b', 'm_w_down', 'v_attn_norm', 'v_w_in', 'v_b_forget', 'v_q_norm', 'v_k_norm', 'v_out_norm_fox', 'v_out_norm_sb', 'v_w_out', 'v_ffn_norm', 'v_w_up', 'v_conv_w', 'v_conv_b', 'v_w_down']
TWIN_OUTPUTS = ['loss', 'grad_x', 'grad_attn_norm', 'grad_w_in', 'grad_b_forget', 'grad_q_norm', 'grad_k_norm', 'grad_out_norm_fox', 'grad_out_norm_sb', 'grad_w_out', 'grad_ffn_norm', 'grad_w_up', 'grad_conv_w', 'grad_conv_b', 'grad_w_down', 'delta_attn_norm', 'delta_w_in', 'delta_b_forget', 'delta_q_norm', 'delta_k_norm', 'delta_out_norm_fox', 'delta_out_norm_sb', 'delta_w_out', 'delta_ffn_norm', 'delta_w_up', 'delta_conv_w', 'delta_conv_b', 'delta_w_down', 'new_m_attn_norm', 'new_m_w_in', 'new_m_b_forget', 'new_m_q_norm', 'new_m_k_norm', 'new_m_out_norm_fox', 'new_m_out_norm_sb', 'new_m_w_out', 'new_m_ffn_norm', 'new_m_w_up', 'new_m_conv_w', 'new_m_conv_b', 'new_m_w_down', 'new_v_attn_norm', 'new_v_w_in', 'new_v_b_forget', 'new_v_q_norm', 'new_v_k_norm', 'new_v_out_norm_fox', 'new_v_out_norm_sb', 'new_v_w_out', 'new_v_ffn_norm', 'new_v_w_up', 'new_v_conv_w', 'new_v_conv_b', 'new_v_w_down']
TWIN_LEAF_KINDS = {'loss': 'loss', 'grad_x': 'grad_x', 'grad_attn_norm': 'grad_w', 'grad_w_in': 'grad_w', 'grad_b_forget': 'grad_w', 'grad_q_norm': 'grad_w', 'grad_k_norm': 'grad_w', 'grad_out_norm_fox': 'grad_w', 'grad_out_norm_sb': 'grad_w', 'grad_w_out': 'grad_w', 'grad_ffn_norm': 'grad_w', 'grad_w_up': 'grad_w', 'grad_conv_w': 'grad_w', 'grad_conv_b': 'grad_w', 'grad_w_down': 'grad_w', 'delta_attn_norm': 'delta_w', 'delta_w_in': 'delta_w', 'delta_b_forget': 'delta_w', 'delta_q_norm': 'delta_w', 'delta_k_norm': 'delta_w', 'delta_out_norm_fox': 'delta_w', 'delta_out_norm_sb': 'delta_w', 'delta_w_out': 'delta_w', 'delta_ffn_norm': 'delta_w', 'delta_w_up': 'delta_w', 'delta_conv_w': 'delta_w', 'delta_conv_b': 'delta_w', 'delta_w_down': 'delta_w', 'new_m_attn_norm': 'new_m', 'new_m_w_in': 'new_m', 'new_m_b_forget': 'new_m', 'new_m_q_norm': 'new_m', 'new_m_k_norm': 'new_m', 'new_m_out_norm_fox': 'new_m', 'new_m_out_norm_sb': 'new_m', 'new_m_w_out': 'new_m', 'new_m_ffn_norm': 'new_m', 'new_m_w_up': 'new_m', 'new_m_conv_w': 'new_m', 'new_m_conv_b': 'new_m', 'new_m_w_down': 'new_m', 'new_v_attn_norm': 'new_v', 'new_v_w_in': 'new_v', 'new_v_b_forget': 'new_v', 'new_v_q_norm': 'new_v', 'new_v_k_norm': 'new_v', 'new_v_out_norm_fox': 'new_v', 'new_v_out_norm_sb': 'new_v', 'new_v_w_out': 'new_v', 'new_v_ffn_norm': 'new_v', 'new_v_w_up': 'new_v', 'new_v_conv_w': 'new_v', 'new_v_conv_b': 'new_v', 'new_v_w_down': 'new_v'}


def _forward(args):
    return _fwd_reference(*[args[k] for k in FWD_PARAMS])


def _output_shape():
    def fwd():
        inp = _fwd_setup_inputs(0)
        return _fwd_reference(*[inp[k] for k in FWD_PARAMS])
    out = _jax.eval_shape(fwd)
    return out.shape, out.dtype

N_MICROBATCH = 1
ADAM_LR = 0.001
ADAM_B1 = 0.9
ADAM_B2 = 0.999
ADAM_EPS = 1e-08
ADAM_WD = 0.01
ADAM_STEP = 10
PER_EXAMPLE_BATCH_AXIS = {'x': 0, 'loss_target': 0}
SHARED_INPUTS = []
_WEIGHT_DTYPES = {'attn_norm': _jnp.float32, 'w_in': _jnp.float32, 'b_forget': _jnp.float32, 'q_norm': _jnp.float32, 'k_norm': _jnp.float32, 'out_norm_fox': _jnp.float32, 'out_norm_sb': _jnp.float32, 'w_out': _jnp.float32, 'ffn_norm': _jnp.float32, 'w_up': _jnp.float32, 'conv_w': _jnp.float32, 'conv_b': _jnp.float32, 'w_down': _jnp.float32}
MOMENT_SCALE = {'attn_norm': 1.705901e+00, 'w_in': 9.617338e-01, 'b_forget': 8.176144e+00, 'q_norm': 1.985928e+00, 'k_norm': 1.995341e+00, 'out_norm_fox': 3.341930e+01, 'out_norm_sb': 3.167734e+01, 'w_out': 1.668135e+00, 'ffn_norm': 2.591548e+01, 'w_up': 3.452285e-01, 'conv_w': 4.049295e+00, 'conv_b': 3.338035e+00, 'w_down': 5.335887e-01}


def _to_microbatches(a, axis):
    t = _jnp.moveaxis(a, axis, 0)
    t = t.reshape((N_MICROBATCH, t.shape[0] // N_MICROBATCH) + t.shape[1:])
    return _jnp.moveaxis(t, 1, axis + 1)


def setup_inputs(seed: int = 0) -> dict:
    inp = _fwd_setup_inputs(seed)
    key = _jax.random.fold_in(_jax.random.key(seed), 7919)
    shape, _ = _output_shape()
    out = dict(inp)
    out["loss_target"] = _jax.random.normal(_jax.random.fold_in(key, 0), shape, _jnp.float32)
    for i, name in enumerate(TWIN_WEIGHTS):
        w = inp[name].astype(_jnp.float32)
        if MOMENT_SCALE is None:
            s = _jnp.sqrt(_jnp.mean(_jnp.square(w)) + 1e-30)
        else:
            s = MOMENT_SCALE[name]
        km, kv = _jax.random.split(_jax.random.fold_in(key, i + 1))
        out[name] = w
        out["m_" + name] = s * _jax.random.normal(km, w.shape, _jnp.float32)
        out["v_" + name] = (s * s) * _jax.random.uniform(kv, w.shape, _jnp.float32, 0.5, 1.5)
    if N_MICROBATCH > 1:
        for name, axis in PER_EXAMPLE_BATCH_AXIS.items():
            out[name] = _to_microbatches(out[name], axis)
    return {'x': out['x'], 'attn_norm': out['attn_norm'], 'w_in': out['w_in'], 'b_forget': out['b_forget'], 'q_norm': out['q_norm'], 'k_norm': out['k_norm'], 'out_norm_fox': out['out_norm_fox'], 'out_norm_sb': out['out_norm_sb'], 'w_out': out['w_out'], 'ffn_norm': out['ffn_norm'], 'w_up': out['w_up'], 'conv_w': out['conv_w'], 'conv_b': out['conv_b'], 'w_down': out['w_down'], 'loss_target': out['loss_target'], 'm_attn_norm': out['m_attn_norm'], 'm_w_in': out['m_w_in'], 'm_b_forget': out['m_b_forget'], 'm_q_norm': out['m_q_norm'], 'm_k_norm': out['m_k_norm'], 'm_out_norm_fox': out['m_out_norm_fox'], 'm_out_norm_sb': out['m_out_norm_sb'], 'm_w_out': out['m_w_out'], 'm_ffn_norm': out['m_ffn_norm'], 'm_w_up': out['m_w_up'], 'm_conv_w': out['m_conv_w'], 'm_conv_b': out['m_conv_b'], 'm_w_down': out['m_w_down'], 'v_attn_norm': out['v_attn_norm'], 'v_w_in': out['v_w_in'], 'v_b_forget': out['v_b_forget'], 'v_q_norm': out['v_q_norm'], 'v_k_norm': out['v_k_norm'], 'v_out_norm_fox': out['v_out_norm_fox'], 'v_out_norm_sb': out['v_out_norm_sb'], 'v_w_out': out['v_w_out'], 'v_ffn_norm': out['v_ffn_norm'], 'v_w_up': out['v_w_up'], 'v_conv_w': out['v_conv_w'], 'v_conv_b': out['v_conv_b'], 'v_w_down': out['v_w_down']}


def _loss(weights, diff, rest, loss_target):
    with _jax.named_scope("forward"):
        args = {**rest, TWIN_DIFF_INPUT: diff, **{k: w.astype(_WEIGHT_DTYPES[k]) for k, w in weights.items()}}
        y = _forward(args)
    with _jax.named_scope("loss_head"):
        err = _jnp.square(y.astype(_jnp.float32) - loss_target)
        return 0.5 * _jnp.sum(_jnp.mean(err, axis=-1)) if err.ndim else 0.5 * err


def _adamw(w, g, m, v):
    m = ADAM_B1 * m + (1.0 - ADAM_B1) * g
    v = ADAM_B2 * v + (1.0 - ADAM_B2) * _jnp.square(g)
    m_hat = m / (1.0 - ADAM_B1 ** ADAM_STEP)
    v_hat = v / (1.0 - ADAM_B2 ** ADAM_STEP)
    delta = -ADAM_LR * (m_hat / (_jnp.sqrt(v_hat) + ADAM_EPS) + ADAM_WD * w)
    return delta, m, v


def reference(x, attn_norm, w_in, b_forget, q_norm, k_norm, out_norm_fox, out_norm_sb, w_out, ffn_norm, w_up, conv_w, conv_b, w_down, loss_target, m_attn_norm, m_w_in, m_b_forget, m_q_norm, m_k_norm, m_out_norm_fox, m_out_norm_sb, m_w_out, m_ffn_norm, m_w_up, m_conv_w, m_conv_b, m_w_down, v_attn_norm, v_w_in, v_b_forget, v_q_norm, v_k_norm, v_out_norm_fox, v_out_norm_sb, v_w_out, v_ffn_norm, v_w_up, v_conv_w, v_conv_b, v_w_down):
    given = dict(x=x, attn_norm=attn_norm, w_in=w_in, b_forget=b_forget, q_norm=q_norm, k_norm=k_norm, out_norm_fox=out_norm_fox, out_norm_sb=out_norm_sb, w_out=w_out, ffn_norm=ffn_norm, w_up=w_up, conv_w=conv_w, conv_b=conv_b, w_down=w_down, loss_target=loss_target, m_attn_norm=m_attn_norm, m_w_in=m_w_in, m_b_forget=m_b_forget, m_q_norm=m_q_norm, m_k_norm=m_k_norm, m_out_norm_fox=m_out_norm_fox, m_out_norm_sb=m_out_norm_sb, m_w_out=m_w_out, m_ffn_norm=m_ffn_norm, m_w_up=m_w_up, m_conv_w=m_conv_w, m_conv_b=m_conv_b, m_w_down=m_w_down, v_attn_norm=v_attn_norm, v_w_in=v_w_in, v_b_forget=v_b_forget, v_q_norm=v_q_norm, v_k_norm=v_k_norm, v_out_norm_fox=v_out_norm_fox, v_out_norm_sb=v_out_norm_sb, v_w_out=v_w_out, v_ffn_norm=v_ffn_norm, v_w_up=v_w_up, v_conv_w=v_conv_w, v_conv_b=v_conv_b, v_w_down=v_w_down)
    weights = {n: given[n] for n in TWIN_WEIGHTS}
    shared = {n: given[n] for n in SHARED_INPUTS}
    per_example = {n: given[n] for n in ['x']}
    grad_fn = _jax.value_and_grad(_loss, argnums=(0, 1))

    def one_microbatch(ex, loss_target):
        ex = dict(ex)
        diff = ex.pop(TWIN_DIFF_INPUT)
        return grad_fn(weights, diff, {**shared, **ex}, loss_target)

    if N_MICROBATCH == 1:
        loss, (grad_w, grad_x) = one_microbatch(per_example, given["loss_target"])
    else:
        def body(carry, xs):
            loss_sum, grad_sum = carry
            l_k, (gw_k, gx_k) = one_microbatch(xs[0], xs[1])
            with _jax.named_scope("update"):
                return (loss_sum + l_k, _jax.tree.map(_jnp.add, grad_sum, gw_k)), gx_k

        init = (_jnp.zeros((), _jnp.float32), _jax.tree.map(_jnp.zeros_like, weights))
        (loss, grad_w), grad_x = _jax.lax.scan(body, init, (per_example, given["loss_target"]))
    with _jax.named_scope("update"):
        delta_w, new_m, new_v = {}, {}, {}
        for n in TWIN_WEIGHTS:
            delta_w[n], new_m[n], new_v[n] = _adamw(weights[n], grad_w[n], given["m_" + n], given["v_" + n])
    return (loss, grad_x, *[grad_w[n] for n in TWIN_WEIGHTS], *[delta_w[n] for n in TWIN_WEIGHTS],
            *[new_m[n] for n in TWIN_WEIGHTS], *[new_v[n] for n in TWIN_WEIGHTS])
```

```python
import functools

import jax
import jax.numpy as jnp
from jax import lax
from jax.experimental import pallas as pl
from jax.experimental.pallas import tpu as pltpu

F32 = jnp.float32
BF16 = jnp.bfloat16
HEAD_DIM = 128
LANES = 128
EPS = 1e-6
N_DEV = 8
ADAM_LR = 0.001
ADAM_B1 = 0.9
ADAM_B2 = 0.999
ADAM_EPS = 1e-08
ADAM_WD = 0.01
ADAM_STEP = 10
VMEM_LIMIT = 56 * 1024 * 1024
MESH = pl.DeviceIdType.MESH

NT_DIMS = (((1,), (1,)), ((), ()))
TN_DIMS = (((0,), (0,)), ((), ()))


def _tile(n, cap, mult=LANES):
    t = (min(cap, n) // mult) * mult
    while t >= mult:
        if n % t == 0:
            return t
        t -= mult
    return n


def _params(sem, vmem=VMEM_LIMIT):
    return pltpu.CompilerParams(dimension_semantics=sem, vmem_limit_bytes=vmem)


def _split_dot(x, u, n_split, x_left=True):
    acc = None
    rest = x
    for s in range(n_split):
        piece = rest.astype(BF16)
        if s + 1 < n_split:
            rest = rest - piece.astype(F32)
        d = (jnp.dot(piece, u, preferred_element_type=F32) if x_left
             else jnp.dot(u, piece, preferred_element_type=F32))
        acc = d if acc is None else acc + d
    return acc


def _tri(n, kind):
    r = lax.broadcasted_iota(jnp.int32, (n, n), 0)
    c = lax.broadcasted_iota(jnp.int32, (n, n), 1)
    return jnp.where(r >= c if kind == "ge" else r <= c, 1.0, 0.0).astype(BF16)


def _mm_nn(name, a_list, g_list, w, resid=None, tm=512, tn_cap=1024):
    s_len = a_list[0].shape[0]
    k_dim, n_dim = w.shape
    tm = _tile(s_len, tm, 8)
    tn = _tile(n_dim, tn_cap)
    normed = bool(g_list)

    def body(*refs):
        refs = list(refs)
        a_refs = [refs.pop(0) for _ in a_list]
        g_refs = [refs.pop(0) for _ in g_list]
        w_ref = refs.pop(0)
        r_ref = refs.pop(0) if resid is not None else None
        o_ref = refs.pop(0)
        if normed:
            h_ref = refs.pop(0)

            @pl.when(pl.program_id(1) == 0)
            def _():
                off = 0
                for a_ref, g_ref in zip(a_refs, g_refs):
                    xv = a_ref[...]
                    kk = xv.shape[1]
                    r = lax.rsqrt(jnp.mean(xv * xv, axis=1, keepdims=True) + EPS)
                    h_ref[:, off:off + kk] = (xv * r * g_ref[...]).astype(BF16)
                    off += kk

            a = h_ref[...]
        else:
            a = a_refs[0][...]
        acc = jnp.dot(a, w_ref[...], preferred_element_type=F32)
        if r_ref is not None:
            acc = acc + r_ref[...]
        o_ref[...] = acc

    in_specs = [pl.BlockSpec((tm, a.shape[1]), lambda i, j: (i, 0)) for a in a_list]
    in_specs += [pl.BlockSpec((1, g.shape[1]), lambda i, j: (0, 0)) for g in g_list]
    in_specs += [pl.BlockSpec((k_dim, tn), lambda i, j: (0, j))]
    args = list(a_list) + list(g_list) + [w]
    if resid is not None:
        in_specs.append(pl.BlockSpec((tm, tn), lambda i, j: (i, j)))
        args.append(resid)
    out_shape = [jax.ShapeDtypeStruct((s_len, n_dim), F32)]
    out_specs = [pl.BlockSpec((tm, tn), lambda i, j: (i, j))]
    if normed:
        out_shape.append(jax.ShapeDtypeStruct((s_len, k_dim), BF16))
        out_specs.append(pl.BlockSpec((tm, k_dim), lambda i, j: (i, 0)))
    res = pl.pallas_call(
        body, name=name, grid=(s_len // tm, n_dim // tn), in_specs=in_specs, out_specs=out_specs,
        out_shape=out_shape, compiler_params=_params(("parallel", "arbitrary")))(*args)
    return res if normed else res[0]


def _rms_bwd(dh, xv, gv, r=None):
    if r is None:
        r = lax.rsqrt(jnp.mean(xv * xv, axis=1, keepdims=True) + EPS)
    xhat = xv * r
    dxh = dh * gv
    dx = r * (dxh - xhat * jnp.mean(dxh * xhat, axis=1, keepdims=True))
    return dx, dh * xhat


def _mm_nt(name, da, w, mode="plain", xs=(), gs=(), dres=None, tm=256, tk_cap=1024):
    s_len, kc = da.shape
    n_out = w.shape[0]
    tm = _tile(s_len, tm, 8)
    tk = _tile(kc, tk_cap)
    nk = kc // tk

    def body(*refs):
        refs = list(refs)
        da_ref, w_ref = refs.pop(0), refs.pop(0)
        x_refs = [refs.pop(0) for _ in xs]
        g_refs = [refs.pop(0) for _ in gs]
        dres_ref = refs.pop(0) if dres is not None else None
        acc_ref = refs.pop()
        out_refs = refs
        i, k = pl.program_id(0), pl.program_id(1)

        @pl.when(k == 0)
        def _():
            acc_ref[...] = jnp.zeros_like(acc_ref)

        acc_ref[...] += lax.dot_general(da_ref[...].astype(BF16), w_ref[...], NT_DIMS,
                                        preferred_element_type=F32)

        @pl.when(k == nk - 1)
        def _():
            if mode == "plain":
                out_refs[0][...] = acc_ref[...]
                return
            n_x = len(xs)
            dx_refs, dg_refs = out_refs[:n_x], out_refs[n_x:]

            @pl.when(i == 0)
            def _():
                for dg_ref in dg_refs:
                    dg_ref[...] = jnp.zeros_like(dg_ref)

            off = 0
            for x_ref, g_ref, dx_ref, dg_ref in zip(x_refs, g_refs, dx_refs, dg_refs):
                kk = x_ref.shape[1]
                dx, dgp = _rms_bwd(acc_ref[:, off:off + kk], x_ref[...], g_ref[...])
                if dres_ref is not None:
                    dx = dx + dres_ref[...]
                dx_ref[...] = dx
                dg_ref[...] += jnp.sum(dgp, axis=0, keepdims=True)
                off += kk

    in_specs = [pl.BlockSpec((tm, tk), lambda i, k: (i, k)), pl.BlockSpec((n_out, tk), lambda i, k: (0, k))]
    in_specs += [pl.BlockSpec((tm, x.shape[1]), lambda i, k: (i, 0)) for x in xs]
    in_specs += [pl.BlockSpec((1, g.shape[1]), lambda i, k: (0, 0)) for g in gs]
    args = [da, w] + list(xs) + list(gs)
    if dres is not None:
        in_specs.append(pl.BlockSpec((tm, n_out), lambda i, k: (i, 0)))
        args.append(dres)
    if mode == "plain":
        out_shape = [jax.ShapeDtypeStruct((s_len, n_out), F32)]
        out_specs = [pl.BlockSpec((tm, n_out), lambda i, k: (i, 0))]
    else:
        out_shape = [jax.ShapeDtypeStruct((s_len, x.shape[1]), F32) for x in xs]
        out_specs = [pl.BlockSpec((tm, x.shape[1]), lambda i, k: (i, 0)) for x in xs]
        out_shape += [jax.ShapeDtypeStruct((1, x.shape[1]), F32) for x in xs]
        out_specs += [pl.BlockSpec((1, x.shape[1]), lambda i, k: (0, 0)) for x in xs]
    res = pl.pallas_call(
        body, name=name, grid=(s_len // tm, nk), in_specs=in_specs, out_specs=out_specs, out_shape=out_shape,
        scratch_shapes=[pltpu.VMEM((tm, n_out), F32)],
        compiler_params=_params(("arbitrary", "arbitrary")))(*args)
    return res[0] if mode == "plain" else res


def _mm_tn(name, a, b, tk_cap=512, tn_cap=1024, tm=1024):
    s_len, k_dim = a.shape
    n_dim = b.shape[1]
    tk = _tile(k_dim, tk_cap)
    tn = _tile(n_dim, tn_cap)
    tm = _tile(s_len, tm, 8)
    nm = s_len // tm

    def body(a_ref, b_ref, o_ref, acc_ref):
        m = pl.program_id(2)

        @pl.when(m == 0)
        def _():
            acc_ref[...] = jnp.zeros_like(acc_ref)

        acc_ref[...] += lax.dot_general(a_ref[...].astype(BF16), b_ref[...].astype(BF16), TN_DIMS,
                                        preferred_element_type=F32)

        @pl.when(m == nm - 1)
        def _():
            o_ref[...] = acc_ref[...].astype(BF16)

    return pl.pallas_call(
        body, name=name, grid=(k_dim // tk, n_dim // tn, nm),
        in_specs=[pl.BlockSpec((tm, tk), lambda i, j, m: (m, i)), pl.BlockSpec((tm, tn), lambda i, j, m: (m, j))],
        out_specs=pl.BlockSpec((tk, tn), lambda i, j, m: (i, j)),
        out_shape=jax.ShapeDtypeStruct((k_dim, n_dim), BF16),
        scratch_shapes=[pltpu.VMEM((tk, tn), F32)],
        compiler_params=_params(("parallel", "parallel", "arbitrary")))(a, b)


def _neg_softplus(z):
    e = jnp.exp(-jnp.abs(z))
    return -(jnp.maximum(z, 0.0) + jnp.log(1.0 + e)), e


def _prep_fwd(name, proj, qg, kg, bf, n_heads, tm=256):
    s_len, n_p = proj.shape
    w_dim = n_heads * HEAD_DIM
    tm = _tile(s_len, tm, 8)

    def body(p_ref, qg_ref, kg_ref, bf_ref, pa_ref, c_ref, carry_ref):
        @pl.when(pl.program_id(0) == 0)
        def _():
            carry_ref[...] = jnp.zeros_like(carry_ref)

        for base, g_ref in ((0, qg_ref), (w_dim, kg_ref)):
            for hh in range(n_heads):
                sl = slice(base + hh * HEAD_DIM, base + (hh + 1) * HEAD_DIM)
                xv = p_ref[:, sl]
                r = lax.rsqrt(jnp.mean(xv * xv, axis=1, keepdims=True) + EPS)
                pa_ref[:, sl] = (xv * r * g_ref[...]).astype(BF16)
        pa_ref[:, 2 * w_dim:] = p_ref[:, 2 * w_dim:6 * w_dim].astype(BF16)
        f = p_ref[:, 6 * w_dim:] + bf_ref[...]
        lf, _ = _neg_softplus(-f)
        lane = lax.broadcasted_iota(jnp.int32, lf.shape, 1)
        lf = jnp.where(lane < n_heads, lf, 0.0)
        cb = _split_dot(lf, _tri(tm, "ge"), 3, x_left=False) + carry_ref[...]
        c_ref[...] = cb
        carry_ref[...] = cb[tm - 1:tm, :]

    return pl.pallas_call(
        body, name=name, grid=(s_len // tm,),
        in_specs=[pl.BlockSpec((tm, n_p), lambda i: (i, 0))] + [pl.BlockSpec((1, LANES), lambda i: (0, 0))] * 3,
        out_specs=[pl.BlockSpec((tm, 6 * w_dim), lambda i: (i, 0)), pl.BlockSpec((tm, LANES), lambda i: (i, 0))],
        out_shape=[jax.ShapeDtypeStruct((s_len, 6 * w_dim), BF16), jax.ShapeDtypeStruct((s_len, LANES), F32)],
        scratch_shapes=[pltpu.VMEM((1, LANES), F32)],
        compiler_params=_params(("arbitrary",)))(proj, qg, kg, bf)


def _prep_bwd(name, proj, qg, kg, bf, d_fox, d_sb, dc, n_heads, tm=256):
    s_len, n_p = proj.shape
    w_dim = n_heads * HEAD_DIM
    tm = _tile(s_len, tm, 8)
    nb = s_len // tm

    def body(p_ref, qg_ref, kg_ref, bf_ref, dqa_ref, dka_ref, dva_ref, dqb_ref, dkb_ref, dvb_ref, dc_ref,
             dp_ref, dqg_ref, dkg_ref, dbf_ref, carry_ref):
        @pl.when(pl.program_id(0) == 0)
        def _():
            for ref in (carry_ref, dqg_ref, dkg_ref, dbf_ref):
                ref[...] = jnp.zeros_like(ref)

        for base, g_ref, d_ref, dg_ref in ((0, qg_ref, dqa_ref, dqg_ref), (w_dim, kg_ref, dka_ref, dkg_ref)):
            dg = jnp.zeros((1, HEAD_DIM), F32)
            for hh in range(n_heads):
                sl = slice(base + hh * HEAD_DIM, base + (hh + 1) * HEAD_DIM)
                dx, dgp = _rms_bwd(d_ref[:, hh * HEAD_DIM:(hh + 1) * HEAD_DIM], p_ref[:, sl], g_ref[...])
                dp_ref[:, sl] = dx.astype(BF16)
                dg = dg + jnp.sum(dgp, axis=0, keepdims=True)
            dg_ref[...] += dg
        for n, d_ref in enumerate((dva_ref, dqb_ref, dkb_ref, dvb_ref)):
            dp_ref[:, (2 + n) * w_dim:(3 + n) * w_dim] = d_ref[...].astype(BF16)
        dlf = _split_dot(dc_ref[...], _tri(tm, "le"), 3, x_left=False) + carry_ref[...]
        carry_ref[...] = dlf[0:1, :]
        f = p_ref[:, 6 * w_dim:] + bf_ref[...]
        e = jnp.exp(-jnp.abs(f))
        sig_neg = jnp.where(f >= 0, e, 1.0) / (1.0 + e)
        lane = lax.broadcasted_iota(jnp.int32, f.shape, 1)
        df = jnp.where(lane < n_heads, dlf * sig_neg, 0.0)
        dp_ref[:, 6 * w_dim:] = df.astype(BF16)
        dbf_ref[...] += jnp.sum(df, axis=0, keepdims=True)

    rev = lambda i: (nb - 1 - i, 0)
    vec = pl.BlockSpec((1, LANES), lambda i: (0, 0))
    return pl.pallas_call(
        body, name=name, grid=(nb,),
        in_specs=[pl.BlockSpec((tm, n_p), rev), vec, vec, vec] + [pl.BlockSpec((tm, w_dim), rev)] * 6
        + [pl.BlockSpec((tm, LANES), rev)],
        out_specs=[pl.BlockSpec((tm, n_p), rev), vec, vec, vec],
        out_shape=[jax.ShapeDtypeStruct((s_len, n_p), BF16)] + [jax.ShapeDtypeStruct((1, LANES), F32)] * 3,
        scratch_shapes=[pltpu.VMEM((1, LANES), F32)],
        compiler_params=_params(("arbitrary",)))(proj, qg, kg, bf, *d_fox, *d_sb, dc)


def _head_col(c_blk, h):
    lane = lax.broadcasted_iota(jnp.int32, c_blk.shape, 1)
    return jnp.sum(jnp.where(lane == h, c_blk, 0.0), axis=1, keepdims=True)


def _fox_fwd(name, pa, c, ct, n_heads, tq=512):
    s_len = pa.shape[0]
    tq = _tile(s_len, tq, LANES)
    scale = HEAD_DIM ** -0.5
    hp = ct.shape[0]

    def body(q_ref, k_ref, v_ref, c_ref, ct_ref, o_ref, lse_ref):
        h, qi = pl.program_id(0), pl.program_id(1)
        q = q_ref[...]
        cq = _head_col(c_ref[...], h)
        row = lax.broadcasted_iota(jnp.int32, (tq, tq), 0)
        col = lax.broadcasted_iota(jnp.int32, (tq, tq), 1)

        def step(kb, carry, masked):
            m, l, acc = carry
            ks = pl.multiple_of(kb * tq, tq)
            k = k_ref[pl.ds(ks, tq), :]
            v = v_ref[pl.ds(ks, tq), :]
            ck = ct_ref[pl.ds(h, 1), pl.ds(ks, tq)]
            s = lax.dot_general(q, k, NT_DIMS, preferred_element_type=F32) * scale + (cq - ck)
            if masked:
                s = jnp.where(col <= row, s, -jnp.inf)
            m_new = jnp.maximum(m, jnp.max(s, axis=1, keepdims=True))
            alpha = jnp.exp(m - m_new)
            p = jnp.exp(s - m_new)
            l = alpha * l + jnp.sum(p, axis=1, keepdims=True)
            acc = alpha * acc + jnp.dot(p.astype(BF16), v, preferred_element_type=F32)
            return m_new, l, acc

        init = (jnp.full((tq, 1), -jnp.inf, F32), jnp.zeros((tq, 1), F32), jnp.zeros((tq, HEAD_DIM), F32))
        carry = lax.fori_loop(0, qi, lambda kb, cr: step(kb, cr, False), init)
        m, l, acc = step(qi, carry, True)
        o_ref[...] = acc / l
        lse_ref[0] = m + jnp.log(l)

    return pl.pallas_call(
        body, name=name, grid=(n_heads, s_len // tq),
        in_specs=[pl.BlockSpec((tq, HEAD_DIM), lambda h, i: (i, h)),
                  pl.BlockSpec((s_len, HEAD_DIM), lambda h, i: (0, n_heads + h)),
                  pl.BlockSpec((s_len, HEAD_DIM), lambda h, i: (0, 2 * n_heads + h)),
                  pl.BlockSpec((tq, LANES), lambda h, i: (i, 0)),
                  pl.BlockSpec((hp, s_len), lambda h, i: (0, 0))],
        out_specs=[pl.BlockSpec((tq, HEAD_DIM), lambda h, i: (i, h)),
                   pl.BlockSpec((1, tq, 1), lambda h, i: (h, i, 0))],
        out_shape=[jax.ShapeDtypeStruct((s_len, n_heads * HEAD_DIM), F32),
                   jax.ShapeDtypeStruct((n_heads, s_len, 1), F32)],
        compiler_params=_params(("parallel", "arbitrary")))(pa, pa, pa, c, ct)


def _fox_bwd(name, pa, c, ct, o, do, lse, n_heads, tq=512):
    s_len = pa.shape[0]
    tq = _tile(s_len, tq, LANES)
    scale = HEAD_DIM ** -0.5
    hp = ct.shape[0]
    w_dim = n_heads * HEAD_DIM

    def body(q_ref, k_ref, v_ref, c_ref, ct_ref, o_ref, do_ref, lse_ref, dq_ref, dk_ref, dv_ref, dct_ref, dcq_ref):
        h, qi = pl.program_id(0), pl.program_id(1)

        @pl.when(qi == 0)
        def _():
            dk_ref[...] = jnp.zeros_like(dk_ref)
            dv_ref[...] = jnp.zeros_like(dv_ref)

        @pl.when((qi == 0) & (h == 0))
        def _():
            dct_ref[...] = jnp.zeros_like(dct_ref)

        q = q_ref[...]
        do32 = do_ref[...]
        dob = do32.astype(BF16)
        dsum = jnp.sum(do32 * o_ref[...], axis=1, keepdims=True)
        lse_v = lse_ref[0]
        cq = _head_col(c_ref[...], h)
        row = lax.broadcasted_iota(jnp.int32, (tq, tq), 0)
        col = lax.broadcasted_iota(jnp.int32, (tq, tq), 1)

        def step(kb, carry, masked):
            dq, dcq = carry
            ks = pl.multiple_of(kb * tq, tq)
            k = k_ref[pl.ds(ks, tq), :]
            v = v_ref[pl.ds(ks, tq), :]
            ck = ct_ref[pl.ds(h, 1), pl.ds(ks, tq)]
            s = lax.dot_general(q, k, NT_DIMS, preferred_element_type=F32) * scale + (cq - ck)
            p = jnp.exp(s - lse_v)
            if masked:
                p = jnp.where(col <= row, p, 0.0)
            dp = lax.dot_general(dob, v, NT_DIMS, preferred_element_type=F32)
            ds = p * (dp - dsum)
            dsb = ds.astype(BF16)
            dk_ref[pl.ds(ks, tq), :] += lax.dot_general(dsb, q, TN_DIMS, preferred_element_type=F32) * scale
            dv_ref[pl.ds(ks, tq), :] += lax.dot_general(p.astype(BF16), dob, TN_DIMS, preferred_element_type=F32)
            dct_ref[pl.ds(h, 1), pl.ds(ks, tq)] -= jnp.sum(ds, axis=0, keepdims=True)
            return dq + jnp.dot(dsb, k, preferred_element_type=F32), dcq + jnp.sum(ds, axis=1, keepdims=True)

        init = (jnp.zeros((tq, HEAD_DIM), F32), jnp.zeros((tq, 1), F32))
        carry = lax.fori_loop(0, qi, lambda kb, cr: step(kb, cr, False), init)
        dq, dcq = step(qi, carry, True)
        dq_ref[...] = dq * scale
        dcq_ref[0] = dcq

    blk = pl.BlockSpec((tq, HEAD_DIM), lambda h, i: (i, h))
    full = pl.BlockSpec((s_len, HEAD_DIM), lambda h, i: (0, h))
    return pl.pallas_call(
        body, name=name, grid=(n_heads, s_len // tq),
        in_specs=[blk,
                  pl.BlockSpec((s_len, HEAD_DIM), lambda h, i: (0, n_heads + h)),
                  pl.BlockSpec((s_len, HEAD_DIM), lambda h, i: (0, 2 * n_heads + h)),
                  pl.BlockSpec((tq, LANES), lambda h, i: (i, 0)),
                  pl.BlockSpec((hp, s_len), lambda h, i: (0, 0)),
                  blk, blk,
                  pl.BlockSpec((1, tq, 1), lambda h, i: (h, i, 0))],
        out_specs=[blk, full, full, pl.BlockSpec((hp, s_len), lambda h, i: (0, 0)),
                   pl.BlockSpec((1, tq, 1), lambda h, i: (h, i, 0))],
        out_shape=[jax.ShapeDtypeStruct((s_len, w_dim), F32)] * 3 + [jax.ShapeDtypeStruct((hp, s_len), F32),
                                                                     jax.ShapeDtypeStruct((n_heads, s_len, 1), F32)],
        compiler_params=_params(("arbitrary", "arbitrary")))(pa, pa, pa, c, ct, o, do, lse)


def _sb_fwd(name, pa, n_heads, tq=512, tk=256):
    s_len = pa.shape[0]
    tq = _tile(s_len, tq, LANES)
    tk = _tile(tq, tk, LANES)
    nsub = tq // tk
    scale = HEAD_DIM ** -0.5

    def body(q_ref, k_ref, v_ref, o_ref, tot_ref):
        qi = pl.program_id(1)
        q = q_ref[...]
        u = _tri(tk, "ge")
        row = lax.broadcasted_iota(jnp.int32, (tq, tk), 0)
        col = lax.broadcasted_iota(jnp.int32, (tq, tk), 1)

        def block(ks, carry, mask_off):
            r, acc = carry
            k = k_ref[pl.ds(ks, tk), :]
            v = v_ref[pl.ds(ks, tk), :]
            z = lax.dot_general(q, k, NT_DIMS, preferred_element_type=F32) * scale
            a, _ = _neg_softplus(z)
            if mask_off is not None:
                valid = col + mask_off < row
                a = jnp.where(valid, a, 0.0)
            rin = _split_dot(a, u, 2)
            w = jnp.exp(z + (r + rin))
            if mask_off is not None:
                w = jnp.where(valid, w, 0.0)
            acc = acc + jnp.dot(w.astype(BF16), v, preferred_element_type=F32)
            return r + rin[:, 0:1], acc

        carry = (jnp.zeros((tq, 1), F32), jnp.zeros((tq, HEAD_DIM), F32))
        q0 = pl.multiple_of(qi * tq, tq)
        for j in reversed(range(nsub)):
            carry = block(q0 + j * tk, carry, j * tk)
        n_left = qi * nsub
        carry = lax.fori_loop(
            0, n_left, lambda n, cr: block(pl.multiple_of((n_left - 1 - n) * tk, tk), cr, None), carry)
        r, acc = carry
        o_ref[...] = acc
        tot_ref[0] = r

    return pl.pallas_call(
        body, name=name, grid=(n_heads, s_len // tq),
        in_specs=[pl.BlockSpec((tq, HEAD_DIM), lambda h, i: (i, 3 * n_heads + h)),
                  pl.BlockSpec((s_len, HEAD_DIM), lambda h, i: (0, 4 * n_heads + h)),
                  pl.BlockSpec((s_len, HEAD_DIM), lambda h, i: (0, 5 * n_heads + h))],
        out_specs=[pl.BlockSpec((tq, HEAD_DIM), lambda h, i: (i, h)),
                   pl.BlockSpec((1, tq, 1), lambda h, i: (h, i, 0))],
        out_shape=[jax.ShapeDtypeStruct((s_len, n_heads * HEAD_DIM), F32),
                   jax.ShapeDtypeStruct((n_heads, s_len, 1), F32)],
        compiler_params=_params(("parallel", "arbitrary")))(pa, pa, pa)


def _sb_bwd(name, pa, do, tot, n_heads, tq=512, tk=256):
    s_len = pa.shape[0]
    tq = _tile(s_len, tq, LANES)
    tk = _tile(tq, tk, LANES)
    nsub = tq // tk
    scale = HEAD_DIM ** -0.5
    w_dim = n_heads * HEAD_DIM

    def body(q_ref, k_ref, v_ref, do_ref, tot_ref, dq_ref, dk_ref, dv_ref):
        qi = pl.program_id(1)

        @pl.when(qi == 0)
        def _():
            dk_ref[...] = jnp.zeros_like(dk_ref)
            dv_ref[...] = jnp.zeros_like(dv_ref)

        q = q_ref[...]
        dob = do_ref[...].astype(BF16)
        u = _tri(tk, "le")
        row = lax.broadcasted_iota(jnp.int32, (tq, tk), 0)
        col = lax.broadcasted_iota(jnp.int32, (tq, tk), 1)

        def block(ks, carry, mask_off):
            rem, cpre, dq = carry
            k = k_ref[pl.ds(ks, tk), :]
            v = v_ref[pl.ds(ks, tk), :]
            z = lax.dot_general(q, k, NT_DIMS, preferred_element_type=F32) * scale
            a, e = _neg_softplus(z)
            if mask_off is not None:
                valid = col + mask_off < row
                a = jnp.where(valid, a, 0.0)
            pin = _split_dot(a, u, 2)
            w = jnp.exp(z + (rem - (pin - a)))
            if mask_off is not None:
                w = jnp.where(valid, w, 0.0)
            g = w * lax.dot_general(dob, v, NT_DIMS, preferred_element_type=F32)
            cin = _split_dot(g, u, 2)
            beta = jnp.where(z >= 0, 1.0, e) / (1.0 + e)
            dz = g - beta * (cpre + cin)
            if mask_off is not None:
                dz = jnp.where(valid, dz, 0.0)
            dzb = dz.astype(BF16)
            dk_ref[pl.ds(ks, tk), :] += lax.dot_general(dzb, q, TN_DIMS, preferred_element_type=F32) * scale
            dv_ref[pl.ds(ks, tk), :] += lax.dot_general(w.astype(BF16), dob, TN_DIMS, preferred_element_type=F32)
            dq = dq + jnp.dot(dzb, k, preferred_element_type=F32)
            return rem - pin[:, tk - 1:tk], cpre + cin[:, tk - 1:tk], dq

        carry = (tot_ref[0], jnp.zeros((tq, 1), F32), jnp.zeros((tq, HEAD_DIM), F32))
        carry = lax.fori_loop(0, qi * nsub, lambda n, cr: block(pl.multiple_of(n * tk, tk), cr, None), carry)
        q0 = pl.multiple_of(qi * tq, tq)
        for j in range(nsub):
            carry = block(q0 + j * tk, carry, j * tk)
        dq_ref[...] = carry[2] * scale

    blk = pl.BlockSpec((tq, HEAD_DIM), lambda h, i: (i, h))
    full = pl.BlockSpec((s_len, HEAD_DIM), lambda h, i: (0, h))
    return pl.pallas_call(
        body, name=name, grid=(n_heads, s_len // tq),
        in_specs=[pl.BlockSpec((tq, HEAD_DIM), lambda h, i: (i, 3 * n_heads + h)),
                  pl.BlockSpec((s_len, HEAD_DIM), lambda h, i: (0, 4 * n_heads + h)),
                  pl.BlockSpec((s_len, HEAD_DIM), lambda h, i: (0, 5 * n_heads + h)),
                  blk,
                  pl.BlockSpec((1, tq, 1), lambda h, i: (h, i, 0))],
        out_specs=[blk, full, full],
        out_shape=[jax.ShapeDtypeStruct((s_len, w_dim), F32)] * 3,
        compiler_params=_params(("arbitrary", "arbitrary")))(pa, pa, pa, do, tot)


def _conv_rows(ext_ref, u_ref, halo_ref, first, tm):
    ext_ref[0:8, :] = jnp.where(first, 0.0, halo_ref[...])
    ext_ref[8:, :] = u_ref[...]
    return ext_ref[8:8 + tm, :], ext_ref[7:7 + tm, :], ext_ref[6:6 + tm, :]


def _sigmoid(x):
    return 1.0 / (1.0 + jnp.exp(-x))


def _conv_specs(tm, tc, nj, order):
    hb = tm // 8
    ij = order

    def at(f):
        return lambda *g: f(*ij(*g))

    return [pl.BlockSpec((tm, tc), at(lambda i, j: (i, j))),
            pl.BlockSpec((tm, tc), at(lambda i, j: (i, j + nj))),
            pl.BlockSpec((8, tc), at(lambda i, j: (jnp.maximum(i * hb - 1, 0), j))),
            pl.BlockSpec((8, tc), at(lambda i, j: (jnp.maximum(i * hb - 1, 0), j + nj))),
            pl.BlockSpec((3, tc), at(lambda i, j: (0, j))),
            pl.BlockSpec((3, tc), at(lambda i, j: (0, j + nj))),
            pl.BlockSpec((1, tc), at(lambda i, j: (0, j))),
            pl.BlockSpec((1, tc), at(lambda i, j: (0, j + nj)))]


def _conv_gate_fwd(name, u, cw, cb, tm=512, tc=512):
    s_len, f2 = u.shape
    f_dim = f2 // 2
    tm = _tile(s_len, tm, 8)
    tc = _tile(f_dim, tc)
    nj = f_dim // tc

    def body(ug_ref, uv_ref, hg_ref, hv_ref, wg_ref, wv_ref, bg_ref, bv_ref, g_ref, ext_ref):
        first = pl.program_id(0) == 0

        def conv(u_ref, h_ref, w_ref, b_ref):
            u0, u1, u2 = _conv_rows(ext_ref, u_ref, h_ref, first, tm)
            return w_ref[2:3, :] * u0 + w_ref[1:2, :] * u1 + w_ref[0:1, :] * u2 + b_ref[...]

        gc = conv(ug_ref, hg_ref, wg_ref, bg_ref)
        vc = conv(uv_ref, hv_ref, wv_ref, bv_ref)
        g_ref[...] = (gc * _sigmoid(gc) * vc).astype(BF16)

    return pl.pallas_call(
        body, name=name, grid=(s_len // tm, nj),
        in_specs=_conv_specs(tm, tc, nj, lambda i, j: (i, j)),
        out_specs=pl.BlockSpec((tm, tc), lambda i, j: (i, j)),
        out_shape=jax.ShapeDtypeStruct((s_len, f_dim), BF16),
        scratch_shapes=[pltpu.VMEM((tm + 8, tc), F32)],
        compiler_params=_params(("parallel", "parallel")))(u, u, u, u, cw, cw, cb, cb)


def _conv_gate_bwd(name, u, dg, cw, cb, tm=512, tc=512):
    s_len, f2 = u.shape
    f_dim = f2 // 2
    tm = _tile(s_len, tm, 8)
    tc = _tile(f_dim, tc)
    nj = f_dim // tc

    def body(ug_ref, uv_ref, hg_ref, hv_ref, wg_ref, wv_ref, bg_ref, bv_ref, dg_ref, duc_ref, dcw_ref, dcb_ref,
             eg_ref, ev_ref):
        first = pl.program_id(1) == 0

        @pl.when(first)
        def _():
            dcw_ref[...] = jnp.zeros_like(dcw_ref)
            dcb_ref[...] = jnp.zeros_like(dcb_ref)

        ug = _conv_rows(eg_ref, ug_ref, hg_ref, first, tm)
        uv = _conv_rows(ev_ref, uv_ref, hv_ref, first, tm)
        gc = wg_ref[2:3, :] * ug[0] + wg_ref[1:2, :] * ug[1] + wg_ref[0:1, :] * ug[2] + bg_ref[...]
        vc = wv_ref[2:3, :] * uv[0] + wv_ref[1:2, :] * uv[1] + wv_ref[0:1, :] * uv[2] + bv_ref[...]
        sg = _sigmoid(gc)
        dgv = dg_ref[...]
        dvc = dgv * (gc * sg)
        dgc = dgv * vc * (sg * (1.0 + gc * (1.0 - sg)))
        duc_ref[0] = dgc
        duc_ref[1] = dvc
        for half, (d, us) in enumerate(((dgc, ug), (dvc, uv))):
            dcb_ref[half] += jnp.sum(d, axis=0, keepdims=True)
            for tap in range(3):
                dcw_ref[half, tap:tap + 1, :] += jnp.sum(d * us[2 - tap], axis=0, keepdims=True)

    order = lambda j, i: (i, j)
    return pl.pallas_call(
        body, name=name, grid=(nj, s_len // tm),
        in_specs=_conv_specs(tm, tc, nj, order) + [pl.BlockSpec((tm, tc), lambda j, i: (i, j))],
        out_specs=[pl.BlockSpec((2, tm, tc), lambda j, i: (0, i, j)),
                   pl.BlockSpec((2, 3, tc), lambda j, i: (0, 0, j)),
                   pl.BlockSpec((2, 1, tc), lambda j, i: (0, 0, j))],
        out_shape=[jax.ShapeDtypeStruct((2, s_len, f_dim), F32), jax.ShapeDtypeStruct((2, 3, f_dim), F32),
                   jax.ShapeDtypeStruct((2, 1, f_dim), F32)],
        scratch_shapes=[pltpu.VMEM((tm + 8, tc), F32), pltpu.VMEM((tm + 8, tc), F32)],
        compiler_params=_params(("parallel", "arbitrary")))(u, u, u, u, cw, cw, cb, cb, dg)


def _conv_t(name, duc, cw, tm=512, tc=512):
    _, s_len, f_dim = duc.shape
    tm = _tile(s_len, tm, 8)
    tc = _tile(f_dim, tc)
    nj = f_dim // tc
    nb = s_len // tm
    hb = tm // 8

    def body(d_ref, halo_ref, w_ref, o_ref, ext_ref):
        last = pl.program_id(1) == nb - 1
        ext_ref[0:tm, :] = d_ref[0]
        ext_ref[tm:, :] = jnp.where(last, 0.0, halo_ref[0])
        o_ref[...] = (w_ref[2:3, :] * ext_ref[0:tm, :] + w_ref[1:2, :] * ext_ref[1:1 + tm, :]
                      + w_ref[0:1, :] * ext_ref[2:2 + tm, :]).astype(BF16)

    return pl.pallas_call(
        body, name=name, grid=(2, nb, nj),
        in_specs=[pl.BlockSpec((1, tm, tc), lambda p, i, j: (p, i, j)),
                  pl.BlockSpec((1, 8, tc), lambda p, i, j: (p, jnp.minimum((i + 1) * hb, nb * hb - 1), j)),
                  pl.BlockSpec((3, tc), lambda p, i, j: (0, p * nj + j))],
        out_specs=pl.BlockSpec((tm, tc), lambda p, i, j: (i, p * nj + j)),
        out_shape=jax.ShapeDtypeStruct((s_len, 2 * f_dim), BF16),
        scratch_shapes=[pltpu.VMEM((tm + 8, tc), F32)],
        compiler_params=_params(("parallel", "parallel", "parallel")))(duc, duc, cw)


def _loss_head(name, y, tgt, tm=512):
    s_len, d = y.shape
    tm = _tile(s_len, tm, 8)

    def body(y_ref, t_ref, dy_ref, l_ref):
        @pl.when(pl.program_id(0) == 0)
        def _():
            l_ref[...] = jnp.zeros_like(l_ref)

        err = y_ref[...] - t_ref[...]
        dy_ref[...] = err * (1.0 / d)
        l_ref[...] += 0.5 * jnp.sum(jnp.sum(err * err, axis=1, keepdims=True) * (1.0 / d), axis=0, keepdims=True)

    blk = pl.BlockSpec((tm, d), lambda i: (i, 0))
    return pl.pallas_call(
        body, name=name, grid=(s_len // tm,), in_specs=[blk, blk],
        out_specs=[blk, pl.BlockSpec((1, LANES), lambda i: (0, 0))],
        out_shape=[jax.ShapeDtypeStruct((s_len, d), F32), jax.ShapeDtypeStruct((1, LANES), F32)],
        compiler_params=_params(("arbitrary",)))(y, tgt)


def _adamw(name, parts, w, m, v, tr=256):
    n_parts, rows, cols = parts.shape
    tr = _tile(rows, tr, 16)
    c1 = 1.0 - ADAM_B1 ** ADAM_STEP
    c2 = 1.0 - ADAM_B2 ** ADAM_STEP

    def body(p_ref, w_ref, m_ref, v_ref, g_ref, d_ref, nm_ref, nv_ref):
        g = p_ref[0].astype(F32)
        for n in range(1, n_parts):
            g = g + p_ref[n].astype(F32)
        nm = ADAM_B1 * m_ref[...] + (1.0 - ADAM_B1) * g
        nv = ADAM_B2 * v_ref[...] + (1.0 - ADAM_B2) * (g * g)
        g_ref[...] = g
        nm_ref[...] = nm
        nv_ref[...] = nv
        d_ref[...] = -ADAM_LR * ((nm / c1) / (jnp.sqrt(nv / c2) + ADAM_EPS) + ADAM_WD * w_ref[...])

    blk = pl.BlockSpec((tr, cols), lambda i: (i, 0))
    return pl.pallas_call(
        body, name=name, grid=(rows // tr,),
        in_specs=[pl.BlockSpec((n_parts, tr, cols), lambda i: (0, i, 0)), blk, blk, blk],
        out_specs=[blk] * 4, out_shape=[jax.ShapeDtypeStruct((rows, cols), F32)] * 4,
        compiler_params=_params(("parallel",)))(parts, w, m, v)


def _peers():
    x, y, c = lax.axis_index("x"), lax.axis_index("y"), lax.axis_index("c")
    out = []
    for k in range(1, N_DEV):
        fx, fy, fc = (k >> 2) & 1, (k >> 1) & 1, k & 1
        px, py, pc = x ^ fx, y ^ fy, c ^ fc
        out.append((k - 1, (px, py, pc), 4 * px + 2 * py + pc))
    return 4 * x + 2 * y + c, out


def _exchange(name, arrays, scatter):
    n = len(arrays)

    def body(*refs):
        in_refs, out_refs = refs[:n], refs[n:2 * n]
        send_sems, recv_sems, local_sems = refs[2 * n:]
        me, peers = _peers()
        copies = []
        for a, (src, dst) in enumerate(zip(in_refs, out_refs)):
            own = pltpu.make_async_copy(src.at[me] if scatter else src, dst.at[me], local_sems.at[a])
            own.start()
            copies.append(own)
            for k, pos, idx in peers:
                cp = pltpu.make_async_remote_copy(
                    src_ref=src.at[idx] if scatter else src, dst_ref=dst.at[me],
                    send_sem=send_sems.at[a, k], recv_sem=recv_sems.at[a, k], device_id=pos, device_id_type=MESH)
                cp.start()
                copies.append(cp)
        for cp in copies:
            cp.wait()

    any_spec = pl.BlockSpec(memory_space=pl.ANY)
    return pl.pallas_call(
        body, name=name, in_specs=[any_spec] * n, out_specs=[any_spec] * n,
        out_shape=[jax.ShapeDtypeStruct(a.shape if scatter else (N_DEV,) + a.shape, a.dtype) for a in arrays],
        scratch_shapes=[pltpu.SemaphoreType.DMA((n, N_DEV - 1)), pltpu.SemaphoreType.DMA((n, N_DEV - 1)),
                        pltpu.SemaphoreType.DMA((n,))],
        compiler_params=pltpu.CompilerParams(has_side_effects=True))(*arrays)


def _row(vec, width=None):
    vec = vec.reshape(1, -1)
    if width is not None and vec.shape[1] < width:
        vec = jnp.pad(vec, ((0, 0), (0, width - vec.shape[1])))
    return vec


def _local_step(x, tgt, small, big):
    n_layers = len(big)
    d = x.shape[1]
    n_heads = d // (2 * HEAD_DIM)
    saved = []
    for l in range(n_layers):
        sp, bp = small[l], big[l]
        proj, h = _mm_nn(f"l{l}_in_proj", [x], [_row(sp["attn_norm"])], bp["w_in"])
        qg, kg, bf = _row(sp["q_norm"]), _row(sp["k_norm"]), _row(sp["b_forget"], LANES)
        pa, c = _prep_fwd(f"l{l}_prep", proj, qg, kg, bf, n_heads)
        ct = c[:, :n_heads].T
        o_a, lse = _fox_fwd(f"l{l}_fox", pa, c, ct, n_heads)
        o_b, tot = _sb_fwd(f"l{l}_sb", pa, n_heads)
        gfox, gsb = _row(sp["out_norm_fox"]), _row(sp["out_norm_sb"])
        x1, merged = _mm_nn(f"l{l}_out_proj", [o_a, o_b], [gfox, gsb], bp["w_out"], resid=x)
        u, h2 = _mm_nn(f"l{l}_up_proj", [x1], [_row(sp["ffn_norm"])], bp["w_up"])
        cb = _row(sp["conv_b"])
        g = _conv_gate_fwd(f"l{l}_conv_gate", u, bp["conv_w"], cb)
        x2 = _mm_nn(f"l{l}_down_proj", [g], [], bp["w_down"], resid=x1)
        saved.append(dict(x=x, h=h, proj=proj, pa=pa, c=c, ct=ct, o_a=o_a, lse=lse, o_b=o_b, tot=tot,
                          merged=merged, x1=x1, h2=h2, u=u, g=g, qg=qg, kg=kg, bf=bf, gfox=gfox, gsb=gsb, cb=cb))
        x = x2
    dx, loss_part = _loss_head("loss_head", x, tgt)
    grads = [None] * n_layers
    for l in reversed(range(n_layers)):
        sp, bp, sv = small[l], big[l], saved[l]
        gr = {}
        dg = _mm_nt(f"l{l}_d_down_act", dx, bp["w_down"])
        gr["w_down"] = _mm_tn(f"l{l}_d_w_down", sv["g"], dx)
        duc, dcw, dcb = _conv_gate_bwd(f"l{l}_d_conv_gate", sv["u"], dg, bp["conv_w"], sv["cb"])
        gr["conv_w"] = dcw.transpose(1, 0, 2).reshape(3, -1)
        gr["conv_b"] = dcb.reshape(-1)
        du = _conv_t(f"l{l}_d_conv", duc, bp["conv_w"])
        gr["w_up"] = _mm_tn(f"l{l}_d_w_up", sv["h2"], du)
        dx1, dffn = _mm_nt(f"l{l}_d_up_act", du, bp["w_up"], "rms_bwd", [sv["x1"]], [_row(sp["ffn_norm"])], dres=dx)
        gr["ffn_norm"] = dffn.reshape(-1)
        gr["w_out"] = _mm_tn(f"l{l}_d_w_out", sv["merged"], dx1)
        do_a, do_b, dgfox, dgsb = _mm_nt(f"l{l}_d_out_act", dx1, bp["w_out"], "rms2_bwd",
                                         [sv["o_a"], sv["o_b"]], [sv["gfox"], sv["gsb"]])
        gr["out_norm_fox"], gr["out_norm_sb"] = dgfox.reshape(-1), dgsb.reshape(-1)
        dq_a, dk_a, dv_a, dct, dcq = _fox_bwd(f"l{l}_d_fox", sv["pa"], sv["c"], sv["ct"], sv["o_a"], do_a, sv["lse"],
                                              n_heads)
        d_sb = _sb_bwd(f"l{l}_d_sb", sv["pa"], do_b, sv["tot"], n_heads)
        dc = jnp.pad((dct + dcq[:, :, 0]).T, ((0, 0), (0, LANES - n_heads)))
        dproj, dqg, dkg, dbf = _prep_bwd(f"l{l}_d_prep", sv["proj"], sv["qg"], sv["kg"], sv["bf"],
                                         (dq_a, dk_a, dv_a), d_sb, dc, n_heads)
        gr["q_norm"], gr["k_norm"], gr["b_forget"] = dqg.reshape(-1), dkg.reshape(-1), dbf.reshape(-1)[:n_heads]
        gr["w_in"] = _mm_tn(f"l{l}_d_w_in", sv["h"], dproj)
        dx, dattn = _mm_nt(f"l{l}_d_in_act", dproj, bp["w_in"], "rms_bwd", [sv["x"]], [_row(sp["attn_norm"])],
                           dres=dx1)
        gr["attn_norm"] = dattn.reshape(-1)
        grads[l] = gr
    return loss_part, dx, grads


def _w_in_to_internal(w, n_heads):
    w3 = 3 * n_heads * HEAD_DIM
    pad = jnp.zeros(w.shape[:-1] + (LANES - n_heads,), w.dtype)
    return jnp.concatenate([w[..., :w3], w[..., w3 + n_heads:], w[..., w3:w3 + n_heads], pad], axis=-1)


def _w_in_from_internal(w, n_heads):
    w3 = 3 * n_heads * HEAD_DIM
    return jnp.concatenate([w[..., :w3], w[..., 2 * w3:2 * w3 + n_heads], w[..., w3:2 * w3]], axis=-1)


SMALL = ("attn_norm", "b_forget", "q_norm", "k_norm", "out_norm_fox", "out_norm_sb", "ffn_norm", "conv_b")
WEIGHTS = ("attn_norm", "w_in", "b_forget", "q_norm", "k_norm", "out_norm_fox", "out_norm_sb", "w_out", "ffn_norm",
           "w_up", "conv_w", "conv_b", "w_down")


def kernel(x, attn_norm, w_in, b_forget, q_norm, k_norm, out_norm_fox, out_norm_sb, w_out, ffn_norm, w_up, conv_w, conv_b, w_down, loss_target, m_attn_norm, m_w_in, m_b_forget, m_q_norm, m_k_norm, m_out_norm_fox, m_out_norm_sb, m_w_out, m_ffn_norm, m_w_up, m_conv_w, m_conv_b, m_w_down, v_attn_norm, v_w_in, v_b_forget, v_q_norm, v_k_norm, v_out_norm_fox, v_out_norm_sb, v_w_out, v_ffn_norm, v_w_up, v_conv_w, v_conv_b, v_w_down):
    w = dict(attn_norm=attn_norm, w_in=w_in, b_forget=b_forget, q_norm=q_norm, k_norm=k_norm,
             out_norm_fox=out_norm_fox, out_norm_sb=out_norm_sb, w_out=w_out, ffn_norm=ffn_norm, w_up=w_up,
             conv_w=conv_w, conv_b=conv_b, w_down=w_down)
    mom = dict(attn_norm=m_attn_norm, w_in=m_w_in, b_forget=m_b_forget, q_norm=m_q_norm, k_norm=m_k_norm,
               out_norm_fox=m_out_norm_fox, out_norm_sb=m_out_norm_sb, w_out=m_w_out, ffn_norm=m_ffn_norm,
               w_up=m_w_up, conv_w=m_conv_w, conv_b=m_conv_b, w_down=m_w_down)
    var = dict(attn_norm=v_attn_norm, w_in=v_w_in, b_forget=v_b_forget, q_norm=v_q_norm, k_norm=v_k_norm,
               out_norm_fox=v_out_norm_fox, out_norm_sb=v_out_norm_sb, w_out=v_w_out, ffn_norm=v_ffn_norm,
               w_up=v_w_up, conv_w=v_conv_w, conv_b=v_conv_b, w_down=v_w_down)
    n_layers, d = attn_norm.shape
    n_heads = d // (2 * HEAD_DIM)
    me = 4 * lax.axis_index("x") + 2 * lax.axis_index("y") + lax.axis_index("c")

    shards = [w["w_in"].astype(BF16), w["w_out"].astype(BF16), w["w_up"].astype(BF16), w["w_down"].astype(BF16),
              w["conv_w"]]
    g_in, g_out, g_up, g_down, g_cw = _exchange("gather_weights", shards, scatter=False)
    cols = lambda a: a.transpose(1, 2, 0, 3).reshape(a.shape[1], a.shape[2], -1)
    rows = lambda a: a.transpose(1, 0, 2, 3).reshape(a.shape[1], -1, a.shape[3])
    full_in = _w_in_to_internal(cols(g_in), n_heads)
    full_out, full_up, full_down, full_cw = rows(g_out), cols(g_up), rows(g_down), cols(g_cw)
    big = [dict(w_in=full_in[l], w_out=full_out[l], w_up=full_up[l], w_down=full_down[l], conv_w=full_cw[l])
           for l in range(n_layers)]
    small = [{k: w[k][l] for k in SMALL} for l in range(n_layers)]

    loss_part, grad_x, grads = _local_step(x[0], loss_target[0], small, big)

    stack = lambda k: jnp.stack([grads[l][k] for l in range(n_layers)])
    to_cols = lambda a: a.reshape(a.shape[0], a.shape[1], N_DEV, -1).transpose(2, 0, 1, 3)
    to_rows = lambda a: a.reshape(a.shape[0], N_DEV, -1, a.shape[2]).transpose(1, 0, 2, 3)
    sends = [to_cols(_w_in_from_internal(stack("w_in"), n_heads)), to_rows(stack("w_out")), to_cols(stack("w_up")),
             to_rows(stack("w_down"))]
    r_in, r_out, r_up, r_down = _exchange("scatter_grads", sends, scatter=True)
    small_names = SMALL + ("conv_w",)
    flat = jnp.concatenate([loss_part.reshape(-1)] + [
        jnp.pad(stack(k).reshape(-1), (0, (-stack(k).size) % LANES)) for k in small_names])
    flat = flat.reshape(-1, LANES)
    (all_small,) = _exchange("gather_small_grads", [flat], scatter=False)

    out = {}

    def update(name, parts, shape):
        res = _adamw("adamw_" + name, parts, w[name].reshape(parts.shape[1:]), mom[name].reshape(parts.shape[1:]),
                     var[name].reshape(parts.shape[1:]))
        out[name] = [r.reshape(shape) for r in res]

    for name, parts in (("w_in", r_in), ("w_out", r_out), ("w_up", r_up), ("w_down", r_down)):
        update(name, parts.reshape(N_DEV, -1, parts.shape[-1]), w[name].shape)
    n_rows = flat.shape[0]
    w_flat, m_flat, v_flat = [], [], []
    for src, dst in ((w, w_flat), (mom, m_flat), (var, v_flat)):
        dst.append(jnp.zeros((LANES,), F32))
        for k in small_names:
            a = src[k]
            if k == "conv_w":
                a = jnp.zeros((n_layers, 3, conv_w.shape[2] * N_DEV), F32)
            dst.append(jnp.pad(a.reshape(-1), (0, (-a.size) % LANES)))
    pack = lambda parts: jnp.concatenate(parts).reshape(n_rows, LANES)
    res = _adamw("adamw_small", all_small, pack(w_flat), pack(m_flat), pack(v_flat))
    res = [r.reshape(-1) for r in res]
    loss = res[0][0]
    off = LANES
    g_cw_full = None
    for k in small_names:
        size = n_layers * 3 * conv_w.shape[2] * N_DEV if k == "conv_w" else w[k].size
        if k == "conv_w":
            g_cw_full = res[0][off:off + size].reshape(n_layers, 3, -1)
        else:
            out[k] = [r[off:off + size].reshape(w[k].shape) for r in res]
        off += size + (-size) % LANES
    c_loc = conv_w.shape[2]
    g_cw_mine = lax.dynamic_slice_in_dim(g_cw_full, me * c_loc, c_loc, axis=2)
    update("conv_w", g_cw_mine.reshape(1, n_layers * 3, c_loc), conv_w.shape)

    outs = [loss, grad_x[None]]
    for n in range(4):
        outs += [out[k][n] for k in WEIGHTS]
    return tuple(outs)
```

```python
import functools

import jax
import jax.numpy as jnp
from jax import lax
from jax.experimental import pallas as pl
from jax.experimental.pallas import tpu as pltpu

F32 = jnp.float32
BF16 = jnp.bfloat16
HEAD_DIM = 128
QK_SCALE = HEAD_DIM ** -0.5
LANES = 128
EPS = 1e-6
N_DEV = 8
ADAM_LR = 0.001
ADAM_B1 = 0.9
ADAM_B2 = 0.999
ADAM_EPS = 1e-08
ADAM_WD = 0.01
ADAM_STEP = 10
VMEM_LIMIT = 56 * 1024 * 1024
MESH = pl.DeviceIdType.MESH

NT_DIMS = (((1,), (1,)), ((), ()))
TN_DIMS = (((0,), (0,)), ((), ()))


def _tile(n, cap, mult=LANES):
    t = (min(cap, n) // mult) * mult
    while t >= mult:
        if n % t == 0:
            return t
        t -= mult
    return n


def _params(sem, vmem=VMEM_LIMIT):
    return pltpu.CompilerParams(dimension_semantics=sem, vmem_limit_bytes=vmem)


def _split_dot(x, u, n_split, x_left=True):
    acc = None
    rest = x
    for s in range(n_split):
        piece = rest.astype(BF16)
        if s + 1 < n_split:
            rest = rest - piece.astype(F32)
        d = (jnp.dot(piece, u, preferred_element_type=F32) if x_left
             else jnp.dot(u, piece, preferred_element_type=F32))
        acc = d if acc is None else acc + d
    return acc


def _tri(n, kind):
    r = lax.broadcasted_iota(jnp.int32, (n, n), 0)
    c = lax.broadcasted_iota(jnp.int32, (n, n), 1)
    return jnp.where(r >= c if kind == "ge" else r <= c, 1.0, 0.0).astype(BF16)


def _mm_nn(name, a_list, g_list, w, resid=None, tm=1024, tn_cap=1024):
    s_len = a_list[0].shape[0]
    k_dim, n_dim = w.shape
    tm = _tile(s_len, tm, 8)
    tn = _tile(n_dim, tn_cap)
    normed = bool(g_list)

    def body(*refs):
        refs = list(refs)
        a_refs = [refs.pop(0) for _ in a_list]
        g_refs = [refs.pop(0) for _ in g_list]
        w_ref = refs.pop(0)
        r_ref = refs.pop(0) if resid is not None else None
        o_ref = refs.pop(0)
        if normed:
            h_ref = refs.pop(0)

            @pl.when(pl.program_id(1) == 0)
            def _():
                off = 0
                for a_ref, g_ref in zip(a_refs, g_refs):
                    xv = a_ref[...]
                    kk = xv.shape[1]
                    r = lax.rsqrt(jnp.mean(xv * xv, axis=1, keepdims=True) + EPS)
                    h_ref[:, off:off + kk] = (xv * r * g_ref[...]).astype(BF16)
                    off += kk

            a = h_ref[...]
        else:
            a = a_refs[0][...]
        acc = jnp.dot(a, w_ref[...], preferred_element_type=F32)
        if r_ref is not None:
            acc = acc + r_ref[...]
        o_ref[...] = acc

    in_specs = [pl.BlockSpec((tm, a.shape[1]), lambda i, j: (i, 0)) for a in a_list]
    in_specs += [pl.BlockSpec((1, g.shape[1]), lambda i, j: (0, 0)) for g in g_list]
    in_specs += [pl.BlockSpec((k_dim, tn), lambda i, j: (0, j))]
    args = list(a_list) + list(g_list) + [w]
    if resid is not None:
        in_specs.append(pl.BlockSpec((tm, tn), lambda i, j: (i, j)))
        args.append(resid)
    out_shape = [jax.ShapeDtypeStruct((s_len, n_dim), F32)]
    out_specs = [pl.BlockSpec((tm, tn), lambda i, j: (i, j))]
    if normed:
        out_shape.append(jax.ShapeDtypeStruct((s_len, k_dim), BF16))
        out_specs.append(pl.BlockSpec((tm, k_dim), lambda i, j: (i, 0)))
    res = pl.pallas_call(
        body, name=name, grid=(s_len // tm, n_dim // tn), in_specs=in_specs, out_specs=out_specs,
        out_shape=out_shape, compiler_params=_params(("parallel", "arbitrary")))(*args)
    return res if normed else res[0]


def _rms_bwd(dh, xv, gv, r=None):
    if r is None:
        r = lax.rsqrt(jnp.mean(xv * xv, axis=1, keepdims=True) + EPS)
    xhat = xv * r
    dxh = dh * gv
    dx = r * (dxh - xhat * jnp.mean(dxh * xhat, axis=1, keepdims=True))
    return dx, dh * xhat


def _mm_nt(name, da, w, mode="plain", xs=(), gs=(), dres=None, tm=512, tk_cap=1024):
    s_len, kc = da.shape
    n_out = w.shape[0]
    tm = _tile(s_len, tm, 8)
    tk = _tile(kc, tk_cap)
    nk = kc // tk

    def body(*refs):
        refs = list(refs)
        da_ref, w_ref = refs.pop(0), refs.pop(0)
        x_refs = [refs.pop(0) for _ in xs]
        g_refs = [refs.pop(0) for _ in gs]
        dres_ref = refs.pop(0) if dres is not None else None
        acc_ref = refs[0] if mode == "plain" else refs.pop()
        out_refs = refs
        i, k = pl.program_id(0), pl.program_id(1)

        @pl.when(k == 0)
        def _():
            acc_ref[...] = jnp.zeros_like(acc_ref)

        acc_ref[...] += lax.dot_general(da_ref[...].astype(BF16), w_ref[...], NT_DIMS,
                                        preferred_element_type=F32)

        if mode == "plain":
            return

        @pl.when(k == nk - 1)
        def _():
            n_x = len(xs)
            dx_refs, dg_refs = out_refs[:n_x], out_refs[n_x:]

            @pl.when(i == 0)
            def _():
                for dg_ref in dg_refs:
                    dg_ref[...] = jnp.zeros_like(dg_ref)

            off = 0
            for x_ref, g_ref, dx_ref, dg_ref in zip(x_refs, g_refs, dx_refs, dg_refs):
                kk = x_ref.shape[1]
                dx, dgp = _rms_bwd(acc_ref[:, off:off + kk], x_ref[...], g_ref[...])
                if dres_ref is not None:
                    dx = dx + dres_ref[...]
                dx_ref[...] = dx
                dg_ref[...] += jnp.sum(dgp, axis=0, keepdims=True)
                off += kk

    in_specs = [pl.BlockSpec((tm, tk), lambda i, k: (i, k)), pl.BlockSpec((n_out, tk), lambda i, k: (0, k))]
    in_specs += [pl.BlockSpec((tm, x.shape[1]), lambda i, k: (i, 0)) for x in xs]
    in_specs += [pl.BlockSpec((1, g.shape[1]), lambda i, k: (0, 0)) for g in gs]
    args = [da, w] + list(xs) + list(gs)
    if dres is not None:
        in_specs.append(pl.BlockSpec((tm, n_out), lambda i, k: (i, 0)))
        args.append(dres)
    if mode == "plain":
        out_shape = [jax.ShapeDtypeStruct((s_len, n_out), F32)]
        out_specs = [pl.BlockSpec((tm, n_out), lambda i, k: (i, 0))]
    else:
        out_shape = [jax.ShapeDtypeStruct((s_len, x.shape[1]), F32) for x in xs]
        out_specs = [pl.BlockSpec((tm, x.shape[1]), lambda i, k: (i, 0)) for x in xs]
        out_shape += [jax.ShapeDtypeStruct((1, x.shape[1]), F32) for x in xs]
        out_specs += [pl.BlockSpec((1, x.shape[1]), lambda i, k: (0, 0)) for x in xs]
    res = pl.pallas_call(
        body, name=name, grid=(s_len // tm, nk), in_specs=in_specs, out_specs=out_specs, out_shape=out_shape,
        scratch_shapes=[] if mode == "plain" else [pltpu.VMEM((tm, n_out), F32)],
        compiler_params=_params(("arbitrary", "arbitrary")))(*args)
    return res[0] if mode == "plain" else res


def _mm_tn(name, a, b, tk_cap=1024, tn_cap=2048, tm=1024):
    s_len, k_dim = a.shape
    n_dim = b.shape[1]
    tk = _tile(k_dim, tk_cap)
    tn = _tile(n_dim, tn_cap)
    tm = _tile(s_len, tm, 8)
    nm = s_len // tm

    def body(a_ref, b_ref, o_ref, acc_ref):
        m = pl.program_id(2)

        @pl.when(m == 0)
        def _():
            acc_ref[...] = jnp.zeros_like(acc_ref)

        acc_ref[...] += lax.dot_general(a_ref[...].astype(BF16), b_ref[...].astype(BF16), TN_DIMS,
                                        preferred_element_type=F32)

        @pl.when(m == nm - 1)
        def _():
            o_ref[...] = acc_ref[...].astype(BF16)

    return pl.pallas_call(
        body, name=name, grid=(k_dim // tk, n_dim // tn, nm),
        in_specs=[pl.BlockSpec((tm, tk), lambda i, j, m: (m, i)), pl.BlockSpec((tm, tn), lambda i, j, m: (m, j))],
        out_specs=pl.BlockSpec((tk, tn), lambda i, j, m: (i, j)),
        out_shape=jax.ShapeDtypeStruct((k_dim, n_dim), BF16),
        scratch_shapes=[pltpu.VMEM((tk, tn), F32)],
        compiler_params=_params(("parallel", "parallel", "arbitrary")))(a, b)


def _neg_softplus(z):
    e = jnp.exp(-jnp.abs(z))
    return -(jnp.maximum(z, 0.0) + jnp.log(1.0 + e)), e


def _prep_fwd(name, proj, qg, kg, bf, n_heads, tm=256):
    s_len, n_p = proj.shape
    w_dim = n_heads * HEAD_DIM
    tm = _tile(s_len, tm, 8)

    def body(p_ref, qg_ref, kg_ref, bf_ref, pa_ref, c_ref, carry_ref):
        @pl.when(pl.program_id(0) == 0)
        def _():
            carry_ref[...] = jnp.zeros_like(carry_ref)

        for base, g_ref, mul in ((0, qg_ref, QK_SCALE), (w_dim, kg_ref, None)):
            for hh in range(n_heads):
                sl = slice(base + hh * HEAD_DIM, base + (hh + 1) * HEAD_DIM)
                xv = p_ref[:, sl]
                r = lax.rsqrt(jnp.mean(xv * xv, axis=1, keepdims=True) + EPS)
                y = xv * r * g_ref[...]
                pa_ref[:, sl] = (y if mul is None else y * mul).astype(BF16)
        pa_ref[:, 2 * w_dim:3 * w_dim] = p_ref[:, 2 * w_dim:3 * w_dim].astype(BF16)
        pa_ref[:, 3 * w_dim:4 * w_dim] = (p_ref[:, 3 * w_dim:4 * w_dim] * QK_SCALE).astype(BF16)
        pa_ref[:, 4 * w_dim:] = p_ref[:, 4 * w_dim:6 * w_dim].astype(BF16)
        f = p_ref[:, 6 * w_dim:] + bf_ref[...]
        lf, _ = _neg_softplus(-f)
        lane = lax.broadcasted_iota(jnp.int32, lf.shape, 1)
        lf = jnp.where(lane < n_heads, lf, 0.0)
        cb = _split_dot(lf, _tri(tm, "ge"), 3, x_left=False) + carry_ref[...]
        c_ref[...] = cb
        carry_ref[...] = cb[tm - 1:tm, :]

    return pl.pallas_call(
        body, name=name, grid=(s_len // tm,),
        in_specs=[pl.BlockSpec((tm, n_p), lambda i: (i, 0))] + [pl.BlockSpec((1, LANES), lambda i: (0, 0))] * 3,
        out_specs=[pl.BlockSpec((tm, 6 * w_dim), lambda i: (i, 0)), pl.BlockSpec((tm, LANES), lambda i: (i, 0))],
        out_shape=[jax.ShapeDtypeStruct((s_len, 6 * w_dim), BF16), jax.ShapeDtypeStruct((s_len, LANES), F32)],
        scratch_shapes=[pltpu.VMEM((1, LANES), F32)],
        compiler_params=_params(("arbitrary",)))(proj, qg, kg, bf)


def _prep_bwd(name, proj, qg, kg, bf, d_fox, d_sb, dc, n_heads, tm=256):
    s_len, n_p = proj.shape
    w_dim = n_heads * HEAD_DIM
    tm = _tile(s_len, tm, 8)
    nb = s_len // tm

    def body(p_ref, qg_ref, kg_ref, bf_ref, dqa_ref, dka_ref, dva_ref, dqb_ref, dkb_ref, dvb_ref, dc_ref,
             dp_ref, dqg_ref, dkg_ref, dbf_ref, carry_ref):
        @pl.when(pl.program_id(0) == 0)
        def _():
            for ref in (carry_ref, dqg_ref, dkg_ref, dbf_ref):
                ref[...] = jnp.zeros_like(ref)

        for base, g_ref, d_ref, dg_ref in ((0, qg_ref, dqa_ref, dqg_ref), (w_dim, kg_ref, dka_ref, dkg_ref)):
            dg = jnp.zeros((1, HEAD_DIM), F32)
            for hh in range(n_heads):
                sl = slice(base + hh * HEAD_DIM, base + (hh + 1) * HEAD_DIM)
                dx, dgp = _rms_bwd(d_ref[:, hh * HEAD_DIM:(hh + 1) * HEAD_DIM], p_ref[:, sl], g_ref[...])
                dp_ref[:, sl] = dx.astype(BF16)
                dg = dg + jnp.sum(dgp, axis=0, keepdims=True)
            dg_ref[...] += dg
        for n, d_ref in enumerate((dva_ref, dqb_ref, dkb_ref, dvb_ref)):
            dp_ref[:, (2 + n) * w_dim:(3 + n) * w_dim] = d_ref[...].astype(BF16)
        dlf = _split_dot(dc_ref[...], _tri(tm, "le"), 3, x_left=False) + carry_ref[...]
        carry_ref[...] = dlf[0:1, :]
        f = p_ref[:, 6 * w_dim:] + bf_ref[...]
        e = jnp.exp(-jnp.abs(f))
        sig_neg = jnp.where(f >= 0, e, 1.0) / (1.0 + e)
        lane = lax.broadcasted_iota(jnp.int32, f.shape, 1)
        df = jnp.where(lane < n_heads, dlf * sig_neg, 0.0)
        dp_ref[:, 6 * w_dim:] = df.astype(BF16)
        dbf_ref[...] += jnp.sum(df, axis=0, keepdims=True)

    rev = lambda i: (nb - 1 - i, 0)
    vec = pl.BlockSpec((1, LANES), lambda i: (0, 0))
    return pl.pallas_call(
        body, name=name, grid=(nb,),
        in_specs=[pl.BlockSpec((tm, n_p), rev), vec, vec, vec] + [pl.BlockSpec((tm, w_dim), rev)] * 6
        + [pl.BlockSpec((tm, LANES), rev)],
        out_specs=[pl.BlockSpec((tm, n_p), rev), vec, vec, vec],
        out_shape=[jax.ShapeDtypeStruct((s_len, n_p), BF16)] + [jax.ShapeDtypeStruct((1, LANES), F32)] * 3,
        scratch_shapes=[pltpu.VMEM((1, LANES), F32)],
        compiler_params=_params(("arbitrary",)))(proj, qg, kg, bf, *d_fox, *d_sb, dc)


def _head_col(c_blk, h):
    lane = lax.broadcasted_iota(jnp.int32, c_blk.shape, 1)
    return jnp.sum(jnp.where(lane == h, c_blk, 0.0), axis=1, keepdims=True)


def _fox_fwd(name, pa, c, ct, n_heads, tq=1024):
    s_len = pa.shape[0]
    tq = _tile(s_len, tq, LANES)
    hp = ct.shape[0]

    def body(q_ref, k_ref, v_ref, c_ref, ct_ref, o_ref, lse_ref):
        h, qi = pl.program_id(0), pl.program_id(1)
        q = q_ref[...]
        cq = _head_col(c_ref[...], h)
        row = lax.broadcasted_iota(jnp.int32, (tq, tq), 0)
        col = lax.broadcasted_iota(jnp.int32, (tq, tq), 1)

        def step(kb, carry, masked):
            m, l, acc = carry
            ks = pl.multiple_of(kb * tq, tq)
            k = k_ref[pl.ds(ks, tq), :]
            v = v_ref[pl.ds(ks, tq), :]
            ck = ct_ref[pl.ds(h, 1), pl.ds(ks, tq)]
            s = lax.dot_general(q, k, NT_DIMS, preferred_element_type=F32) + (cq - ck)
            if masked:
                s = jnp.where(col <= row, s, -jnp.inf)
            m_new = jnp.maximum(m, jnp.max(s, axis=1, keepdims=True))
            alpha = jnp.exp(m - m_new)
            p = jnp.exp(s - m_new)
            l = alpha * l + jnp.sum(p, axis=1, keepdims=True)
            acc = alpha * acc + jnp.dot(p.astype(BF16), v, preferred_element_type=F32)
            return m_new, l, acc

        init = (jnp.full((tq, 1), -jnp.inf, F32), jnp.zeros((tq, 1), F32), jnp.zeros((tq, HEAD_DIM), F32))
        carry = lax.fori_loop(0, qi, lambda kb, cr: step(kb, cr, False), init)
        m, l, acc = step(qi, carry, True)
        o_ref[...] = acc / l
        lse_ref[0] = m + jnp.log(l)

    return pl.pallas_call(
        body, name=name, grid=(n_heads, s_len // tq),
        in_specs=[pl.BlockSpec((tq, HEAD_DIM), lambda h, i: (i, h)),
                  pl.BlockSpec((s_len, HEAD_DIM), lambda h, i: (0, n_heads + h)),
                  pl.BlockSpec((s_len, HEAD_DIM), lambda h, i: (0, 2 * n_heads + h)),
                  pl.BlockSpec((tq, LANES), lambda h, i: (i, 0)),
                  pl.BlockSpec((hp, s_len), lambda h, i: (0, 0))],
        out_specs=[pl.BlockSpec((tq, HEAD_DIM), lambda h, i: (i, h)),
                   pl.BlockSpec((1, tq, 1), lambda h, i: (h, i, 0))],
        out_shape=[jax.ShapeDtypeStruct((s_len, n_heads * HEAD_DIM), F32),
                   jax.ShapeDtypeStruct((n_heads, s_len, 1), F32)],
        compiler_params=_params(("parallel", "arbitrary")))(pa, pa, pa, c, ct)


def _fox_bwd(name, pa, c, ct, o, do, lse, n_heads, tq=1024):
    s_len = pa.shape[0]
    tq = _tile(s_len, tq, LANES)
    hp = ct.shape[0]
    w_dim = n_heads * HEAD_DIM

    def body(q_ref, k_ref, v_ref, c_ref, ct_ref, o_ref, do_ref, lse_ref, dq_ref, dk_ref, dv_ref, dct_ref, dcq_ref):
        h, qi = pl.program_id(0), pl.program_id(1)

        @pl.when(qi == 0)
        def _():
            dk_ref[...] = jnp.zeros_like(dk_ref)
            dv_ref[...] = jnp.zeros_like(dv_ref)

        @pl.when((qi == 0) & (h == 0))
        def _():
            dct_ref[...] = jnp.zeros_like(dct_ref)

        q = q_ref[...]
        do32 = do_ref[...]
        dob = do32.astype(BF16)
        dsum = jnp.sum(do32 * o_ref[...], axis=1, keepdims=True)
        lse_v = lse_ref[0]
        cq = _head_col(c_ref[...], h)
        row = lax.broadcasted_iota(jnp.int32, (tq, tq), 0)
        col = lax.broadcasted_iota(jnp.int32, (tq, tq), 1)

        def step(kb, carry, masked):
            dq, dcq = carry
            ks = pl.multiple_of(kb * tq, tq)
            k = k_ref[pl.ds(ks, tq), :]
            v = v_ref[pl.ds(ks, tq), :]
            ck = ct_ref[pl.ds(h, 1), pl.ds(ks, tq)]
            s = lax.dot_general(q, k, NT_DIMS, preferred_element_type=F32) + (cq - ck)
            p = jnp.exp(s - lse_v)
            if masked:
                p = jnp.where(col <= row, p, 0.0)
            dp = lax.dot_general(dob, v, NT_DIMS, preferred_element_type=F32)
            ds = p * (dp - dsum)
            dsb = ds.astype(BF16)
            dk_ref[pl.ds(ks, tq), :] += lax.dot_general(dsb, q, TN_DIMS, preferred_element_type=F32)
            dv_ref[pl.ds(ks, tq), :] += lax.dot_general(p.astype(BF16), dob, TN_DIMS, preferred_element_type=F32)
            dct_ref[pl.ds(h, 1), pl.ds(ks, tq)] -= jnp.sum(ds, axis=0, keepdims=True)
            return dq + jnp.dot(dsb, k, preferred_element_type=F32), dcq + jnp.sum(ds, axis=1, keepdims=True)

        init = (jnp.zeros((tq, HEAD_DIM), F32), jnp.zeros((tq, 1), F32))
        carry = lax.fori_loop(0, qi, lambda kb, cr: step(kb, cr, False), init)
        dq, dcq = step(qi, carry, True)
        dq_ref[...] = dq * QK_SCALE
        dcq_ref[0] = dcq

    blk = pl.BlockSpec((tq, HEAD_DIM), lambda h, i: (i, h))
    full = pl.BlockSpec((s_len, HEAD_DIM), lambda h, i: (0, h))
    return pl.pallas_call(
        body, name=name, grid=(n_heads, s_len // tq),
        in_specs=[blk,
                  pl.BlockSpec((s_len, HEAD_DIM), lambda h, i: (0, n_heads + h)),
                  pl.BlockSpec((s_len, HEAD_DIM), lambda h, i: (0, 2 * n_heads + h)),
                  pl.BlockSpec((tq, LANES), lambda h, i: (i, 0)),
                  pl.BlockSpec((hp, s_len), lambda h, i: (0, 0)),
                  blk, blk,
                  pl.BlockSpec((1, tq, 1), lambda h, i: (h, i, 0))],
        out_specs=[blk, full, full, pl.BlockSpec((hp, s_len), lambda h, i: (0, 0)),
                   pl.BlockSpec((1, tq, 1), lambda h, i: (h, i, 0))],
        out_shape=[jax.ShapeDtypeStruct((s_len, w_dim), F32)] * 3 + [jax.ShapeDtypeStruct((hp, s_len), F32),
                                                                     jax.ShapeDtypeStruct((n_heads, s_len, 1), F32)],
        compiler_params=_params(("arbitrary", "arbitrary")))(pa, pa, pa, c, ct, o, do, lse)


def _sb_fwd(name, pa, n_heads, tq=1024, tk=256):
    s_len = pa.shape[0]
    tq = _tile(s_len, tq, LANES)
    tk = _tile(tq, tk, LANES)
    nsub = tq // tk

    def body(q_ref, k_ref, v_ref, o_ref, tot_ref):
        qi = pl.program_id(1)
        q = q_ref[...]
        u = _tri(tk, "ge")
        row = lax.broadcasted_iota(jnp.int32, (tq, tk), 0)
        col = lax.broadcasted_iota(jnp.int32, (tq, tk), 1)

        def block(ks, carry, mask_off):
            r, acc = carry
            k = k_ref[pl.ds(ks, tk), :]
            v = v_ref[pl.ds(ks, tk), :]
            z = lax.dot_general(q, k, NT_DIMS, preferred_element_type=F32)
            a, _ = _neg_softplus(z)
            if mask_off is not None:
                valid = col + mask_off < row
                a = jnp.where(valid, a, 0.0)
            rin = _split_dot(a, u, 2)
            w = jnp.exp(z + (r + rin))
            if mask_off is not None:
                w = jnp.where(valid, w, 0.0)
            acc = acc + jnp.dot(w.astype(BF16), v, preferred_element_type=F32)
            return r + rin[:, 0:1], acc

        carry = (jnp.zeros((tq, 1), F32), jnp.zeros((tq, HEAD_DIM), F32))
        q0 = pl.multiple_of(qi * tq, tq)
        for j in reversed(range(nsub)):
            carry = block(q0 + j * tk, carry, j * tk)
        n_left = qi * nsub
        carry = lax.fori_loop(
            0, n_left, lambda n, cr: block(pl.multiple_of((n_left - 1 - n) * tk, tk), cr, None), carry)
        r, acc = carry
        o_ref[...] = acc
        tot_ref[0] = r

    return pl.pallas_call(
        body, name=name, grid=(n_heads, s_len // tq),
        in_specs=[pl.BlockSpec((tq, HEAD_DIM), lambda h, i: (i, 3 * n_heads + h)),
                  pl.BlockSpec((s_len, HEAD_DIM), lambda h, i: (0, 4 * n_heads + h)),
                  pl.BlockSpec((s_len, HEAD_DIM), lambda h, i: (0, 5 * n_heads + h))],
        out_specs=[pl.BlockSpec((tq, HEAD_DIM), lambda h, i: (i, h)),
                   pl.BlockSpec((1, tq, 1), lambda h, i: (h, i, 0))],
        out_shape=[jax.ShapeDtypeStruct((s_len, n_heads * HEAD_DIM), F32),
                   jax.ShapeDtypeStruct((n_heads, s_len, 1), F32)],
        compiler_params=_params(("parallel", "arbitrary")))(pa, pa, pa)


def _sb_bwd(name, pa, do, tot, n_heads, tq=1024, tk=256):
    s_len = pa.shape[0]
    tq = _tile(s_len, tq, LANES)
    tk = _tile(tq, tk, LANES)
    nsub = tq // tk
    w_dim = n_heads * HEAD_DIM

    def body(q_ref, k_ref, v_ref, do_ref, tot_ref, dq_ref, dk_ref, dv_ref):
        qi = pl.program_id(1)

        @pl.when(qi == 0)
        def _():
            dk_ref[...] = jnp.zeros_like(dk_ref)
            dv_ref[...] = jnp.zeros_like(dv_ref)

        q = q_ref[...]
        dob = do_ref[...].astype(BF16)
        u = _tri(tk, "le")
        row = lax.broadcasted_iota(jnp.int32, (tq, tk), 0)
        col = lax.broadcasted_iota(jnp.int32, (tq, tk), 1)

        def block(ks, carry, mask_off):
            rem, cpre, dq = carry
            k = k_ref[pl.ds(ks, tk), :]
            v = v_ref[pl.ds(ks, tk), :]
            z = lax.dot_general(q, k, NT_DIMS, preferred_element_type=F32)
            a, e = _neg_softplus(z)
            if mask_off is not None:
                valid = col + mask_off < row
                a = jnp.where(valid, a, 0.0)
            pin = _split_dot(a, u, 2)
            w = jnp.exp(z + (rem - (pin - a)))
            if mask_off is not None:
                w = jnp.where(valid, w, 0.0)
            g = w * lax.dot_general(dob, v, NT_DIMS, preferred_element_type=F32)
            cin = _split_dot(g, u, 2)
            beta = jnp.where(z >= 0, 1.0, e) / (1.0 + e)
            dz = g - beta * (cpre + cin)
            if mask_off is not None:
                dz = jnp.where(valid, dz, 0.0)
            dzb = dz.astype(BF16)
            dk_ref[pl.ds(ks, tk), :] += lax.dot_general(dzb, q, TN_DIMS, preferred_element_type=F32)
            dv_ref[pl.ds(ks, tk), :] += lax.dot_general(w.astype(BF16), dob, TN_DIMS, preferred_element_type=F32)
            dq = dq + jnp.dot(dzb, k, preferred_element_type=F32)
            return rem - pin[:, tk - 1:tk], cpre + cin[:, tk - 1:tk], dq

        carry = (tot_ref[0], jnp.zeros((tq, 1), F32), jnp.zeros((tq, HEAD_DIM), F32))
        carry = lax.fori_loop(0, qi * nsub, lambda n, cr: block(pl.multiple_of(n * tk, tk), cr, None), carry)
        q0 = pl.multiple_of(qi * tq, tq)
        for j in range(nsub):
            carry = block(q0 + j * tk, carry, j * tk)
        dq_ref[...] = carry[2] * QK_SCALE

    blk = pl.BlockSpec((tq, HEAD_DIM), lambda h, i: (i, h))
    full = pl.BlockSpec((s_len, HEAD_DIM), lambda h, i: (0, h))
    return pl.pallas_call(
        body, name=name, grid=(n_heads, s_len // tq),
        in_specs=[pl.BlockSpec((tq, HEAD_DIM), lambda h, i: (i, 3 * n_heads + h)),
                  pl.BlockSpec((s_len, HEAD_DIM), lambda h, i: (0, 4 * n_heads + h)),
                  pl.BlockSpec((s_len, HEAD_DIM), lambda h, i: (0, 5 * n_heads + h)),
                  blk,
                  pl.BlockSpec((1, tq, 1), lambda h, i: (h, i, 0))],
        out_specs=[blk, full, full],
        out_shape=[jax.ShapeDtypeStruct((s_len, w_dim), F32)] * 3,
        compiler_params=_params(("arbitrary", "arbitrary")))(pa, pa, pa, do, tot)


def _conv_rows(ext_ref, u_ref, halo_ref, first, tm):
    ext_ref[0:8, :] = jnp.where(first, 0.0, halo_ref[...])
    ext_ref[8:, :] = u_ref[...]
    return ext_ref[8:8 + tm, :], ext_ref[7:7 + tm, :], ext_ref[6:6 + tm, :]


def _sigmoid(x):
    return 1.0 / (1.0 + jnp.exp(-x))


def _conv_specs(tm, tc, nj, order):
    hb = tm // 8
    ij = order

    def at(f):
        return lambda *g: f(*ij(*g))

    return [pl.BlockSpec((tm, tc), at(lambda i, j: (i, j))),
            pl.BlockSpec((tm, tc), at(lambda i, j: (i, j + nj))),
            pl.BlockSpec((8, tc), at(lambda i, j: (jnp.maximum(i * hb - 1, 0), j))),
            pl.BlockSpec((8, tc), at(lambda i, j: (jnp.maximum(i * hb - 1, 0), j + nj))),
            pl.BlockSpec((3, tc), at(lambda i, j: (0, j))),
            pl.BlockSpec((3, tc), at(lambda i, j: (0, j + nj))),
            pl.BlockSpec((1, tc), at(lambda i, j: (0, j))),
            pl.BlockSpec((1, tc), at(lambda i, j: (0, j + nj)))]


def _conv_gate_fwd(name, u, cw, cb, tm=512, tc=512):
    s_len, f2 = u.shape
    f_dim = f2 // 2
    tm = _tile(s_len, tm, 8)
    tc = _tile(f_dim, tc)
    nj = f_dim // tc

    def body(ug_ref, uv_ref, hg_ref, hv_ref, wg_ref, wv_ref, bg_ref, bv_ref, g_ref, ext_ref):
        first = pl.program_id(0) == 0

        def conv(u_ref, h_ref, w_ref, b_ref):
            u0, u1, u2 = _conv_rows(ext_ref, u_ref, h_ref, first, tm)
            return w_ref[2:3, :] * u0 + w_ref[1:2, :] * u1 + w_ref[0:1, :] * u2 + b_ref[...]

        gc = conv(ug_ref, hg_ref, wg_ref, bg_ref)
        vc = conv(uv_ref, hv_ref, wv_ref, bv_ref)
        g_ref[...] = (gc * _sigmoid(gc) * vc).astype(BF16)

    return pl.pallas_call(
        body, name=name, grid=(s_len // tm, nj),
        in_specs=_conv_specs(tm, tc, nj, lambda i, j: (i, j)),
        out_specs=pl.BlockSpec((tm, tc), lambda i, j: (i, j)),
        out_shape=jax.ShapeDtypeStruct((s_len, f_dim), BF16),
        scratch_shapes=[pltpu.VMEM((tm + 8, tc), F32)],
        compiler_params=_params(("parallel", "parallel")))(u, u, u, u, cw, cw, cb, cb)


def _conv_gate_bwd(name, u, dg, cw, cb, tm=512, tc=512):
    s_len, f2 = u.shape
    f_dim = f2 // 2
    tm = _tile(s_len, tm, 8)
    tc = _tile(f_dim, tc)
    nj = f_dim // tc

    def body(ug_ref, uv_ref, hg_ref, hv_ref, wg_ref, wv_ref, bg_ref, bv_ref, dg_ref, duc_ref, dcw_ref, dcb_ref,
             eg_ref, ev_ref):
        first = pl.program_id(1) == 0

        @pl.when(first)
        def _():
            dcw_ref[...] = jnp.zeros_like(dcw_ref)
            dcb_ref[...] = jnp.zeros_like(dcb_ref)

        ug = _conv_rows(eg_ref, ug_ref, hg_ref, first, tm)
        uv = _conv_rows(ev_ref, uv_ref, hv_ref, first, tm)
        gc = wg_ref[2:3, :] * ug[0] + wg_ref[1:2, :] * ug[1] + wg_ref[0:1, :] * ug[2] + bg_ref[...]
        vc = wv_ref[2:3, :] * uv[0] + wv_ref[1:2, :] * uv[1] + wv_ref[0:1, :] * uv[2] + bv_ref[...]
        sg = _sigmoid(gc)
        dgv = dg_ref[...]
        dvc = dgv * (gc * sg)
        dgc = dgv * vc * (sg * (1.0 + gc * (1.0 - sg)))
        duc_ref[0] = dgc
        duc_ref[1] = dvc
        for half, (d, us) in enumerate(((dgc, ug), (dvc, uv))):
            dcb_ref[half] += jnp.sum(d, axis=0, keepdims=True)
            for tap in range(3):
                dcw_ref[half, tap:tap + 1, :] += jnp.sum(d * us[2 - tap], axis=0, keepdims=True)

    order = lambda j, i: (i, j)
    return pl.pallas_call(
        body, name=name, grid=(nj, s_len // tm),
        in_specs=_conv_specs(tm, tc, nj, order) + [pl.BlockSpec((tm, tc), lambda j, i: (i, j))],
        out_specs=[pl.BlockSpec((2, tm, tc), lambda j, i: (0, i, j)),
                   pl.BlockSpec((2, 3, tc), lambda j, i: (0, 0, j)),
                   pl.BlockSpec((2, 1, tc), lambda j, i: (0, 0, j))],
        out_shape=[jax.ShapeDtypeStruct((2, s_len, f_dim), F32), jax.ShapeDtypeStruct((2, 3, f_dim), F32),
                   jax.ShapeDtypeStruct((2, 1, f_dim), F32)],
        scratch_shapes=[pltpu.VMEM((tm + 8, tc), F32), pltpu.VMEM((tm + 8, tc), F32)],
        compiler_params=_params(("parallel", "arbitrary")))(u, u, u, u, cw, cw, cb, cb, dg)


def _conv_t(name, duc, cw, tm=512, tc=512):
    _, s_len, f_dim = duc.shape
    tm = _tile(s_len, tm, 8)
    tc = _tile(f_dim, tc)
    nj = f_dim // tc
    nb = s_len // tm
    hb = tm // 8

    def body(d_ref, halo_ref, w_ref, o_ref, ext_ref):
        last = pl.program_id(1) == nb - 1
        ext_ref[0:tm, :] = d_ref[0]
        ext_ref[tm:, :] = jnp.where(last, 0.0, halo_ref[0])
        o_ref[...] = (w_ref[2:3, :] * ext_ref[0:tm, :] + w_ref[1:2, :] * ext_ref[1:1 + tm, :]
                      + w_ref[0:1, :] * ext_ref[2:2 + tm, :]).astype(BF16)

    return pl.pallas_call(
        body, name=name, grid=(2, nb, nj),
        in_specs=[pl.BlockSpec((1, tm, tc), lambda p, i, j: (p, i, j)),
                  pl.BlockSpec((1, 8, tc), lambda p, i, j: (p, jnp.minimum((i + 1) * hb, nb * hb - 1), j)),
                  pl.BlockSpec((3, tc), lambda p, i, j: (0, p * nj + j))],
        out_specs=pl.BlockSpec((tm, tc), lambda p, i, j: (i, p * nj + j)),
        out_shape=jax.ShapeDtypeStruct((s_len, 2 * f_dim), BF16),
        scratch_shapes=[pltpu.VMEM((tm + 8, tc), F32)],
        compiler_params=_params(("parallel", "parallel", "parallel")))(duc, duc, cw)


def _loss_head(name, y, tgt, tm=512):
    s_len, d = y.shape
    tm = _tile(s_len, tm, 8)

    def body(y_ref, t_ref, dy_ref, l_ref):
        @pl.when(pl.program_id(0) == 0)
        def _():
            l_ref[...] = jnp.zeros_like(l_ref)

        err = y_ref[...] - t_ref[...]
        dy_ref[...] = err * (1.0 / d)
        l_ref[...] += 0.5 * jnp.sum(jnp.sum(err * err, axis=1, keepdims=True) * (1.0 / d), axis=0, keepdims=True)

    blk = pl.BlockSpec((tm, d), lambda i: (i, 0))
    return pl.pallas_call(
        body, name=name, grid=(s_len // tm,), in_specs=[blk, blk],
        out_specs=[blk, pl.BlockSpec((1, LANES), lambda i: (0, 0))],
        out_shape=[jax.ShapeDtypeStruct((s_len, d), F32), jax.ShapeDtypeStruct((1, LANES), F32)],
        compiler_params=_params(("arbitrary",)))(y, tgt)


def _adamw(name, parts, w, m, v, tr=256):
    n_parts, rows, cols = parts.shape
    tr = _tile(rows, tr, 16)
    c1 = 1.0 - ADAM_B1 ** ADAM_STEP
    c2 = 1.0 - ADAM_B2 ** ADAM_STEP

    def body(p_ref, w_ref, m_ref, v_ref, g_ref, d_ref, nm_ref, nv_ref):
        g = p_ref[0].astype(F32)
        for n in range(1, n_parts):
            g = g + p_ref[n].astype(F32)
        nm = ADAM_B1 * m_ref[...] + (1.0 - ADAM_B1) * g
        nv = ADAM_B2 * v_ref[...] + (1.0 - ADAM_B2) * (g * g)
        g_ref[...] = g
        nm_ref[...] = nm
        nv_ref[...] = nv
        d_ref[...] = -ADAM_LR * ((nm / c1) / (jnp.sqrt(nv / c2) + ADAM_EPS) + ADAM_WD * w_ref[...])

    blk = pl.BlockSpec((tr, cols), lambda i: (i, 0))
    return pl.pallas_call(
        body, name=name, grid=(rows // tr,),
        in_specs=[pl.BlockSpec((n_parts, tr, cols), lambda i: (0, i, 0)), blk, blk, blk],
        out_specs=[blk] * 4, out_shape=[jax.ShapeDtypeStruct((rows, cols), F32)] * 4,
        compiler_params=_params(("parallel",)))(parts, w, m, v)


def _peers():
    x, y, c = lax.axis_index("x"), lax.axis_index("y"), lax.axis_index("c")
    out = []
    for k in range(1, N_DEV):
        fx, fy, fc = (k >> 2) & 1, (k >> 1) & 1, k & 1
        px, py, pc = x ^ fx, y ^ fy, c ^ fc
        out.append((k - 1, (px, py, pc), 4 * px + 2 * py + pc))
    return 4 * x + 2 * y + c, out


def _exchange(name, arrays, scatter):
    n = len(arrays)

    def body(*refs):
        in_refs, out_refs = refs[:n], refs[n:2 * n]
        send_sems, recv_sems, local_sems = refs[2 * n:]
        me, peers = _peers()
        copies = []
        for a, (src, dst) in enumerate(zip(in_refs, out_refs)):
            own = pltpu.make_async_copy(src.at[me] if scatter else src, dst.at[me], local_sems.at[a])
            own.start()
            copies.append(own)
            for k, pos, idx in peers:
                cp = pltpu.make_async_remote_copy(
                    src_ref=src.at[idx] if scatter else src, dst_ref=dst.at[me],
                    send_sem=send_sems.at[a, k], recv_sem=recv_sems.at[a, k], device_id=pos, device_id_type=MESH)
                cp.start()
                copies.append(cp)
        for cp in copies:
            cp.wait()

    any_spec = pl.BlockSpec(memory_space=pl.ANY)
    return pl.pallas_call(
        body, name=name, in_specs=[any_spec] * n, out_specs=[any_spec] * n,
        out_shape=[jax.ShapeDtypeStruct(a.shape if scatter else (N_DEV,) + a.shape, a.dtype) for a in arrays],
        scratch_shapes=[pltpu.SemaphoreType.DMA((n, N_DEV - 1)), pltpu.SemaphoreType.DMA((n, N_DEV - 1)),
                        pltpu.SemaphoreType.DMA((n,))],
        compiler_params=pltpu.CompilerParams(has_side_effects=True))(*arrays)


def _row(vec, width=None):
    vec = vec.reshape(1, -1)
    if width is not None and vec.shape[1] < width:
        vec = jnp.pad(vec, ((0, 0), (0, width - vec.shape[1])))
    return vec


def _local_step(x, tgt, small, big):
    n_layers = len(big)
    d = x.shape[1]
    n_heads = d // (2 * HEAD_DIM)
    saved = []
    for l in range(n_layers):
        sp, bp = small[l], big[l]
        proj, h = _mm_nn(f"l{l}_in_proj", [x], [_row(sp["attn_norm"])], bp["w_in"])
        qg, kg, bf = _row(sp["q_norm"]), _row(sp["k_norm"]), _row(sp["b_forget"], LANES)
        pa, c = _prep_fwd(f"l{l}_prep", proj, qg, kg, bf, n_heads)
        ct = c[:, :n_heads].T
        o_a, lse = _fox_fwd(f"l{l}_fox", pa, c, ct, n_heads)
        o_b, tot = _sb_fwd(f"l{l}_sb", pa, n_heads)
        gfox, gsb = _row(sp["out_norm_fox"]), _row(sp["out_norm_sb"])
        x1, merged = _mm_nn(f"l{l}_out_proj", [o_a, o_b], [gfox, gsb], bp["w_out"], resid=x)
        u, h2 = _mm_nn(f"l{l}_up_proj", [x1], [_row(sp["ffn_norm"])], bp["w_up"])
        cb = _row(sp["conv_b"])
        g = _conv_gate_fwd(f"l{l}_conv_gate", u, bp["conv_w"], cb)
        x2 = _mm_nn(f"l{l}_down_proj", [g], [], bp["w_down"], resid=x1, tn_cap=512)
        saved.append(dict(x=x, h=h, proj=proj, pa=pa, c=c, ct=ct, o_a=o_a, lse=lse, o_b=o_b, tot=tot,
                          merged=merged, x1=x1, h2=h2, u=u, g=g, qg=qg, kg=kg, bf=bf, gfox=gfox, gsb=gsb, cb=cb))
        x = x2
    dx, loss_part = _loss_head("loss_head", x, tgt)
    grads = [None] * n_layers
    for l in reversed(range(n_layers)):
        sp, bp, sv = small[l], big[l], saved[l]
        gr = {}
        dg = _mm_nt(f"l{l}_d_down_act", dx, bp["w_down"])
        gr["w_down"] = _mm_tn(f"l{l}_d_w_down", sv["g"], dx)
        duc, dcw, dcb = _conv_gate_bwd(f"l{l}_d_conv_gate", sv["u"], dg, bp["conv_w"], sv["cb"])
        gr["conv_w"] = dcw.transpose(1, 0, 2).reshape(3, -1)
        gr["conv_b"] = dcb.reshape(-1)
        du = _conv_t(f"l{l}_d_conv", duc, bp["conv_w"])
        gr["w_up"] = _mm_tn(f"l{l}_d_w_up", sv["h2"], du)
        dx1, dffn = _mm_nt(f"l{l}_d_up_act", du, bp["w_up"], "rms_bwd", [sv["x1"]], [_row(sp["ffn_norm"])], dres=dx)
        gr["ffn_norm"] = dffn.reshape(-1)
        gr["w_out"] = _mm_tn(f"l{l}_d_w_out", sv["merged"], dx1)
        do_a, do_b, dgfox, dgsb = _mm_nt(f"l{l}_d_out_act", dx1, bp["w_out"], "rms2_bwd",
                                         [sv["o_a"], sv["o_b"]], [sv["gfox"], sv["gsb"]])
        gr["out_norm_fox"], gr["out_norm_sb"] = dgfox.reshape(-1), dgsb.reshape(-1)
        dq_a, dk_a, dv_a, dct, dcq = _fox_bwd(f"l{l}_d_fox", sv["pa"], sv["c"], sv["ct"], sv["o_a"], do_a, sv["lse"],
                                              n_heads)
        d_sb = _sb_bwd(f"l{l}_d_sb", sv["pa"], do_b, sv["tot"], n_heads)
        dc = jnp.pad((dct + dcq[:, :, 0]).T, ((0, 0), (0, LANES - n_heads)))
        dproj, dqg, dkg, dbf = _prep_bwd(f"l{l}_d_prep", sv["proj"], sv["qg"], sv["kg"], sv["bf"],
                                         (dq_a, dk_a, dv_a), d_sb, dc, n_heads)
        gr["q_norm"], gr["k_norm"], gr["b_forget"] = dqg.reshape(-1), dkg.reshape(-1), dbf.reshape(-1)[:n_heads]
        gr["w_in"] = _mm_tn(f"l{l}_d_w_in", sv["h"], dproj)
        dx, dattn = _mm_nt(f"l{l}_d_in_act", dproj, bp["w_in"], "rms_bwd", [sv["x"]], [_row(sp["attn_norm"])],
                           dres=dx1)
        gr["attn_norm"] = dattn.reshape(-1)
        grads[l] = gr
    return loss_part, dx, grads


def _w_in_to_internal(w, n_heads):
    w3 = 3 * n_heads * HEAD_DIM
    pad = jnp.zeros(w.shape[:-1] + (LANES - n_heads,), w.dtype)
    return jnp.concatenate([w[..., :w3], w[..., w3 + n_heads:], w[..., w3:w3 + n_heads], pad], axis=-1)


def _w_in_from_internal(w, n_heads):
    w3 = 3 * n_heads * HEAD_DIM
    return jnp.concatenate([w[..., :w3], w[..., 2 * w3:2 * w3 + n_heads], w[..., w3:2 * w3]], axis=-1)


SMALL = ("attn_norm", "b_forget", "q_norm", "k_norm", "out_norm_fox", "out_norm_sb", "ffn_norm", "conv_b")
WEIGHTS = ("attn_norm", "w_in", "b_forget", "q_norm", "k_norm", "out_norm_fox", "out_norm_sb", "w_out", "ffn_norm",
           "w_up", "conv_w", "conv_b", "w_down")


def kernel(x, attn_norm, w_in, b_forget, q_norm, k_norm, out_norm_fox, out_norm_sb, w_out, ffn_norm, w_up, conv_w, conv_b, w_down, loss_target, m_attn_norm, m_w_in, m_b_forget, m_q_norm, m_k_norm, m_out_norm_fox, m_out_norm_sb, m_w_out, m_ffn_norm, m_w_up, m_conv_w, m_conv_b, m_w_down, v_attn_norm, v_w_in, v_b_forget, v_q_norm, v_k_norm, v_out_norm_fox, v_out_norm_sb, v_w_out, v_ffn_norm, v_w_up, v_conv_w, v_conv_b, v_w_down):
    w = dict(attn_norm=attn_norm, w_in=w_in, b_forget=b_forget, q_norm=q_norm, k_norm=k_norm,
             out_norm_fox=out_norm_fox, out_norm_sb=out_norm_sb, w_out=w_out, ffn_norm=ffn_norm, w_up=w_up,
             conv_w=conv_w, conv_b=conv_b, w_down=w_down)
    mom = dict(attn_norm=m_attn_norm, w_in=m_w_in, b_forget=m_b_forget, q_norm=m_q_norm, k_norm=m_k_norm,
               out_norm_fox=m_out_norm_fox, out_norm_sb=m_out_norm_sb, w_out=m_w_out, ffn_norm=m_ffn_norm,
               w_up=m_w_up, conv_w=m_conv_w, conv_b=m_conv_b, w_down=m_w_down)
    var = dict(attn_norm=v_attn_norm, w_in=v_w_in, b_forget=v_b_forget, q_norm=v_q_norm, k_norm=v_k_norm,
               out_norm_fox=v_out_norm_fox, out_norm_sb=v_out_norm_sb, w_out=v_w_out, ffn_norm=v_ffn_norm,
               w_up=v_w_up, conv_w=v_conv_w, conv_b=v_conv_b, w_down=v_w_down)
    n_layers, d = attn_norm.shape
    n_heads = d // (2 * HEAD_DIM)
    me = 4 * lax.axis_index("x") + 2 * lax.axis_index("y") + lax.axis_index("c")

    shards = [w["w_in"].astype(BF16), w["w_out"].astype(BF16), w["w_up"].astype(BF16), w["w_down"].astype(BF16),
              w["conv_w"]]
    g_in, g_out, g_up, g_down, g_cw = _exchange("gather_weights", shards, scatter=False)
    cols = lambda a: a.transpose(1, 2, 0, 3).reshape(a.shape[1], a.shape[2], -1)
    rows = lambda a: a.transpose(1, 0, 2, 3).reshape(a.shape[1], -1, a.shape[3])
    full_in = _w_in_to_internal(cols(g_in), n_heads)
    full_out, full_up, full_down, full_cw = rows(g_out), cols(g_up), rows(g_down), cols(g_cw)
    big = [dict(w_in=full_in[l], w_out=full_out[l], w_up=full_up[l], w_down=full_down[l], conv_w=full_cw[l])
           for l in range(n_layers)]
    small = [{k: w[k][l] for k in SMALL} for l in range(n_layers)]

    loss_part, grad_x, grads = _local_step(x[0], loss_target[0], small, big)

    stack = lambda k: jnp.stack([grads[l][k] for l in range(n_layers)])
    to_cols = lambda a: a.reshape(a.shape[0], a.shape[1], N_DEV, -1).transpose(2, 0, 1, 3)
    to_rows = lambda a: a.reshape(a.shape[0], N_DEV, -1, a.shape[2]).transpose(1, 0, 2, 3)
    sends = [to_cols(_w_in_from_internal(stack("w_in"), n_heads)), to_rows(stack("w_out")), to_cols(stack("w_up")),
             to_rows(stack("w_down"))]
    r_in, r_out, r_up, r_down = _exchange("scatter_grads", sends, scatter=True)
    small_names = SMALL + ("conv_w",)
    flat = jnp.concatenate([loss_part.reshape(-1)] + [
        jnp.pad(stack(k).reshape(-1), (0, (-stack(k).size) % LANES)) for k in small_names])
    flat = flat.reshape(-1, LANES)
    (all_small,) = _exchange("gather_small_grads", [flat], scatter=False)

    out = {}

    def update(name, parts, shape):
        res = _adamw("adamw_" + name, parts, w[name].reshape(parts.shape[1:]), mom[name].reshape(parts.shape[1:]),
                     var[name].reshape(parts.shape[1:]))
        out[name] = [r.reshape(shape) for r in res]

    for name, parts in (("w_in", r_in), ("w_out", r_out), ("w_up", r_up), ("w_down", r_down)):
        update(name, parts.reshape(N_DEV, -1, parts.shape[-1]), w[name].shape)
    n_rows = flat.shape[0]
    w_flat, m_flat, v_flat = [], [], []
    for src, dst in ((w, w_flat), (mom, m_flat), (var, v_flat)):
        dst.append(jnp.zeros((LANES,), F32))
        for k in small_names:
            a = src[k]
            if k == "conv_w":
                a = jnp.zeros((n_layers, 3, conv_w.shape[2] * N_DEV), F32)
            dst.append(jnp.pad(a.reshape(-1), (0, (-a.size) % LANES)))
    pack = lambda parts: jnp.concatenate(parts).reshape(n_rows, LANES)
    res = _adamw("adamw_small", all_small, pack(w_flat), pack(m_flat), pack(v_flat))
    res = [r.reshape(-1) for r in res]
    loss = res[0][0]
    off = LANES
    g_cw_full = None
    for k in small_names:
        size = n_layers * 3 * conv_w.shape[2] * N_DEV if k == "conv_w" else w[k].size
        if k == "conv_w":
            g_cw_full = res[0][off:off + size].reshape(n_layers, 3, -1)
        else:
            out[k] = [r[off:off + size].reshape(w[k].shape) for r in res]
        off += size + (-size) % LANES
    c_loc = conv_w.shape[2]
    g_cw_mine = lax.dynamic_slice_in_dim(g_cw_full, me * c_loc, c_loc, axis=2)
    update("conv_w", g_cw_mine.reshape(1, n_layers * 3, c_loc), conv_w.shape)

    outs = [loss, grad_x[None]]
    for n in range(4):
        outs += [out[k][n] for k in WEIGHTS]
    return tuple(outs)
```

```python
import functools

import jax
import jax.numpy as jnp
from jax import lax
from jax.experimental import pallas as pl
from jax.experimental.pallas import tpu as pltpu

F32 = jnp.float32
BF16 = jnp.bfloat16
HEAD_DIM = 128
QK_SCALE = HEAD_DIM ** -0.5
LANES = 128
EPS = 1e-6
N_DEV = 8
ADAM_LR = 0.001
ADAM_B1 = 0.9
ADAM_B2 = 0.999
ADAM_EPS = 1e-08
ADAM_WD = 0.01
ADAM_STEP = 10
VMEM_LIMIT = 56 * 1024 * 1024
ADAMW_VMEM = 24 * 1024 * 1024
MESH = pl.DeviceIdType.MESH

NT_DIMS = (((1,), (1,)), ((), ()))
TN_DIMS = (((0,), (0,)), ((), ()))


def _tile(n, cap, mult=LANES):
    t = (min(cap, n) // mult) * mult
    while t >= mult:
        if n % t == 0:
            return t
        t -= mult
    return n


def _params(sem, vmem=VMEM_LIMIT):
    return pltpu.CompilerParams(dimension_semantics=sem, vmem_limit_bytes=vmem)


def _split_dot(x, u, n_split, x_left=True):
    acc = None
    rest = x
    for s in range(n_split):
        piece = rest.astype(BF16)
        if s + 1 < n_split:
            rest = rest - piece.astype(F32)
        d = (jnp.dot(piece, u, preferred_element_type=F32) if x_left
             else jnp.dot(u, piece, preferred_element_type=F32))
        acc = d if acc is None else acc + d
    return acc


def _tri(n, kind):
    r = lax.broadcasted_iota(jnp.int32, (n, n), 0)
    c = lax.broadcasted_iota(jnp.int32, (n, n), 1)
    return jnp.where(r >= c if kind == "ge" else r <= c, 1.0, 0.0).astype(BF16)


def _mm_nn(name, a_list, g_list, w, resid=None, tm=1024, tn_cap=1024):
    s_len = a_list[0].shape[0]
    k_dim, n_dim = w.shape
    tm = _tile(s_len, tm, 8)
    tn = _tile(n_dim, tn_cap)
    normed = bool(g_list)

    def body(*refs):
        refs = list(refs)
        a_refs = [refs.pop(0) for _ in a_list]
        g_refs = [refs.pop(0) for _ in g_list]
        w_ref = refs.pop(0)
        r_ref = refs.pop(0) if resid is not None else None
        o_ref = refs.pop(0)
        if normed:
            h_ref = refs.pop(0)

            @pl.when(pl.program_id(1) == 0)
            def _():
                off = 0
                for a_ref, g_ref in zip(a_refs, g_refs):
                    xv = a_ref[...]
                    kk = xv.shape[1]
                    r = lax.rsqrt(jnp.mean(xv * xv, axis=1, keepdims=True) + EPS)
                    h_ref[:, off:off + kk] = (xv * r * g_ref[...]).astype(BF16)
                    off += kk

            a = h_ref[...]
        else:
            a = a_refs[0][...]
        acc = jnp.dot(a, w_ref[...], preferred_element_type=F32)
        if r_ref is not None:
            acc = acc + r_ref[...]
        o_ref[...] = acc

    in_specs = [pl.BlockSpec((tm, a.shape[1]), lambda i, j: (i, 0)) for a in a_list]
    in_specs += [pl.BlockSpec((1, g.shape[1]), lambda i, j: (0, 0)) for g in g_list]
    in_specs += [pl.BlockSpec((k_dim, tn), lambda i, j: (0, j))]
    args = list(a_list) + list(g_list) + [w]
    if resid is not None:
        in_specs.append(pl.BlockSpec((tm, tn), lambda i, j: (i, j)))
        args.append(resid)
    out_shape = [jax.ShapeDtypeStruct((s_len, n_dim), F32)]
    out_specs = [pl.BlockSpec((tm, tn), lambda i, j: (i, j))]
    if normed:
        out_shape.append(jax.ShapeDtypeStruct((s_len, k_dim), BF16))
        out_specs.append(pl.BlockSpec((tm, k_dim), lambda i, j: (i, 0)))
    res = pl.pallas_call(
        body, name=name, grid=(s_len // tm, n_dim // tn), in_specs=in_specs, out_specs=out_specs,
        out_shape=out_shape, compiler_params=_params(("parallel", "arbitrary")))(*args)
    return res if normed else res[0]


def _rms_bwd(dh, xv, gv, r=None):
    if r is None:
        r = lax.rsqrt(jnp.mean(xv * xv, axis=1, keepdims=True) + EPS)
    xhat = xv * r
    dxh = dh * gv
    dx = r * (dxh - xhat * jnp.mean(dxh * xhat, axis=1, keepdims=True))
    return dx, dh * xhat


def _mm_nt(name, da, w, mode="plain", xs=(), gs=(), dres=None, tm=512, tk_cap=1024):
    s_len, kc = da.shape
    n_out = w.shape[0]
    tm = _tile(s_len, tm, 8)
    tk = _tile(kc, tk_cap)
    nk = kc // tk

    def body(*refs):
        refs = list(refs)
        da_ref, w_ref = refs.pop(0), refs.pop(0)
        x_refs = [refs.pop(0) for _ in xs]
        g_refs = [refs.pop(0) for _ in gs]
        dres_ref = refs.pop(0) if dres is not None else None
        acc_ref = refs[0] if mode == "plain" else refs.pop()
        out_refs = refs
        i, k = pl.program_id(0), pl.program_id(1)

        @pl.when(k == 0)
        def _():
            acc_ref[...] = jnp.zeros_like(acc_ref)

        acc_ref[...] += lax.dot_general(da_ref[...].astype(BF16), w_ref[...], NT_DIMS,
                                        preferred_element_type=F32)

        if mode == "plain":
            return

        @pl.when(k == nk - 1)
        def _():
            n_x = len(xs)
            dx_refs, dg_refs = out_refs[:n_x], out_refs[n_x:]

            @pl.when(i == 0)
            def _():
                for dg_ref in dg_refs:
                    dg_ref[...] = jnp.zeros_like(dg_ref)

            off = 0
            for x_ref, g_ref, dx_ref, dg_ref in zip(x_refs, g_refs, dx_refs, dg_refs):
                kk = x_ref.shape[1]
                dx, dgp = _rms_bwd(acc_ref[:, off:off + kk], x_ref[...], g_ref[...])
                if dres_ref is not None:
                    dx = dx + dres_ref[...]
                dx_ref[...] = dx
                dg_ref[...] += jnp.sum(dgp, axis=0, keepdims=True)
                off += kk

    in_specs = [pl.BlockSpec((tm, tk), lambda i, k: (i, k)), pl.BlockSpec((n_out, tk), lambda i, k: (0, k))]
    in_specs += [pl.BlockSpec((tm, x.shape[1]), lambda i, k: (i, 0)) for x in xs]
    in_specs += [pl.BlockSpec((1, g.shape[1]), lambda i, k: (0, 0)) for g in gs]
    args = [da, w] + list(xs) + list(gs)
    if dres is not None:
        in_specs.append(pl.BlockSpec((tm, n_out), lambda i, k: (i, 0)))
        args.append(dres)
    if mode == "plain":
        out_shape = [jax.ShapeDtypeStruct((s_len, n_out), F32)]
        out_specs = [pl.BlockSpec((tm, n_out), lambda i, k: (i, 0))]
    else:
        out_shape = [jax.ShapeDtypeStruct((s_len, x.shape[1]), F32) for x in xs]
        out_specs = [pl.BlockSpec((tm, x.shape[1]), lambda i, k: (i, 0)) for x in xs]
        out_shape += [jax.ShapeDtypeStruct((1, x.shape[1]), F32) for x in xs]
        out_specs += [pl.BlockSpec((1, x.shape[1]), lambda i, k: (0, 0)) for x in xs]
    res = pl.pallas_call(
        body, name=name, grid=(s_len // tm, nk), in_specs=in_specs, out_specs=out_specs, out_shape=out_shape,
        scratch_shapes=[] if mode == "plain" else [pltpu.VMEM((tm, n_out), F32)],
        compiler_params=_params(("arbitrary", "arbitrary")))(*args)
    return res[0] if mode == "plain" else res


def _mm_tn(name, a, b, tk_cap=1024, tn_cap=2048, tm=1024):
    s_len, k_dim = a.shape
    n_dim = b.shape[1]
    tk = _tile(k_dim, tk_cap)
    tn = _tile(n_dim, tn_cap)
    tm = _tile(s_len, tm, 8)
    nm = s_len // tm

    def body(a_ref, b_ref, o_ref, acc_ref):
        m = pl.program_id(2)

        @pl.when(m == 0)
        def _():
            acc_ref[...] = jnp.zeros_like(acc_ref)

        acc_ref[...] += lax.dot_general(a_ref[...].astype(BF16), b_ref[...].astype(BF16), TN_DIMS,
                                        preferred_element_type=F32)

        @pl.when(m == nm - 1)
        def _():
            o_ref[...] = acc_ref[...].astype(BF16)

    return pl.pallas_call(
        body, name=name, grid=(k_dim // tk, n_dim // tn, nm),
        in_specs=[pl.BlockSpec((tm, tk), lambda i, j, m: (m, i)), pl.BlockSpec((tm, tn), lambda i, j, m: (m, j))],
        out_specs=pl.BlockSpec((tk, tn), lambda i, j, m: (i, j)),
        out_shape=jax.ShapeDtypeStruct((k_dim, n_dim), BF16),
        scratch_shapes=[pltpu.VMEM((tk, tn), F32)],
        compiler_params=_params(("parallel", "parallel", "arbitrary")))(a, b)


def _neg_softplus(z):
    e = jnp.exp(-jnp.abs(z))
    return -(jnp.maximum(z, 0.0) + jnp.log(1.0 + e)), e


def _prep_fwd(name, proj, qg, kg, bf, n_heads, tm=256):
    s_len, n_p = proj.shape
    w_dim = n_heads * HEAD_DIM
    tm = _tile(s_len, tm, 8)

    def body(p_ref, qg_ref, kg_ref, bf_ref, pa_ref, c_ref, carry_ref):
        @pl.when(pl.program_id(0) == 0)
        def _():
            carry_ref[...] = jnp.zeros_like(carry_ref)

        for base, g_ref, mul in ((0, qg_ref, QK_SCALE), (w_dim, kg_ref, None)):
            for hh in range(n_heads):
                sl = slice(base + hh * HEAD_DIM, base + (hh + 1) * HEAD_DIM)
                xv = p_ref[:, sl]
                r = lax.rsqrt(jnp.mean(xv * xv, axis=1, keepdims=True) + EPS)
                y = xv * r * g_ref[...]
                pa_ref[:, sl] = (y if mul is None else y * mul).astype(BF16)
        pa_ref[:, 2 * w_dim:3 * w_dim] = p_ref[:, 2 * w_dim:3 * w_dim].astype(BF16)
        pa_ref[:, 3 * w_dim:4 * w_dim] = (p_ref[:, 3 * w_dim:4 * w_dim] * QK_SCALE).astype(BF16)
        pa_ref[:, 4 * w_dim:] = p_ref[:, 4 * w_dim:6 * w_dim].astype(BF16)
        f = p_ref[:, 6 * w_dim:] + bf_ref[...]
        lf, _ = _neg_softplus(-f)
        lane = lax.broadcasted_iota(jnp.int32, lf.shape, 1)
        lf = jnp.where(lane < n_heads, lf, 0.0)
        cb = _split_dot(lf, _tri(tm, "ge"), 3, x_left=False) + carry_ref[...]
        c_ref[...] = cb
        carry_ref[...] = cb[tm - 1:tm, :]

    return pl.pallas_call(
        body, name=name, grid=(s_len // tm,),
        in_specs=[pl.BlockSpec((tm, n_p), lambda i: (i, 0))] + [pl.BlockSpec((1, LANES), lambda i: (0, 0))] * 3,
        out_specs=[pl.BlockSpec((tm, 6 * w_dim), lambda i: (i, 0)), pl.BlockSpec((tm, LANES), lambda i: (i, 0))],
        out_shape=[jax.ShapeDtypeStruct((s_len, 6 * w_dim), BF16), jax.ShapeDtypeStruct((s_len, LANES), F32)],
        scratch_shapes=[pltpu.VMEM((1, LANES), F32)],
        compiler_params=_params(("arbitrary",)))(proj, qg, kg, bf)


def _prep_bwd(name, proj, qg, kg, bf, d_fox, d_sb, dc, n_heads, tm=256):
    s_len, n_p = proj.shape
    w_dim = n_heads * HEAD_DIM
    tm = _tile(s_len, tm, 8)
    nb = s_len // tm

    def body(p_ref, qg_ref, kg_ref, bf_ref, dqa_ref, dka_ref, dva_ref, dqb_ref, dkb_ref, dvb_ref, dc_ref,
             dp_ref, dqg_ref, dkg_ref, dbf_ref, carry_ref):
        @pl.when(pl.program_id(0) == 0)
        def _():
            for ref in (carry_ref, dqg_ref, dkg_ref, dbf_ref):
                ref[...] = jnp.zeros_like(ref)

        for base, g_ref, d_ref, dg_ref in ((0, qg_ref, dqa_ref, dqg_ref), (w_dim, kg_ref, dka_ref, dkg_ref)):
            dg = jnp.zeros((1, HEAD_DIM), F32)
            for hh in range(n_heads):
                sl = slice(base + hh * HEAD_DIM, base + (hh + 1) * HEAD_DIM)
                dx, dgp = _rms_bwd(d_ref[:, hh * HEAD_DIM:(hh + 1) * HEAD_DIM], p_ref[:, sl], g_ref[...])
                dp_ref[:, sl] = dx.astype(BF16)
                dg = dg + jnp.sum(dgp, axis=0, keepdims=True)
            dg_ref[...] += dg
        for n, d_ref in enumerate((dva_ref, dqb_ref, dkb_ref, dvb_ref)):
            dp_ref[:, (2 + n) * w_dim:(3 + n) * w_dim] = d_ref[...].astype(BF16)
        dlf = _split_dot(dc_ref[...], _tri(tm, "le"), 3, x_left=False) + carry_ref[...]
        carry_ref[...] = dlf[0:1, :]
        f = p_ref[:, 6 * w_dim:] + bf_ref[...]
        e = jnp.exp(-jnp.abs(f))
        sig_neg = jnp.where(f >= 0, e, 1.0) / (1.0 + e)
        lane = lax.broadcasted_iota(jnp.int32, f.shape, 1)
        df = jnp.where(lane < n_heads, dlf * sig_neg, 0.0)
        dp_ref[:, 6 * w_dim:] = df.astype(BF16)
        dbf_ref[...] += jnp.sum(df, axis=0, keepdims=True)

    rev = lambda i: (nb - 1 - i, 0)
    vec = pl.BlockSpec((1, LANES), lambda i: (0, 0))
    return pl.pallas_call(
        body, name=name, grid=(nb,),
        in_specs=[pl.BlockSpec((tm, n_p), rev), vec, vec, vec] + [pl.BlockSpec((tm, w_dim), rev)] * 6
        + [pl.BlockSpec((tm, LANES), rev)],
        out_specs=[pl.BlockSpec((tm, n_p), rev), vec, vec, vec],
        out_shape=[jax.ShapeDtypeStruct((s_len, n_p), BF16)] + [jax.ShapeDtypeStruct((1, LANES), F32)] * 3,
        scratch_shapes=[pltpu.VMEM((1, LANES), F32)],
        compiler_params=_params(("arbitrary",)))(proj, qg, kg, bf, *d_fox, *d_sb, dc)


def _head_col(c_blk, h):
    lane = lax.broadcasted_iota(jnp.int32, c_blk.shape, 1)
    return jnp.sum(jnp.where(lane == h, c_blk, 0.0), axis=1, keepdims=True)


def _fox_fwd(name, pa, c, ct, n_heads, tq=1024):
    s_len = pa.shape[0]
    tq = _tile(s_len, tq, LANES)
    hp = ct.shape[0]

    def body(q_ref, k_ref, v_ref, c_ref, ct_ref, o_ref, lse_ref):
        h, qi = pl.program_id(0), pl.program_id(1)
        q = q_ref[...]
        cq = _head_col(c_ref[...], h)
        row = lax.broadcasted_iota(jnp.int32, (tq, tq), 0)
        col = lax.broadcasted_iota(jnp.int32, (tq, tq), 1)

        def step(kb, carry, masked):
            m, l, acc = carry
            ks = pl.multiple_of(kb * tq, tq)
            k = k_ref[pl.ds(ks, tq), :]
            v = v_ref[pl.ds(ks, tq), :]
            ck = ct_ref[pl.ds(h, 1), pl.ds(ks, tq)]
            s = lax.dot_general(q, k, NT_DIMS, preferred_element_type=F32) + (cq - ck)
            if masked:
                s = jnp.where(col <= row, s, -jnp.inf)
            m_new = jnp.maximum(m, jnp.max(s, axis=1, keepdims=True))
            alpha = jnp.exp(m - m_new)
            p = jnp.exp(s - m_new)
            l = alpha * l + jnp.sum(p, axis=1, keepdims=True)
            acc = alpha * acc + jnp.dot(p.astype(BF16), v, preferred_element_type=F32)
            return m_new, l, acc

        init = (jnp.full((tq, 1), -jnp.inf, F32), jnp.zeros((tq, 1), F32), jnp.zeros((tq, HEAD_DIM), F32))
        carry = lax.fori_loop(0, qi, lambda kb, cr: step(kb, cr, False), init)
        m, l, acc = step(qi, carry, True)
        o_ref[...] = acc / l
        lse_ref[0] = m + jnp.log(l)

    return pl.pallas_call(
        body, name=name, grid=(n_heads, s_len // tq),
        in_specs=[pl.BlockSpec((tq, HEAD_DIM), lambda h, i: (i, h)),
                  pl.BlockSpec((s_len, HEAD_DIM), lambda h, i: (0, n_heads + h)),
                  pl.BlockSpec((s_len, HEAD_DIM), lambda h, i: (0, 2 * n_heads + h)),
                  pl.BlockSpec((tq, LANES), lambda h, i: (i, 0)),
                  pl.BlockSpec((hp, s_len), lambda h, i: (0, 0))],
        out_specs=[pl.BlockSpec((tq, HEAD_DIM), lambda h, i: (i, h)),
                   pl.BlockSpec((1, tq, 1), lambda h, i: (h, i, 0))],
        out_shape=[jax.ShapeDtypeStruct((s_len, n_heads * HEAD_DIM), F32),
                   jax.ShapeDtypeStruct((n_heads, s_len, 1), F32)],
        compiler_params=_params(("parallel", "arbitrary")))(pa, pa, pa, c, ct)


def _fox_bwd(name, pa, c, ct, o, do, lse, n_heads, tq=1024):
    s_len = pa.shape[0]
    tq = _tile(s_len, tq, LANES)
    hp = ct.shape[0]
    w_dim = n_heads * HEAD_DIM

    def body(q_ref, k_ref, v_ref, c_ref, ct_ref, o_ref, do_ref, lse_ref, dq_ref, dk_ref, dv_ref, dct_ref, dcq_ref):
        h, qi = pl.program_id(0), pl.program_id(1)

        @pl.when(qi == 0)
        def _():
            dk_ref[...] = jnp.zeros_like(dk_ref)
            dv_ref[...] = jnp.zeros_like(dv_ref)

        @pl.when((qi == 0) & (h == 0))
        def _():
            dct_ref[...] = jnp.zeros_like(dct_ref)

        q = q_ref[...]
        do32 = do_ref[...]
        dob = do32.astype(BF16)
        dsum = jnp.sum(do32 * o_ref[...], axis=1, keepdims=True)
        lse_v = lse_ref[0]
        cq = _head_col(c_ref[...], h)
        row = lax.broadcasted_iota(jnp.int32, (tq, tq), 0)
        col = lax.broadcasted_iota(jnp.int32, (tq, tq), 1)

        def step(kb, carry, masked):
            dq, dcq = carry
            ks = pl.multiple_of(kb * tq, tq)
            k = k_ref[pl.ds(ks, tq), :]
            v = v_ref[pl.ds(ks, tq), :]
            ck = ct_ref[pl.ds(h, 1), pl.ds(ks, tq)]
            s = lax.dot_general(q, k, NT_DIMS, preferred_element_type=F32) + (cq - ck)
            p = jnp.exp(s - lse_v)
            if masked:
                p = jnp.where(col <= row, p, 0.0)
            dp = lax.dot_general(dob, v, NT_DIMS, preferred_element_type=F32)
            ds = p * (dp - dsum)
            dsb = ds.astype(BF16)
            dk_ref[pl.ds(ks, tq), :] += lax.dot_general(dsb, q, TN_DIMS, preferred_element_type=F32)
            dv_ref[pl.ds(ks, tq), :] += lax.dot_general(p.astype(BF16), dob, TN_DIMS, preferred_element_type=F32)
            dct_ref[pl.ds(h, 1), pl.ds(ks, tq)] -= jnp.sum(ds, axis=0, keepdims=True)
            return dq + jnp.dot(dsb, k, preferred_element_type=F32), dcq + jnp.sum(ds, axis=1, keepdims=True)

        init = (jnp.zeros((tq, HEAD_DIM), F32), jnp.zeros((tq, 1), F32))
        carry = lax.fori_loop(0, qi, lambda kb, cr: step(kb, cr, False), init)
        dq, dcq = step(qi, carry, True)
        dq_ref[...] = dq * QK_SCALE
        dcq_ref[0] = dcq

    blk = pl.BlockSpec((tq, HEAD_DIM), lambda h, i: (i, h))
    full = pl.BlockSpec((s_len, HEAD_DIM), lambda h, i: (0, h))
    return pl.pallas_call(
        body, name=name, grid=(n_heads, s_len // tq),
        in_specs=[blk,
                  pl.BlockSpec((s_len, HEAD_DIM), lambda h, i: (0, n_heads + h)),
                  pl.BlockSpec((s_len, HEAD_DIM), lambda h, i: (0, 2 * n_heads + h)),
                  pl.BlockSpec((tq, LANES), lambda h, i: (i, 0)),
                  pl.BlockSpec((hp, s_len), lambda h, i: (0, 0)),
                  blk, blk,
                  pl.BlockSpec((1, tq, 1), lambda h, i: (h, i, 0))],
        out_specs=[blk, full, full, pl.BlockSpec((hp, s_len), lambda h, i: (0, 0)),
                   pl.BlockSpec((1, tq, 1), lambda h, i: (h, i, 0))],
        out_shape=[jax.ShapeDtypeStruct((s_len, w_dim), F32)] * 3 + [jax.ShapeDtypeStruct((hp, s_len), F32),
                                                                     jax.ShapeDtypeStruct((n_heads, s_len, 1), F32)],
        compiler_params=_params(("arbitrary", "arbitrary")))(pa, pa, pa, c, ct, o, do, lse)


def _sb_fwd(name, pa, n_heads, tq=1024, tk=256):
    s_len = pa.shape[0]
    tq = _tile(s_len, tq, LANES)
    tk = _tile(tq, tk, LANES)
    nsub = tq // tk

    def body(q_ref, k_ref, v_ref, o_ref, tot_ref):
        qi = pl.program_id(1)
        q = q_ref[...]
        u = _tri(tk, "ge")
        row = lax.broadcasted_iota(jnp.int32, (tq, tk), 0)
        col = lax.broadcasted_iota(jnp.int32, (tq, tk), 1)

        def block(ks, carry, mask_off):
            r, acc = carry
            k = k_ref[pl.ds(ks, tk), :]
            v = v_ref[pl.ds(ks, tk), :]
            z = lax.dot_general(q, k, NT_DIMS, preferred_element_type=F32)
            a, _ = _neg_softplus(z)
            if mask_off is not None:
                valid = col + mask_off < row
                a = jnp.where(valid, a, 0.0)
            rin = _split_dot(a, u, 2)
            w = jnp.exp(z + (r + rin))
            if mask_off is not None:
                w = jnp.where(valid, w, 0.0)
            acc = acc + jnp.dot(w.astype(BF16), v, preferred_element_type=F32)
            return r + rin[:, 0:1], acc

        carry = (jnp.zeros((tq, 1), F32), jnp.zeros((tq, HEAD_DIM), F32))
        q0 = pl.multiple_of(qi * tq, tq)
        for j in reversed(range(nsub)):
            carry = block(q0 + j * tk, carry, j * tk)
        n_left = qi * nsub
        carry = lax.fori_loop(
            0, n_left, lambda n, cr: block(pl.multiple_of((n_left - 1 - n) * tk, tk), cr, None), carry)
        r, acc = carry
        o_ref[...] = acc
        tot_ref[0] = r

    return pl.pallas_call(
        body, name=name, grid=(n_heads, s_len // tq),
        in_specs=[pl.BlockSpec((tq, HEAD_DIM), lambda h, i: (i, 3 * n_heads + h)),
                  pl.BlockSpec((s_len, HEAD_DIM), lambda h, i: (0, 4 * n_heads + h)),
                  pl.BlockSpec((s_len, HEAD_DIM), lambda h, i: (0, 5 * n_heads + h))],
        out_specs=[pl.BlockSpec((tq, HEAD_DIM), lambda h, i: (i, h)),
                   pl.BlockSpec((1, tq, 1), lambda h, i: (h, i, 0))],
        out_shape=[jax.ShapeDtypeStruct((s_len, n_heads * HEAD_DIM), F32),
                   jax.ShapeDtypeStruct((n_heads, s_len, 1), F32)],
        compiler_params=_params(("parallel", "arbitrary")))(pa, pa, pa)


def _sb_bwd(name, pa, do, tot, n_heads, tq=1024, tk=256):
    s_len = pa.shape[0]
    tq = _tile(s_len, tq, LANES)
    tk = _tile(tq, tk, LANES)
    nsub = tq // tk
    w_dim = n_heads * HEAD_DIM

    def body(q_ref, k_ref, v_ref, do_ref, tot_ref, dq_ref, dk_ref, dv_ref):
        qi = pl.program_id(1)

        @pl.when(qi == 0)
        def _():
            dk_ref[...] = jnp.zeros_like(dk_ref)
            dv_ref[...] = jnp.zeros_like(dv_ref)

        q = q_ref[...]
        dob = do_ref[...].astype(BF16)
        u = _tri(tk, "le")
        row = lax.broadcasted_iota(jnp.int32, (tq, tk), 0)
        col = lax.broadcasted_iota(jnp.int32, (tq, tk), 1)

        def block(ks, carry, mask_off):
            rem, cpre, dq = carry
            k = k_ref[pl.ds(ks, tk), :]
            v = v_ref[pl.ds(ks, tk), :]
            z = lax.dot_general(q, k, NT_DIMS, preferred_element_type=F32)
            a, e = _neg_softplus(z)
            if mask_off is not None:
                valid = col + mask_off < row
                a = jnp.where(valid, a, 0.0)
            pin = _split_dot(a, u, 2)
            w = jnp.exp(z + (rem - (pin - a)))
            if mask_off is not None:
                w = jnp.where(valid, w, 0.0)
            g = w * lax.dot_general(dob, v, NT_DIMS, preferred_element_type=F32)
            cin = _split_dot(g, u, 2)
            beta = jnp.where(z >= 0, 1.0, e) / (1.0 + e)
            dz = g - beta * (cpre + cin)
            if mask_off is not None:
                dz = jnp.where(valid, dz, 0.0)
            dzb = dz.astype(BF16)
            dk_ref[pl.ds(ks, tk), :] += lax.dot_general(dzb, q, TN_DIMS, preferred_element_type=F32)
            dv_ref[pl.ds(ks, tk), :] += lax.dot_general(w.astype(BF16), dob, TN_DIMS, preferred_element_type=F32)
            dq = dq + jnp.dot(dzb, k, preferred_element_type=F32)
            return rem - pin[:, tk - 1:tk], cpre + cin[:, tk - 1:tk], dq

        carry = (tot_ref[0], jnp.zeros((tq, 1), F32), jnp.zeros((tq, HEAD_DIM), F32))
        carry = lax.fori_loop(0, qi * nsub, lambda n, cr: block(pl.multiple_of(n * tk, tk), cr, None), carry)
        q0 = pl.multiple_of(qi * tq, tq)
        for j in range(nsub):
            carry = block(q0 + j * tk, carry, j * tk)
        dq_ref[...] = carry[2] * QK_SCALE

    blk = pl.BlockSpec((tq, HEAD_DIM), lambda h, i: (i, h))
    full = pl.BlockSpec((s_len, HEAD_DIM), lambda h, i: (0, h))
    return pl.pallas_call(
        body, name=name, grid=(n_heads, s_len // tq),
        in_specs=[pl.BlockSpec((tq, HEAD_DIM), lambda h, i: (i, 3 * n_heads + h)),
                  pl.BlockSpec((s_len, HEAD_DIM), lambda h, i: (0, 4 * n_heads + h)),
                  pl.BlockSpec((s_len, HEAD_DIM), lambda h, i: (0, 5 * n_heads + h)),
                  blk,
                  pl.BlockSpec((1, tq, 1), lambda h, i: (h, i, 0))],
        out_specs=[blk, full, full],
        out_shape=[jax.ShapeDtypeStruct((s_len, w_dim), F32)] * 3,
        compiler_params=_params(("arbitrary", "arbitrary")))(pa, pa, pa, do, tot)


def _conv_rows(ext_ref, u_ref, halo_ref, first, tm):
    ext_ref[0:8, :] = jnp.where(first, 0.0, halo_ref[...])
    ext_ref[8:, :] = u_ref[...]
    return ext_ref[8:8 + tm, :], ext_ref[7:7 + tm, :], ext_ref[6:6 + tm, :]


def _sigmoid(x):
    return 1.0 / (1.0 + jnp.exp(-x))


def _conv_specs(tm, tc, nj, order):
    hb = tm // 8
    ij = order

    def at(f):
        return lambda *g: f(*ij(*g))

    return [pl.BlockSpec((tm, tc), at(lambda i, j: (i, j))),
            pl.BlockSpec((tm, tc), at(lambda i, j: (i, j + nj))),
            pl.BlockSpec((8, tc), at(lambda i, j: (jnp.maximum(i * hb - 1, 0), j))),
            pl.BlockSpec((8, tc), at(lambda i, j: (jnp.maximum(i * hb - 1, 0), j + nj))),
            pl.BlockSpec((3, tc), at(lambda i, j: (0, j))),
            pl.BlockSpec((3, tc), at(lambda i, j: (0, j + nj))),
            pl.BlockSpec((1, tc), at(lambda i, j: (0, j))),
            pl.BlockSpec((1, tc), at(lambda i, j: (0, j + nj)))]


def _conv_gate_fwd(name, u, cw, cb, tm=512, tc=512):
    s_len, f2 = u.shape
    f_dim = f2 // 2
    tm = _tile(s_len, tm, 8)
    tc = _tile(f_dim, tc)
    nj = f_dim // tc

    def body(ug_ref, uv_ref, hg_ref, hv_ref, wg_ref, wv_ref, bg_ref, bv_ref, g_ref, ext_ref):
        first = pl.program_id(0) == 0

        def conv(u_ref, h_ref, w_ref, b_ref):
            u0, u1, u2 = _conv_rows(ext_ref, u_ref, h_ref, first, tm)
            return w_ref[2:3, :] * u0 + w_ref[1:2, :] * u1 + w_ref[0:1, :] * u2 + b_ref[...]

        gc = conv(ug_ref, hg_ref, wg_ref, bg_ref)
        vc = conv(uv_ref, hv_ref, wv_ref, bv_ref)
        g_ref[...] = (gc * _sigmoid(gc) * vc).astype(BF16)

    return pl.pallas_call(
        body, name=name, grid=(s_len // tm, nj),
        in_specs=_conv_specs(tm, tc, nj, lambda i, j: (i, j)),
        out_specs=pl.BlockSpec((tm, tc), lambda i, j: (i, j)),
        out_shape=jax.ShapeDtypeStruct((s_len, f_dim), BF16),
        scratch_shapes=[pltpu.VMEM((tm + 8, tc), F32)],
        compiler_params=_params(("parallel", "parallel")))(u, u, u, u, cw, cw, cb, cb)


def _conv_gate_bwd(name, u, dg, cw, cb, tm=512, tc=512):
    s_len, f2 = u.shape
    f_dim = f2 // 2
    tm = _tile(s_len, tm, 8)
    tc = _tile(f_dim, tc)
    nj = f_dim // tc

    def body(ug_ref, uv_ref, hg_ref, hv_ref, wg_ref, wv_ref, bg_ref, bv_ref, dg_ref, duc_ref, dcw_ref, dcb_ref,
             eg_ref, ev_ref):
        first = pl.program_id(1) == 0

        @pl.when(first)
        def _():
            dcw_ref[...] = jnp.zeros_like(dcw_ref)
            dcb_ref[...] = jnp.zeros_like(dcb_ref)

        ug = _conv_rows(eg_ref, ug_ref, hg_ref, first, tm)
        uv = _conv_rows(ev_ref, uv_ref, hv_ref, first, tm)
        gc = wg_ref[2:3, :] * ug[0] + wg_ref[1:2, :] * ug[1] + wg_ref[0:1, :] * ug[2] + bg_ref[...]
        vc = wv_ref[2:3, :] * uv[0] + wv_ref[1:2, :] * uv[1] + wv_ref[0:1, :] * uv[2] + bv_ref[...]
        sg = _sigmoid(gc)
        dgv = dg_ref[...]
        dvc = dgv * (gc * sg)
        dgc = dgv * vc * (sg * (1.0 + gc * (1.0 - sg)))
        duc_ref[0] = dgc
        duc_ref[1] = dvc
        for half, (d, us) in enumerate(((dgc, ug), (dvc, uv))):
            dcb_ref[half] += jnp.sum(d, axis=0, keepdims=True)
            for tap in range(3):
                dcw_ref[half, tap:tap + 1, :] += jnp.sum(d * us[2 - tap], axis=0, keepdims=True)

    order = lambda j, i: (i, j)
    return pl.pallas_call(
        body, name=name, grid=(nj, s_len // tm),
        in_specs=_conv_specs(tm, tc, nj, order) + [pl.BlockSpec((tm, tc), lambda j, i: (i, j))],
        out_specs=[pl.BlockSpec((2, tm, tc), lambda j, i: (0, i, j)),
                   pl.BlockSpec((2, 3, tc), lambda j, i: (0, 0, j)),
                   pl.BlockSpec((2, 1, tc), lambda j, i: (0, 0, j))],
        out_shape=[jax.ShapeDtypeStruct((2, s_len, f_dim), F32), jax.ShapeDtypeStruct((2, 3, f_dim), F32),
                   jax.ShapeDtypeStruct((2, 1, f_dim), F32)],
        scratch_shapes=[pltpu.VMEM((tm + 8, tc), F32), pltpu.VMEM((tm + 8, tc), F32)],
        compiler_params=_params(("parallel", "arbitrary")))(u, u, u, u, cw, cw, cb, cb, dg)


def _conv_t(name, duc, cw, tm=512, tc=512):
    _, s_len, f_dim = duc.shape
    tm = _tile(s_len, tm, 8)
    tc = _tile(f_dim, tc)
    nj = f_dim // tc
    nb = s_len // tm
    hb = tm // 8

    def body(d_ref, halo_ref, w_ref, o_ref, ext_ref):
        last = pl.program_id(1) == nb - 1
        ext_ref[0:tm, :] = d_ref[0]
        ext_ref[tm:, :] = jnp.where(last, 0.0, halo_ref[0])
        o_ref[...] = (w_ref[2:3, :] * ext_ref[0:tm, :] + w_ref[1:2, :] * ext_ref[1:1 + tm, :]
                      + w_ref[0:1, :] * ext_ref[2:2 + tm, :]).astype(BF16)

    return pl.pallas_call(
        body, name=name, grid=(2, nb, nj),
        in_specs=[pl.BlockSpec((1, tm, tc), lambda p, i, j: (p, i, j)),
                  pl.BlockSpec((1, 8, tc), lambda p, i, j: (p, jnp.minimum((i + 1) * hb, nb * hb - 1), j)),
                  pl.BlockSpec((3, tc), lambda p, i, j: (0, p * nj + j))],
        out_specs=pl.BlockSpec((tm, tc), lambda p, i, j: (i, p * nj + j)),
        out_shape=jax.ShapeDtypeStruct((s_len, 2 * f_dim), BF16),
        scratch_shapes=[pltpu.VMEM((tm + 8, tc), F32)],
        compiler_params=_params(("parallel", "parallel", "parallel")))(duc, duc, cw)


def _loss_head(name, y, tgt, tm=512):
    s_len, d = y.shape
    tm = _tile(s_len, tm, 8)

    def body(y_ref, t_ref, dy_ref, l_ref):
        @pl.when(pl.program_id(0) == 0)
        def _():
            l_ref[...] = jnp.zeros_like(l_ref)

        err = y_ref[...] - t_ref[...]
        dy_ref[...] = err * (1.0 / d)
        l_ref[...] += 0.5 * jnp.sum(jnp.sum(err * err, axis=1, keepdims=True) * (1.0 / d), axis=0, keepdims=True)

    blk = pl.BlockSpec((tm, d), lambda i: (i, 0))
    return pl.pallas_call(
        body, name=name, grid=(s_len // tm,), in_specs=[blk, blk],
        out_specs=[blk, pl.BlockSpec((1, LANES), lambda i: (0, 0))],
        out_shape=[jax.ShapeDtypeStruct((s_len, d), F32), jax.ShapeDtypeStruct((1, LANES), F32)],
        compiler_params=_params(("arbitrary",)))(y, tgt)


def _adamw(name, parts_list, w, m, v, tr=256):
    n_l = len(parts_list)
    n_parts, rows, cols = parts_list[0].shape
    row_bytes = -(-cols // LANES) * LANES * (2 * n_l * n_parts * parts_list[0].dtype.itemsize + 2 * 7 * 4)
    tr = _tile(rows, min(tr, max(16, ADAMW_VMEM // row_bytes // 16 * 16)), 16)
    nb = rows // tr
    c1 = 1.0 - ADAM_B1 ** ADAM_STEP
    c2 = 1.0 - ADAM_B2 ** ADAM_STEP

    def body(*refs):
        p_refs = refs[:n_l]
        w_ref, m_ref, v_ref, g_ref, d_ref, nm_ref, nv_ref = refs[n_l:]
        for l, p_ref in enumerate(p_refs):
            @pl.when(pl.program_id(0) == l)
            def _(p_ref=p_ref):
                g = p_ref[0].astype(F32)
                for n in range(1, n_parts):
                    g = g + p_ref[n].astype(F32)
                nm = ADAM_B1 * m_ref[...] + (1.0 - ADAM_B1) * g
                nv = ADAM_B2 * v_ref[...] + (1.0 - ADAM_B2) * (g * g)
                g_ref[...] = g
                nm_ref[...] = nm
                nv_ref[...] = nv
                d_ref[...] = -ADAM_LR * ((nm / c1) / (jnp.sqrt(nv / c2) + ADAM_EPS) + ADAM_WD * w_ref[...])

    p_specs = [pl.BlockSpec((n_parts, tr, cols), lambda li, i, l=l: (0, jnp.where(li == l, i, 0), 0))
               for l in range(n_l)]
    blk = pl.BlockSpec((tr, cols), lambda li, i: (li * nb + i, 0))
    return pl.pallas_call(
        body, name=name, grid=(n_l, nb), in_specs=p_specs + [blk, blk, blk],
        out_specs=[blk] * 4, out_shape=[jax.ShapeDtypeStruct((n_l * rows, cols), F32)] * 4,
        compiler_params=_params(("arbitrary", "arbitrary")))(*parts_list, w, m, v)


def _peers():
    x, y, c = lax.axis_index("x"), lax.axis_index("y"), lax.axis_index("c")
    out = []
    for k in range(1, N_DEV):
        fx, fy, fc = (k >> 2) & 1, (k >> 1) & 1, k & 1
        px, py, pc = x ^ fx, y ^ fy, c ^ fc
        out.append((k - 1, (px, py, pc), 4 * px + 2 * py + pc))
    return 4 * x + 2 * y + c, out


def _block_of(ref, axis, index, size):
    if axis is None:
        return ref.at[index]
    idx = [slice(None)] * len(ref.shape)
    idx[axis] = pl.ds(pl.multiple_of(index * size, size), size)
    return ref.at[tuple(idx)]


def _land_shape(shape, axis, scatter):
    shape = list(shape)
    if scatter:
        if axis is None:
            return tuple(shape)
        shape[axis] //= N_DEV
        return (N_DEV, *shape)
    if axis is None:
        return (N_DEV, *shape)
    shape[axis] *= N_DEV
    return tuple(shape)


def _copy_ends(axis, scatter, src, land, me, idx):
    if scatter:
        size = None if axis is None else src.shape[axis] // N_DEV
        return _block_of(src, axis, idx, size), land.at[me]
    return src, _block_of(land, axis, me, None if axis is None else src.shape[axis])


def _remote_copies(axes, scatter, in_refs, land_refs, send_sems, recv_sems):
    me, peers = _peers()
    out = []
    for a, (axis, src, land) in enumerate(zip(axes, in_refs, land_refs)):
        for k, pos, idx in peers:
            s, d = _copy_ends(axis, scatter, src, land, me, idx)
            out.append(pltpu.make_async_remote_copy(
                src_ref=s, dst_ref=d, send_sem=send_sems.at[a * (N_DEV - 1) + k],
                recv_sem=recv_sems.at[a * (N_DEV - 1) + k],
                device_id=pos, device_id_type=MESH))
    return out


def _exchange(name, arrays, axes, scatter):
    n = len(arrays)

    def body(*refs):
        in_refs, land_refs = refs[:n], refs[n:2 * n]
        send_sems, recv_sems, local_sems = refs[2 * n:]
        me, _ = _peers()
        copies = _remote_copies(axes, scatter, in_refs, land_refs, send_sems, recv_sems)
        for a, (axis, src, land) in enumerate(zip(axes, in_refs, land_refs)):
            s, d = _copy_ends(axis, scatter, src, land, me, me)
            copies.append(pltpu.make_async_copy(s, d, local_sems.at[a]))
        for cp in copies:
            cp.start()
        for cp in copies:
            cp.wait()

    any_spec = pl.BlockSpec(memory_space=pl.ANY)
    return pl.pallas_call(
        body, name=name, in_specs=[any_spec] * n, out_specs=[any_spec] * n,
        out_shape=[jax.ShapeDtypeStruct(_land_shape(a.shape, ax, scatter), a.dtype) for a, ax in zip(arrays, axes)],
        scratch_shapes=[pltpu.SemaphoreType.DMA((n * (N_DEV - 1),)), pltpu.SemaphoreType.DMA((n * (N_DEV - 1),)),
                        pltpu.SemaphoreType.DMA((n,))],
        compiler_params=pltpu.CompilerParams(has_side_effects=True))(*arrays)


HBM_SPEC = pl.BlockSpec(memory_space=pltpu.HBM)
SEM_SPEC = pl.BlockSpec(memory_space=pltpu.SEMAPHORE)
DATAFLOW = pltpu.SideEffectType.DATAFLOW_SIDE_EFFECTING


def _exchange_start(name, arrays, axes, scatter, after=None):
    n = len(arrays)
    extra = [] if after is None else [after]

    def body(*refs):
        in_refs, land_refs = refs[:n], refs[n:2 * n]
        send_sems, recv_sems = refs[2 * n + len(extra)], refs[2 * n + len(extra) + 1]
        token = refs[-1]
        for cp in _remote_copies(axes, scatter, in_refs, land_refs, send_sems, recv_sems):
            cp.start()
        token[...] = jnp.zeros_like(token)

    lands = [lax.empty(_land_shape(a.shape, ax, scatter), a.dtype) for a, ax in zip(arrays, axes)]
    hbm = lambda a: pltpu.HBM(a.shape, a.dtype)
    res = pl.pallas_call(
        body, name=name, in_specs=[HBM_SPEC] * (2 * n) + [pl.BlockSpec(memory_space=pl.ANY)] * len(extra),
        out_specs=[SEM_SPEC, SEM_SPEC] + [HBM_SPEC] * (2 * n) + [pl.BlockSpec(memory_space=pltpu.VMEM)],
        out_shape=[pltpu.SemaphoreType.DMA((n * (N_DEV - 1),)), pltpu.SemaphoreType.DMA((n * (N_DEV - 1),))]
        + [hbm(a) for a in arrays] + [hbm(a) for a in lands] + [jax.ShapeDtypeStruct((8, LANES), F32)],
        input_output_aliases={i: i + 2 for i in range(2 * n)},
        compiler_params=pltpu.CompilerParams(has_side_effects=DATAFLOW))(
            *[pltpu.with_memory_space_constraint(a, pltpu.HBM) for a in list(arrays) + lands], *extra)
    return res[:-1], res[-1]


def _exchange_wait(name, flight, axes, scatter, after):
    send_sems, recv_sems = flight[0], flight[1]
    n = (len(flight) - 2) // 2
    arrays, lands = flight[2:2 + n], flight[2 + n:]

    def body(*refs):
        in_refs, land_refs = refs[:n], refs[n:2 * n]
        send_sems, recv_sems = refs[2 * n], refs[2 * n + 1]
        for cp in _remote_copies(axes, scatter, in_refs, land_refs, send_sems, recv_sems):
            cp.wait_send()
            cp.wait_recv()

    res = pl.pallas_call(
        body, name=name, in_specs=[HBM_SPEC] * (2 * n) + [SEM_SPEC, SEM_SPEC, pl.BlockSpec(memory_space=pl.ANY)],
        out_specs=[HBM_SPEC] * (2 * n),
        out_shape=[pltpu.HBM(a.shape, a.dtype) for a in list(arrays) + list(lands)],
        input_output_aliases={i: i for i in range(2 * n)},
        compiler_params=pltpu.CompilerParams(has_side_effects=DATAFLOW))(*arrays, *lands, send_sems, recv_sems, after)
    return res[n:]


def _keep_own(land, own, axis, me):
    if axis is None:
        return lax.dynamic_update_index_in_dim(land, own, me, 0)
    return lax.dynamic_update_slice_in_dim(land, own, me * own.shape[axis], axis)


def _row(vec, width=None):
    vec = vec.reshape(1, -1)
    if width is not None and vec.shape[1] < width:
        vec = jnp.pad(vec, ((0, 0), (0, width - vec.shape[1])))
    return vec


def _layer_fwd(l, x, sp, bp, token=None):
    n_heads = x.shape[1] // (2 * HEAD_DIM)
    g_attn = _row(sp["attn_norm"])
    if token is not None:
        g_attn = g_attn + token[0:1, 0:1]
    proj, h = _mm_nn(f"l{l}_in_proj", [x], [g_attn], bp["w_in"])
    qg, kg, bf = _row(sp["q_norm"]), _row(sp["k_norm"]), _row(sp["b_forget"], LANES)
    pa, c = _prep_fwd(f"l{l}_prep", proj, qg, kg, bf, n_heads)
    ct = c[:, :n_heads].T
    o_a, lse = _fox_fwd(f"l{l}_fox", pa, c, ct, n_heads)
    o_b, tot = _sb_fwd(f"l{l}_sb", pa, n_heads)
    gfox, gsb = _row(sp["out_norm_fox"]), _row(sp["out_norm_sb"])
    x1, merged = _mm_nn(f"l{l}_out_proj", [o_a, o_b], [gfox, gsb], bp["w_out"], resid=x)
    u, h2 = _mm_nn(f"l{l}_up_proj", [x1], [_row(sp["ffn_norm"])], bp["w_up"])
    cb = _row(sp["conv_b"])
    g = _conv_gate_fwd(f"l{l}_conv_gate", u, bp["conv_w"], cb)
    x2 = _mm_nn(f"l{l}_down_proj", [g], [], bp["w_down"], resid=x1, tn_cap=512)
    saved = dict(x=x, h=h, proj=proj, pa=pa, c=c, ct=ct, o_a=o_a, lse=lse, o_b=o_b, tot=tot, merged=merged, x1=x1,
                 h2=h2, u=u, g=g, qg=qg, kg=kg, bf=bf, gfox=gfox, gsb=gsb, cb=cb)
    return x2, saved


def _layer_bwd(l, dx, sp, bp, sv, token=None):
    n_heads = dx.shape[1] // (2 * HEAD_DIM)
    gr = {}
    dg = _mm_nt(f"l{l}_d_down_act", dx, bp["w_down"])
    gr["w_down"] = _mm_tn(f"l{l}_d_w_down", sv["g"], dx)
    cb = sv["cb"] if token is None else sv["cb"] + token[0:1, 0:1]
    duc, dcw, dcb = _conv_gate_bwd(f"l{l}_d_conv_gate", sv["u"], dg, bp["conv_w"], cb)
    gr["conv_w"] = dcw.transpose(1, 0, 2).reshape(3, -1)
    gr["conv_b"] = dcb.reshape(-1)
    du = _conv_t(f"l{l}_d_conv", duc, bp["conv_w"])
    gr["w_up"] = _mm_tn(f"l{l}_d_w_up", sv["h2"], du)
    dx1, dffn = _mm_nt(f"l{l}_d_up_act", du, bp["w_up"], "rms_bwd", [sv["x1"]], [_row(sp["ffn_norm"])], dres=dx)
    gr["ffn_norm"] = dffn.reshape(-1)
    gr["w_out"] = _mm_tn(f"l{l}_d_w_out", sv["merged"], dx1)
    do_a, do_b, dgfox, dgsb = _mm_nt(f"l{l}_d_out_act", dx1, bp["w_out"], "rms2_bwd",
                                     [sv["o_a"], sv["o_b"]], [sv["gfox"], sv["gsb"]])
    gr["out_norm_fox"], gr["out_norm_sb"] = dgfox.reshape(-1), dgsb.reshape(-1)
    dq_a, dk_a, dv_a, dct, dcq = _fox_bwd(f"l{l}_d_fox", sv["pa"], sv["c"], sv["ct"], sv["o_a"], do_a, sv["lse"],
                                          n_heads)
    d_sb = _sb_bwd(f"l{l}_d_sb", sv["pa"], do_b, sv["tot"], n_heads)
    dc = jnp.pad((dct + dcq[:, :, 0]).T, ((0, 0), (0, LANES - n_heads)))
    dproj, dqg, dkg, dbf = _prep_bwd(f"l{l}_d_prep", sv["proj"], sv["qg"], sv["kg"], sv["bf"],
                                     (dq_a, dk_a, dv_a), d_sb, dc, n_heads)
    gr["q_norm"], gr["k_norm"], gr["b_forget"] = dqg.reshape(-1), dkg.reshape(-1), dbf.reshape(-1)[:n_heads]
    gr["w_in"] = _mm_tn(f"l{l}_d_w_in", sv["h"], dproj)
    dx0, dattn = _mm_nt(f"l{l}_d_in_act", dproj, bp["w_in"], "rms_bwd", [sv["x"]], [_row(sp["attn_norm"])],
                        dres=dx1)
    gr["attn_norm"] = dattn.reshape(-1)
    return dx0, gr


def _local_step(x, tgt, small, big):
    n_layers = len(big)
    saved = []
    for l in range(n_layers):
        x, sv = _layer_fwd(l, x, small[l], big[l])
        saved.append(sv)
    dx, loss_part = _loss_head("loss_head", x, tgt)
    grads = [None] * n_layers
    for l in reversed(range(n_layers)):
        dx, grads[l] = _layer_bwd(l, dx, small[l], big[l], saved[l])
    return loss_part, dx, grads


def _w_in_to_internal(w, n_heads):
    w3 = 3 * n_heads * HEAD_DIM
    pad = jnp.zeros(w.shape[:-1] + (LANES - n_heads,), w.dtype)
    return jnp.concatenate([w[..., :w3], w[..., w3 + n_heads:], w[..., w3:w3 + n_heads], pad], axis=-1)


def _w_in_from_internal(w, n_heads):
    w3 = 3 * n_heads * HEAD_DIM
    return jnp.concatenate([w[..., :w3], w[..., 2 * w3:2 * w3 + n_heads], w[..., w3:2 * w3]], axis=-1)


SMALL = ("attn_norm", "b_forget", "q_norm", "k_norm", "out_norm_fox", "out_norm_sb", "ffn_norm", "conv_b")
BIG = ("w_in", "w_out", "w_up", "w_down")
WEIGHTS = ("attn_norm", "w_in", "b_forget", "q_norm", "k_norm", "out_norm_fox", "out_norm_sb", "w_out", "ffn_norm",
           "w_up", "conv_w", "conv_b", "w_down")


def kernel(x, attn_norm, w_in, b_forget, q_norm, k_norm, out_norm_fox, out_norm_sb, w_out, ffn_norm, w_up, conv_w, conv_b, w_down, loss_target, m_attn_norm, m_w_in, m_b_forget, m_q_norm, m_k_norm, m_out_norm_fox, m_out_norm_sb, m_w_out, m_ffn_norm, m_w_up, m_conv_w, m_conv_b, m_w_down, v_attn_norm, v_w_in, v_b_forget, v_q_norm, v_k_norm, v_out_norm_fox, v_out_norm_sb, v_w_out, v_ffn_norm, v_w_up, v_conv_w, v_conv_b, v_w_down):
    w = dict(attn_norm=attn_norm, w_in=w_in, b_forget=b_forget, q_norm=q_norm, k_norm=k_norm,
             out_norm_fox=out_norm_fox, out_norm_sb=out_norm_sb, w_out=w_out, ffn_norm=ffn_norm, w_up=w_up,
             conv_w=conv_w, conv_b=conv_b, w_down=w_down)
    mom = dict(attn_norm=m_attn_norm, w_in=m_w_in, b_forget=m_b_forget, q_norm=m_q_norm, k_norm=m_k_norm,
               out_norm_fox=m_out_norm_fox, out_norm_sb=m_out_norm_sb, w_out=m_w_out, ffn_norm=m_ffn_norm,
               w_up=m_w_up, conv_w=m_conv_w, conv_b=m_conv_b, w_down=m_w_down)
    var = dict(attn_norm=v_attn_norm, w_in=v_w_in, b_forget=v_b_forget, q_norm=v_q_norm, k_norm=v_k_norm,
               out_norm_fox=v_out_norm_fox, out_norm_sb=v_out_norm_sb, w_out=v_w_out, ffn_norm=v_ffn_norm,
               w_up=v_w_up, conv_w=v_conv_w, conv_b=v_conv_b, w_down=v_w_down)
    n_layers, d = attn_norm.shape
    n_heads = d // (2 * HEAD_DIM)
    me = 4 * lax.axis_index("x") + 2 * lax.axis_index("y") + lax.axis_index("c")

    shard = {k: w[k].astype(BF16) for k in BIG}
    g0_in, g0_out, g0_up, g0_down, full_cw = _exchange(
        "gather_l0", [shard[k][0] for k in BIG] + [w["conv_w"]], [None, 0, 1, 0, 2], scatter=False)
    rest_axes = [None, 1, 2, 1]
    flight, token = _exchange_start("gather_rest_start", [shard[k][1:] for k in BIG], rest_axes, scatter=False,
                                    after=g0_in)
    small = [{k: w[k][l] for k in SMALL} for l in range(n_layers)]
    big = [dict(w_in=_w_in_to_internal(g0_in.transpose(1, 0, 2).reshape(d, -1), n_heads), w_out=g0_out, w_up=g0_up,
                w_down=g0_down, conv_w=full_cw[0])]

    saved = [None] * n_layers
    act, saved[0] = _layer_fwd(0, x[0], small[0], big[0], token)
    rest = _exchange_wait("gather_rest_wait", flight, rest_axes, False, after=act)
    rest = [_keep_own(land, shard[k][1:], ax, me) for land, k, ax in zip(rest, BIG, rest_axes)]
    rest[0] = _w_in_to_internal(rest[0].transpose(1, 2, 0, 3).reshape(n_layers - 1, d, -1), n_heads)
    for l in range(1, n_layers):
        big.append(dict(w_in=rest[0][l - 1], w_out=rest[1][l - 1], w_up=rest[2][l - 1], w_down=rest[3][l - 1],
                        conv_w=full_cw[l]))
        act, saved[l] = _layer_fwd(l, act, small[l], big[l])
    dx, loss_part = _loss_head("loss_head", act, loss_target[0])
    grads, flights, sends = [None] * n_layers, [None] * n_layers, [None] * n_layers
    send_axes = [None, 0, 1, 0]
    token = None
    for l in reversed(range(n_layers)):
        dx, grads[l] = _layer_bwd(l, dx, small[l], big[l], saved[l], token)
        g_in = _w_in_from_internal(grads[l]["w_in"], n_heads)
        sends[l] = [g_in.reshape(d, N_DEV, -1).transpose(1, 0, 2), grads[l]["w_out"], grads[l]["w_up"],
                    grads[l]["w_down"]]
        if l > 0:
            flights[l], token = _exchange_start(f"scatter_l{l}_start", sends[l], send_axes, scatter=True)
    grad_x = dx
    parts = [None] * n_layers
    parts[0] = _exchange("scatter_l0", sends[0], send_axes, scatter=True)
    for l in reversed(range(1, n_layers)):
        lands = _exchange_wait(f"scatter_l{l}_wait", flights[l], send_axes, True, after=grad_x)
        parts[l] = []
        for land, a, ax in zip(lands, sends[l], send_axes):
            own = (lax.dynamic_index_in_dim(a, me, 0, keepdims=False) if ax is None
                   else lax.dynamic_slice_in_dim(a, me * (a.shape[ax] // N_DEV), a.shape[ax] // N_DEV, ax))
            parts[l].append(_keep_own(land, own, None, me))

    stack = lambda k: jnp.stack([grads[l][k] for l in range(n_layers)])
    small_names = SMALL + ("conv_w",)
    flat = jnp.concatenate([loss_part.reshape(-1)] + [
        jnp.pad(stack(k).reshape(-1), (0, (-stack(k).size) % LANES)) for k in small_names])
    flat = flat.reshape(-1, LANES)
    (all_small,) = _exchange("gather_small_grads", [flat], [None], scatter=False)

    out = {}

    def update(name, parts_list):
        rc = (-1, parts_list[0].shape[-1])
        res = _adamw("adamw_" + name, parts_list, w[name].reshape(rc), mom[name].reshape(rc), var[name].reshape(rc))
        out[name] = [r.reshape(w[name].shape) for r in res]

    for n, name in enumerate(BIG):
        update(name, [parts[l][n] for l in range(n_layers)])
    n_rows = flat.shape[0]
    w_flat, m_flat, v_flat = [], [], []
    for src, dst in ((w, w_flat), (mom, m_flat), (var, v_flat)):
        dst.append(jnp.zeros((LANES,), F32))
        for k in small_names:
            a = src[k]
            if k == "conv_w":
                a = jnp.zeros((n_layers, 3, conv_w.shape[2] * N_DEV), F32)
            dst.append(jnp.pad(a.reshape(-1), (0, (-a.size) % LANES)))
    pack = lambda parts: jnp.concatenate(parts).reshape(n_rows, LANES)
    res = _adamw("adamw_small", [all_small], pack(w_flat), pack(m_flat), pack(v_flat))
    res = [r.reshape(-1) for r in res]
    loss = res[0][0]
    off = LANES
    g_cw_full = None
    for k in small_names:
        size = n_layers * 3 * conv_w.shape[2] * N_DEV if k == "conv_w" else w[k].size
        if k == "conv_w":
            g_cw_full = res[0][off:off + size].reshape(n_layers, 3, -1)
        else:
            out[k] = [r[off:off + size].reshape(w[k].shape) for r in res]
        off += size + (-size) % LANES
    c_loc = conv_w.shape[2]
    g_cw_mine = lax.dynamic_slice_in_dim(g_cw_full, me * c_loc, c_loc, axis=2)
    update("conv_w", [g_cw_mine.reshape(1, n_layers * 3, c_loc)])

    outs = [loss, grad_x[None]]
    for n in range(4):
        outs += [out[k][n] for k in WEIGHTS]
    return tuple(outs)
```

```python
import functools

import jax
import jax.numpy as jnp
from jax import lax
from jax.experimental import pallas as pl
from jax.experimental.pallas import tpu as pltpu

F32 = jnp.float32
BF16 = jnp.bfloat16
HEAD_DIM = 128
QK_SCALE = HEAD_DIM ** -0.5
SB_SKIP = 110.0
SB_BOUND_SLACK = 1.0 + 2.0 ** -6
LANES = 128
EPS = 1e-6
N_DEV = 8
ADAM_LR = 0.001
ADAM_B1 = 0.9
ADAM_B2 = 0.999
ADAM_EPS = 1e-08
ADAM_WD = 0.01
ADAM_STEP = 10
VMEM_LIMIT = 56 * 1024 * 1024
ADAMW_VMEM = 24 * 1024 * 1024
MESH = pl.DeviceIdType.MESH

NT_DIMS = (((1,), (1,)), ((), ()))
TN_DIMS = (((0,), (0,)), ((), ()))


def _tile(n, cap, mult=LANES):
    t = (min(cap, n) // mult) * mult
    while t >= mult:
        if n % t == 0:
            return t
        t -= mult
    return n


def _params(sem, vmem=VMEM_LIMIT):
    return pltpu.CompilerParams(dimension_semantics=sem, vmem_limit_bytes=vmem)


def _split_dot(x, u, n_split, x_left=True):
    acc = None
    rest = x
    for s in range(n_split):
        piece = rest.astype(BF16)
        if s + 1 < n_split:
            rest = rest - piece.astype(F32)
        d = (jnp.dot(piece, u, preferred_element_type=F32) if x_left
             else jnp.dot(u, piece, preferred_element_type=F32))
        acc = d if acc is None else acc + d
    return acc


def _tri(n, kind):
    r = lax.broadcasted_iota(jnp.int32, (n, n), 0)
    c = lax.broadcasted_iota(jnp.int32, (n, n), 1)
    return jnp.where(r >= c if kind == "ge" else r <= c, 1.0, 0.0).astype(BF16)


def _mm_nn(name, a_list, g_list, w, resid=None, tm=1024, tn_cap=1024, ride=None):
    s_len = a_list[0].shape[0]
    k_dim, n_dim = w.shape
    tm = _tile(s_len, tm, 8)
    tn = _tile(n_dim, tn_cap)
    normed = bool(g_list)

    def body(*refs):
        refs = list(refs)
        a_refs = [refs.pop(0) for _ in a_list]
        g_refs = [refs.pop(0) for _ in g_list]
        w_ref = refs.pop(0)
        r_ref = refs.pop(0) if resid is not None else None
        o_ref = refs.pop(0)
        if normed:
            h_ref = refs.pop(0)

            @pl.when(pl.program_id(1) == 0)
            def _():
                off = 0
                for a_ref, g_ref in zip(a_refs, g_refs):
                    xv = a_ref[...]
                    kk = xv.shape[1]
                    r = lax.rsqrt(jnp.mean(xv * xv, axis=1, keepdims=True) + EPS)
                    h_ref[:, off:off + kk] = (xv * r * g_ref[...]).astype(BF16)
                    off += kk

            a = h_ref[...]
        else:
            a = a_refs[0][...]
        acc = jnp.dot(a, w_ref[...], preferred_element_type=F32)
        if r_ref is not None:
            acc = acc + r_ref[...]
        o_ref[...] = acc

    in_specs = [pl.BlockSpec((tm, a.shape[1]), lambda i, j: (i, 0)) for a in a_list]
    in_specs += [pl.BlockSpec((1, g.shape[1]), lambda i, j: (0, 0)) for g in g_list]
    in_specs += [pl.BlockSpec((k_dim, tn), lambda i, j: (0, j))]
    args = list(a_list) + list(g_list) + [w]
    if resid is not None:
        in_specs.append(pl.BlockSpec((tm, tn), lambda i, j: (i, j)))
        args.append(resid)
    out_shape = [jax.ShapeDtypeStruct((s_len, n_dim), F32)]
    out_specs = [pl.BlockSpec((tm, tn), lambda i, j: (i, j))]
    if normed:
        out_shape.append(jax.ShapeDtypeStruct((s_len, k_dim), BF16))
        out_specs.append(pl.BlockSpec((tm, k_dim), lambda i, j: (i, 0)))
    res, lands = _hosted_call(body, ride, name=name, grid=(s_len // tm, n_dim // tn), in_specs=in_specs,
                              out_specs=out_specs, out_shape=out_shape, sem=("parallel", "arbitrary"), args=args)
    res = res if normed else res[0]
    return res if ride is None else (res, lands)


def _rms_bwd(dh, xv, gv, r=None):
    if r is None:
        r = lax.rsqrt(jnp.mean(xv * xv, axis=1, keepdims=True) + EPS)
    xhat = xv * r
    dxh = dh * gv
    dx = r * (dxh - xhat * jnp.mean(dxh * xhat, axis=1, keepdims=True))
    return dx, dh * xhat


def _mm_nt(name, da, w, mode="plain", xs=(), gs=(), dres=None, tm=512, tk_cap=1024, ride=None):
    s_len, kc = da.shape
    n_out = w.shape[0]
    tm = _tile(s_len, tm, 8)
    tk = _tile(kc, tk_cap)
    nk = kc // tk

    def body(*refs):
        refs = list(refs)
        da_ref, w_ref = refs.pop(0), refs.pop(0)
        x_refs = [refs.pop(0) for _ in xs]
        g_refs = [refs.pop(0) for _ in gs]
        dres_ref = refs.pop(0) if dres is not None else None
        acc_ref = refs[0] if mode == "plain" else refs.pop()
        out_refs = refs
        i, k = pl.program_id(0), pl.program_id(1)

        @pl.when(k == 0)
        def _():
            acc_ref[...] = jnp.zeros_like(acc_ref)

        acc_ref[...] += lax.dot_general(da_ref[...].astype(BF16), w_ref[...], NT_DIMS,
                                        preferred_element_type=F32)

        if mode == "plain":
            return

        @pl.when(k == nk - 1)
        def _():
            n_x = len(xs)
            dx_refs, dg_refs = out_refs[:n_x], out_refs[n_x:]

            @pl.when(i == 0)
            def _():
                for dg_ref in dg_refs:
                    dg_ref[...] = jnp.zeros_like(dg_ref)

            off = 0
            for x_ref, g_ref, dx_ref, dg_ref in zip(x_refs, g_refs, dx_refs, dg_refs):
                kk = x_ref.shape[1]
                dx, dgp = _rms_bwd(acc_ref[:, off:off + kk], x_ref[...], g_ref[...])
                if dres_ref is not None:
                    dx = dx + dres_ref[...]
                dx_ref[...] = dx
                dg_ref[...] += jnp.sum(dgp, axis=0, keepdims=True)
                off += kk

    in_specs = [pl.BlockSpec((tm, tk), lambda i, k: (i, k)), pl.BlockSpec((n_out, tk), lambda i, k: (0, k))]
    in_specs += [pl.BlockSpec((tm, x.shape[1]), lambda i, k: (i, 0)) for x in xs]
    in_specs += [pl.BlockSpec((1, g.shape[1]), lambda i, k: (0, 0)) for g in gs]
    args = [da, w] + list(xs) + list(gs)
    if dres is not None:
        in_specs.append(pl.BlockSpec((tm, n_out), lambda i, k: (i, 0)))
        args.append(dres)
    if mode == "plain":
        out_shape = [jax.ShapeDtypeStruct((s_len, n_out), F32)]
        out_specs = [pl.BlockSpec((tm, n_out), lambda i, k: (i, 0))]
    else:
        out_shape = [jax.ShapeDtypeStruct((s_len, x.shape[1]), F32) for x in xs]
        out_specs = [pl.BlockSpec((tm, x.shape[1]), lambda i, k: (i, 0)) for x in xs]
        out_shape += [jax.ShapeDtypeStruct((1, x.shape[1]), F32) for x in xs]
        out_specs += [pl.BlockSpec((1, x.shape[1]), lambda i, k: (0, 0)) for x in xs]
    res, lands = _hosted_call(body, ride, name=name, grid=(s_len // tm, nk), in_specs=in_specs, out_specs=out_specs,
                              out_shape=out_shape, sem=("arbitrary", "arbitrary"), args=args,
                              scratch_shapes=[] if mode == "plain" else [pltpu.VMEM((tm, n_out), F32)])
    res = res[0] if mode == "plain" else res
    return res if ride is None else (res, lands)


def _mm_tn(name, a, b, tk_cap=1024, tn_cap=2048, tm=1024, ride=None):
    s_len, k_dim = a.shape
    n_dim = b.shape[1]
    tk = _tile(k_dim, tk_cap)
    tn = _tile(n_dim, tn_cap)
    tm = _tile(s_len, tm, 8)
    nm = s_len // tm

    def body(a_ref, b_ref, o_ref, acc_ref):
        m = pl.program_id(2)

        @pl.when(m == 0)
        def _():
            acc_ref[...] = jnp.zeros_like(acc_ref)

        acc_ref[...] += lax.dot_general(a_ref[...].astype(BF16), b_ref[...].astype(BF16), TN_DIMS,
                                        preferred_element_type=F32)

        @pl.when(m == nm - 1)
        def _():
            o_ref[...] = acc_ref[...].astype(BF16)

    res, lands = _hosted_call(
        body, ride, name=name, grid=(k_dim // tk, n_dim // tn, nm),
        in_specs=[pl.BlockSpec((tm, tk), lambda i, j, m: (m, i)), pl.BlockSpec((tm, tn), lambda i, j, m: (m, j))],
        out_specs=[pl.BlockSpec((tk, tn), lambda i, j, m: (i, j))],
        out_shape=[jax.ShapeDtypeStruct((k_dim, n_dim), BF16)],
        scratch_shapes=[pltpu.VMEM((tk, tn), F32)], sem=("parallel", "parallel", "arbitrary"), args=[a, b])
    return res[0] if ride is None else (res[0], lands)


def _neg_softplus(z):
    e = jnp.exp(-jnp.abs(z))
    return -(jnp.maximum(z, 0.0) + jnp.log(1.0 + e)), e


def _prep_fwd(name, proj, qg, kg, bf, n_heads, tm=256):
    s_len, n_p = proj.shape
    w_dim = n_heads * HEAD_DIM
    tm = _tile(s_len, tm, 8)

    def body(p_ref, qg_ref, kg_ref, bf_ref, pa_ref, c_ref, carry_ref):
        @pl.when(pl.program_id(0) == 0)
        def _():
            carry_ref[...] = jnp.zeros_like(carry_ref)

        for base, g_ref, mul in ((0, qg_ref, QK_SCALE), (w_dim, kg_ref, None)):
            for hh in range(n_heads):
                sl = slice(base + hh * HEAD_DIM, base + (hh + 1) * HEAD_DIM)
                xv = p_ref[:, sl]
                r = lax.rsqrt(jnp.mean(xv * xv, axis=1, keepdims=True) + EPS)
                y = xv * r * g_ref[...]
                pa_ref[:, sl] = (y if mul is None else y * mul).astype(BF16)
        pa_ref[:, 2 * w_dim:3 * w_dim] = p_ref[:, 2 * w_dim:3 * w_dim].astype(BF16)
        pa_ref[:, 3 * w_dim:4 * w_dim] = (p_ref[:, 3 * w_dim:4 * w_dim] * QK_SCALE).astype(BF16)
        pa_ref[:, 4 * w_dim:] = p_ref[:, 4 * w_dim:6 * w_dim].astype(BF16)
        f = p_ref[:, 6 * w_dim:] + bf_ref[...]
        lf, _ = _neg_softplus(-f)
        lane = lax.broadcasted_iota(jnp.int32, lf.shape, 1)
        lf = jnp.where(lane < n_heads, lf, 0.0)
        cb = _split_dot(lf, _tri(tm, "ge"), 3, x_left=False) + carry_ref[...]
        c_ref[...] = cb
        carry_ref[...] = cb[tm - 1:tm, :]

    return pl.pallas_call(
        body, name=name, grid=(s_len // tm,),
        in_specs=[pl.BlockSpec((tm, n_p), lambda i: (i, 0))] + [pl.BlockSpec((1, LANES), lambda i: (0, 0))] * 3,
        out_specs=[pl.BlockSpec((tm, 6 * w_dim), lambda i: (i, 0)), pl.BlockSpec((tm, LANES), lambda i: (i, 0))],
        out_shape=[jax.ShapeDtypeStruct((s_len, 6 * w_dim), BF16), jax.ShapeDtypeStruct((s_len, LANES), F32)],
        scratch_shapes=[pltpu.VMEM((1, LANES), F32)],
        compiler_params=_params(("arbitrary",)))(proj, qg, kg, bf)


def _prep_bwd(name, proj, qg, kg, bf, d_fox, d_sb, dc, n_heads, tm=256):
    s_len, n_p = proj.shape
    w_dim = n_heads * HEAD_DIM
    tm = _tile(s_len, tm, 8)
    nb = s_len // tm

    def body(p_ref, qg_ref, kg_ref, bf_ref, dqa_ref, dka_ref, dva_ref, dqb_ref, dkb_ref, dvb_ref, dc_ref,
             dp_ref, dqg_ref, dkg_ref, dbf_ref, carry_ref):
        @pl.when(pl.program_id(0) == 0)
        def _():
            for ref in (carry_ref, dqg_ref, dkg_ref, dbf_ref):
                ref[...] = jnp.zeros_like(ref)

        for base, g_ref, d_ref, dg_ref in ((0, qg_ref, dqa_ref, dqg_ref), (w_dim, kg_ref, dka_ref, dkg_ref)):
            dg = jnp.zeros((1, HEAD_DIM), F32)
            for hh in range(n_heads):
                sl = slice(base + hh * HEAD_DIM, base + (hh + 1) * HEAD_DIM)
                dx, dgp = _rms_bwd(d_ref[:, hh * HEAD_DIM:(hh + 1) * HEAD_DIM], p_ref[:, sl], g_ref[...])
                dp_ref[:, sl] = dx.astype(BF16)
                dg = dg + jnp.sum(dgp, axis=0, keepdims=True)
            dg_ref[...] += dg
        for n, d_ref in enumerate((dva_ref, dqb_ref, dkb_ref, dvb_ref)):
            dp_ref[:, (2 + n) * w_dim:(3 + n) * w_dim] = d_ref[...].astype(BF16)
        dlf = _split_dot(dc_ref[...], _tri(tm, "le"), 3, x_left=False) + carry_ref[...]
        carry_ref[...] = dlf[0:1, :]
        f = p_ref[:, 6 * w_dim:] + bf_ref[...]
        e = jnp.exp(-jnp.abs(f))
        sig_neg = jnp.where(f >= 0, e, 1.0) / (1.0 + e)
        lane = lax.broadcasted_iota(jnp.int32, f.shape, 1)
        df = jnp.where(lane < n_heads, dlf * sig_neg, 0.0)
        dp_ref[:, 6 * w_dim:] = df.astype(BF16)
        dbf_ref[...] += jnp.sum(df, axis=0, keepdims=True)

    rev = lambda i: (nb - 1 - i, 0)
    vec = pl.BlockSpec((1, LANES), lambda i: (0, 0))
    return pl.pallas_call(
        body, name=name, grid=(nb,),
        in_specs=[pl.BlockSpec((tm, n_p), rev), vec, vec, vec] + [pl.BlockSpec((tm, w_dim), rev)] * 6
        + [pl.BlockSpec((tm, LANES), rev)],
        out_specs=[pl.BlockSpec((tm, n_p), rev), vec, vec, vec],
        out_shape=[jax.ShapeDtypeStruct((s_len, n_p), BF16)] + [jax.ShapeDtypeStruct((1, LANES), F32)] * 3,
        scratch_shapes=[pltpu.VMEM((1, LANES), F32)],
        compiler_params=_params(("arbitrary",)))(proj, qg, kg, bf, *d_fox, *d_sb, dc)


def _head_col(c_blk, h):
    lane = lax.broadcasted_iota(jnp.int32, c_blk.shape, 1)
    return jnp.sum(jnp.where(lane == h, c_blk, 0.0), axis=1, keepdims=True)


def _fox_fwd(name, pa, c, ct, n_heads, tq=1024):
    s_len = pa.shape[0]
    tq = _tile(s_len, tq, LANES)
    hp = ct.shape[0]

    def body(q_ref, k_ref, v_ref, c_ref, ct_ref, o_ref, lse_ref):
        h, qi = pl.program_id(0), pl.program_id(1)
        q = q_ref[...]
        cq = _head_col(c_ref[...], h)
        row = lax.broadcasted_iota(jnp.int32, (tq, tq), 0)
        col = lax.broadcasted_iota(jnp.int32, (tq, tq), 1)

        def step(kb, carry, masked):
            m, l, acc = carry
            ks = pl.multiple_of(kb * tq, tq)
            k = k_ref[pl.ds(ks, tq), :]
            v = v_ref[pl.ds(ks, tq), :]
            ck = ct_ref[pl.ds(h, 1), pl.ds(ks, tq)]
            s = lax.dot_general(q, k, NT_DIMS, preferred_element_type=F32) + (cq - ck)
            if masked:
                s = jnp.where(col <= row, s, -jnp.inf)
            m_new = jnp.maximum(m, jnp.max(s, axis=1, keepdims=True))
            alpha = jnp.exp(m - m_new)
            p = jnp.exp(s - m_new)
            l = alpha * l + jnp.sum(p, axis=1, keepdims=True)
            acc = alpha * acc + jnp.dot(p.astype(BF16), v, preferred_element_type=F32)
            return m_new, l, acc

        init = (jnp.full((tq, 1), -jnp.inf, F32), jnp.zeros((tq, 1), F32), jnp.zeros((tq, HEAD_DIM), F32))
        carry = lax.fori_loop(0, qi, lambda kb, cr: step(kb, cr, False), init)
        m, l, acc = step(qi, carry, True)
        o_ref[...] = acc / l
        lse_ref[0] = m + jnp.log(l)

    return pl.pallas_call(
        body, name=name, grid=(n_heads, s_len // tq),
        in_specs=[pl.BlockSpec((tq, HEAD_DIM), lambda h, i: (i, h)),
                  pl.BlockSpec((s_len, HEAD_DIM), lambda h, i: (0, n_heads + h)),
                  pl.BlockSpec((s_len, HEAD_DIM), lambda h, i: (0, 2 * n_heads + h)),
                  pl.BlockSpec((tq, LANES), lambda h, i: (i, 0)),
                  pl.BlockSpec((hp, s_len), lambda h, i: (0, 0))],
        out_specs=[pl.BlockSpec((tq, HEAD_DIM), lambda h, i: (i, h)),
                   pl.BlockSpec((1, tq, 1), lambda h, i: (h, i, 0))],
        out_shape=[jax.ShapeDtypeStruct((s_len, n_heads * HEAD_DIM), F32),
                   jax.ShapeDtypeStruct((n_heads, s_len, 1), F32)],
        compiler_params=_params(("parallel", "arbitrary")))(pa, pa, pa, c, ct)


def _fox_bwd(name, pa, c, ct, o, do, lse, n_heads, tq=1024):
    s_len = pa.shape[0]
    tq = _tile(s_len, tq, LANES)
    hp = ct.shape[0]
    w_dim = n_heads * HEAD_DIM

    def body(q_ref, k_ref, v_ref, c_ref, ct_ref, o_ref, do_ref, lse_ref, dq_ref, dk_ref, dv_ref, dct_ref, dcq_ref):
        h, qi = pl.program_id(0), pl.program_id(1)

        @pl.when(qi == 0)
        def _():
            dk_ref[...] = jnp.zeros_like(dk_ref)
            dv_ref[...] = jnp.zeros_like(dv_ref)

        @pl.when((qi == 0) & (h == 0))
        def _():
            dct_ref[...] = jnp.zeros_like(dct_ref)

        q = q_ref[...]
        do32 = do_ref[...]
        dob = do32.astype(BF16)
        dsum = jnp.sum(do32 * o_ref[...], axis=1, keepdims=True)
        lse_v = lse_ref[0]
        cq = _head_col(c_ref[...], h)
        row = lax.broadcasted_iota(jnp.int32, (tq, tq), 0)
        col = lax.broadcasted_iota(jnp.int32, (tq, tq), 1)

        def step(kb, carry, masked):
            dq, dcq = carry
            ks = pl.multiple_of(kb * tq, tq)
            k = k_ref[pl.ds(ks, tq), :]
            v = v_ref[pl.ds(ks, tq), :]
            ck = ct_ref[pl.ds(h, 1), pl.ds(ks, tq)]
            s = lax.dot_general(q, k, NT_DIMS, preferred_element_type=F32) + (cq - ck)
            p = jnp.exp(s - lse_v)
            if masked:
                p = jnp.where(col <= row, p, 0.0)
            dp = lax.dot_general(dob, v, NT_DIMS, preferred_element_type=F32)
            ds = p * (dp - dsum)
            dsb = ds.astype(BF16)
            dk_ref[pl.ds(ks, tq), :] += lax.dot_general(dsb, q, TN_DIMS, preferred_element_type=F32)
            dv_ref[pl.ds(ks, tq), :] += lax.dot_general(p.astype(BF16), dob, TN_DIMS, preferred_element_type=F32)
            dct_ref[pl.ds(h, 1), pl.ds(ks, tq)] -= jnp.sum(ds, axis=0, keepdims=True)
            return dq + jnp.dot(dsb, k, preferred_element_type=F32), dcq + jnp.sum(ds, axis=1, keepdims=True)

        init = (jnp.zeros((tq, HEAD_DIM), F32), jnp.zeros((tq, 1), F32))
        carry = lax.fori_loop(0, qi, lambda kb, cr: step(kb, cr, False), init)
        dq, dcq = step(qi, carry, True)
        dq_ref[...] = dq * QK_SCALE
        dcq_ref[0] = dcq

    blk = pl.BlockSpec((tq, HEAD_DIM), lambda h, i: (i, h))
    full = pl.BlockSpec((s_len, HEAD_DIM), lambda h, i: (0, h))
    return pl.pallas_call(
        body, name=name, grid=(n_heads, s_len // tq),
        in_specs=[blk,
                  pl.BlockSpec((s_len, HEAD_DIM), lambda h, i: (0, n_heads + h)),
                  pl.BlockSpec((s_len, HEAD_DIM), lambda h, i: (0, 2 * n_heads + h)),
                  pl.BlockSpec((tq, LANES), lambda h, i: (i, 0)),
                  pl.BlockSpec((hp, s_len), lambda h, i: (0, 0)),
                  blk, blk,
                  pl.BlockSpec((1, tq, 1), lambda h, i: (h, i, 0))],
        out_specs=[blk, full, full, pl.BlockSpec((hp, s_len), lambda h, i: (0, 0)),
                   pl.BlockSpec((1, tq, 1), lambda h, i: (h, i, 0))],
        out_shape=[jax.ShapeDtypeStruct((s_len, w_dim), F32)] * 3 + [jax.ShapeDtypeStruct((hp, s_len), F32),
                                                                     jax.ShapeDtypeStruct((n_heads, s_len, 1), F32)],
        compiler_params=_params(("arbitrary", "arbitrary")))(pa, pa, pa, c, ct, o, do, lse)


def _sb_fwd(name, pa, n_heads, tq=512, tk=256):
    s_len = pa.shape[0]
    tq = _tile(s_len, tq, LANES)
    tk = _tile(tq, tk, LANES)
    nsub = tq // tk

    def body(q_ref, k_ref, v_ref, o_ref, tot_ref, kst_ref, kmax_ref):
        qi = pl.program_id(1)

        @pl.when(qi == 0)
        def _():
            kf = k_ref[...].astype(F32)
            kmax_ref[...] = jnp.sqrt(jnp.max(jnp.sum(kf * kf, axis=1, keepdims=True), axis=0, keepdims=True))

        q = q_ref[...]
        qf = q.astype(F32)
        z_bound = jnp.sqrt(jnp.sum(qf * qf, axis=1, keepdims=True)) * kmax_ref[...] * SB_BOUND_SLACK
        u = _tri(tk, "ge")
        row = lax.broadcasted_iota(jnp.int32, (tq, tk), 0)
        col = lax.broadcasted_iota(jnp.int32, (tq, tk), 1)

        def block(ks, carry, mask_off):
            r, acc = carry
            k = k_ref[pl.ds(ks, tk), :]
            v = v_ref[pl.ds(ks, tk), :]
            z = lax.dot_general(q, k, NT_DIMS, preferred_element_type=F32)
            a, _ = _neg_softplus(z)
            if mask_off is not None:
                valid = col + mask_off < row
                a = jnp.where(valid, a, 0.0)
            rin = _split_dot(a, u, 2)
            w = jnp.exp(z + (r + rin))
            if mask_off is not None:
                w = jnp.where(valid, w, 0.0)
            acc = acc + jnp.dot(w.astype(BF16), v, preferred_element_type=F32)
            return r + rin[:, 0:1], acc

        carry = (jnp.zeros((tq, 1), F32), jnp.zeros((tq, HEAD_DIM), F32))
        q0 = pl.multiple_of(qi * tq, tq)
        for j in reversed(range(nsub)):
            carry = block(q0 + j * tk, carry, j * tk)
        def walk(state):
            kb, _, r, acc = state
            r, acc = block(pl.multiple_of(kb * tk, tk), (r, acc), None)
            return kb - 1, jnp.max(r + z_bound), r, acc

        state = (qi * nsub - 1, jnp.max(carry[0] + z_bound)) + carry
        kb, _, r, acc = lax.while_loop(lambda st: (st[0] >= 0) & (st[1] > -SB_SKIP), walk, state)
        o_ref[...] = acc
        tot_ref[0] = r
        kst_ref[...] = jnp.full(kst_ref.shape, (kb + 1).astype(F32))

    nq = s_len // tq
    return pl.pallas_call(
        body, name=name, grid=(n_heads, nq),
        in_specs=[pl.BlockSpec((tq, HEAD_DIM), lambda h, i: (i, 3 * n_heads + h)),
                  pl.BlockSpec((s_len, HEAD_DIM), lambda h, i: (0, 4 * n_heads + h)),
                  pl.BlockSpec((s_len, HEAD_DIM), lambda h, i: (0, 5 * n_heads + h))],
        out_specs=[pl.BlockSpec((tq, HEAD_DIM), lambda h, i: (i, h)),
                   pl.BlockSpec((1, tq, 1), lambda h, i: (h, i, 0)),
                   pl.BlockSpec((1, 1, 8, LANES), lambda h, i: (h, i, 0, 0))],
        out_shape=[jax.ShapeDtypeStruct((s_len, n_heads * HEAD_DIM), F32),
                   jax.ShapeDtypeStruct((n_heads, s_len, 1), F32),
                   jax.ShapeDtypeStruct((n_heads, nq, 8, LANES), F32)],
        scratch_shapes=[pltpu.VMEM((1, 1), F32)],
        compiler_params=_params(("parallel", "arbitrary")))(pa, pa, pa)


def _sb_bwd(name, pa, do, tot, kst, n_heads, tq=512, tk=256):
    s_len = pa.shape[0]
    tq = _tile(s_len, tq, LANES)
    tk = _tile(tq, tk, LANES)
    nsub = tq // tk
    w_dim = n_heads * HEAD_DIM

    def body(q_ref, k_ref, v_ref, do_ref, tot_ref, kst_ref, dq_ref, dk_ref, dv_ref):
        qi = pl.program_id(1)

        @pl.when(qi == 0)
        def _():
            dk_ref[...] = jnp.zeros_like(dk_ref)
            dv_ref[...] = jnp.zeros_like(dv_ref)

        q = q_ref[...]
        dob = do_ref[...].astype(BF16)
        u = _tri(tk, "le")
        row = lax.broadcasted_iota(jnp.int32, (tq, tk), 0)
        col = lax.broadcasted_iota(jnp.int32, (tq, tk), 1)

        def block(ks, carry, mask_off):
            rem, cpre, dq = carry
            k = k_ref[pl.ds(ks, tk), :]
            v = v_ref[pl.ds(ks, tk), :]
            z = lax.dot_general(q, k, NT_DIMS, preferred_element_type=F32)
            a, e = _neg_softplus(z)
            if mask_off is not None:
                valid = col + mask_off < row
                a = jnp.where(valid, a, 0.0)
            pin = _split_dot(a, u, 2)
            w = jnp.exp(z + (rem - (pin - a)))
            if mask_off is not None:
                w = jnp.where(valid, w, 0.0)
            g = w * lax.dot_general(dob, v, NT_DIMS, preferred_element_type=F32)
            cin = _split_dot(g, u, 2)
            beta = jnp.where(z >= 0, 1.0, e) / (1.0 + e)
            dz = g - beta * (cpre + cin)
            if mask_off is not None:
                dz = jnp.where(valid, dz, 0.0)
            dzb = dz.astype(BF16)
            dk_ref[pl.ds(ks, tk), :] += lax.dot_general(dzb, q, TN_DIMS, preferred_element_type=F32)
            dv_ref[pl.ds(ks, tk), :] += lax.dot_general(w.astype(BF16), dob, TN_DIMS, preferred_element_type=F32)
            dq = dq + jnp.dot(dzb, k, preferred_element_type=F32)
            return rem - pin[:, tk - 1:tk], cpre + cin[:, tk - 1:tk], dq

        carry = (tot_ref[0], jnp.zeros((tq, 1), F32), jnp.zeros((tq, HEAD_DIM), F32))
        first = jnp.max(kst_ref[0, 0]).astype(jnp.int32)
        carry = lax.fori_loop(first, qi * nsub, lambda n, cr: block(pl.multiple_of(n * tk, tk), cr, None), carry)
        q0 = pl.multiple_of(qi * tq, tq)
        for j in range(nsub):
            carry = block(q0 + j * tk, carry, j * tk)
        dq_ref[...] = carry[2] * QK_SCALE

    blk = pl.BlockSpec((tq, HEAD_DIM), lambda h, i: (i, h))
    full = pl.BlockSpec((s_len, HEAD_DIM), lambda h, i: (0, h))
    return pl.pallas_call(
        body, name=name, grid=(n_heads, s_len // tq),
        in_specs=[pl.BlockSpec((tq, HEAD_DIM), lambda h, i: (i, 3 * n_heads + h)),
                  pl.BlockSpec((s_len, HEAD_DIM), lambda h, i: (0, 4 * n_heads + h)),
                  pl.BlockSpec((s_len, HEAD_DIM), lambda h, i: (0, 5 * n_heads + h)),
                  blk,
                  pl.BlockSpec((1, tq, 1), lambda h, i: (h, i, 0)),
                  pl.BlockSpec((1, 1, 8, LANES), lambda h, i: (h, i, 0, 0))],
        out_specs=[blk, full, full],
        out_shape=[jax.ShapeDtypeStruct((s_len, w_dim), F32)] * 3,
        compiler_params=_params(("arbitrary", "arbitrary")))(pa, pa, pa, do, tot, kst)


def _conv_rows(ext_ref, u_ref, halo_ref, first, tm):
    ext_ref[0:8, :] = jnp.where(first, 0.0, halo_ref[...])
    ext_ref[8:, :] = u_ref[...]
    return ext_ref[8:8 + tm, :], ext_ref[7:7 + tm, :], ext_ref[6:6 + tm, :]


def _sigmoid(x):
    return 1.0 / (1.0 + jnp.exp(-x))


def _conv_specs(tm, tc, nj, order):
    hb = tm // 8
    ij = order

    def at(f):
        return lambda *g: f(*ij(*g))

    return [pl.BlockSpec((tm, tc), at(lambda i, j: (i, j))),
            pl.BlockSpec((tm, tc), at(lambda i, j: (i, j + nj))),
            pl.BlockSpec((8, tc), at(lambda i, j: (jnp.maximum(i * hb - 1, 0), j))),
            pl.BlockSpec((8, tc), at(lambda i, j: (jnp.maximum(i * hb - 1, 0), j + nj))),
            pl.BlockSpec((3, tc), at(lambda i, j: (0, j))),
            pl.BlockSpec((3, tc), at(lambda i, j: (0, j + nj))),
            pl.BlockSpec((1, tc), at(lambda i, j: (0, j))),
            pl.BlockSpec((1, tc), at(lambda i, j: (0, j + nj)))]


def _conv_gate_fwd(name, u, cw, cb, tm=512, tc=512):
    s_len, f2 = u.shape
    f_dim = f2 // 2
    tm = _tile(s_len, tm, 8)
    tc = _tile(f_dim, tc)
    nj = f_dim // tc

    def body(ug_ref, uv_ref, hg_ref, hv_ref, wg_ref, wv_ref, bg_ref, bv_ref, g_ref, ext_ref):
        first = pl.program_id(0) == 0

        def conv(u_ref, h_ref, w_ref, b_ref):
            u0, u1, u2 = _conv_rows(ext_ref, u_ref, h_ref, first, tm)
            return w_ref[2:3, :] * u0 + w_ref[1:2, :] * u1 + w_ref[0:1, :] * u2 + b_ref[...]

        gc = conv(ug_ref, hg_ref, wg_ref, bg_ref)
        vc = conv(uv_ref, hv_ref, wv_ref, bv_ref)
        g_ref[...] = (gc * _sigmoid(gc) * vc).astype(BF16)

    return pl.pallas_call(
        body, name=name, grid=(s_len // tm, nj),
        in_specs=_conv_specs(tm, tc, nj, lambda i, j: (i, j)),
        out_specs=pl.BlockSpec((tm, tc), lambda i, j: (i, j)),
        out_shape=jax.ShapeDtypeStruct((s_len, f_dim), BF16),
        scratch_shapes=[pltpu.VMEM((tm + 8, tc), F32)],
        compiler_params=_params(("parallel", "parallel")))(u, u, u, u, cw, cw, cb, cb)


def _conv_gate_bwd(name, u, dg, cw, cb, tm=512, tc=512):
    s_len, f2 = u.shape
    f_dim = f2 // 2
    tm = _tile(s_len, tm, 8)
    tc = _tile(f_dim, tc)
    nj = f_dim // tc

    def body(ug_ref, uv_ref, hg_ref, hv_ref, wg_ref, wv_ref, bg_ref, bv_ref, dg_ref, duc_ref, dcw_ref, dcb_ref,
             eg_ref, ev_ref):
        first = pl.program_id(1) == 0

        @pl.when(first)
        def _():
            dcw_ref[...] = jnp.zeros_like(dcw_ref)
            dcb_ref[...] = jnp.zeros_like(dcb_ref)

        ug = _conv_rows(eg_ref, ug_ref, hg_ref, first, tm)
        uv = _conv_rows(ev_ref, uv_ref, hv_ref, first, tm)
        gc = wg_ref[2:3, :] * ug[0] + wg_ref[1:2, :] * ug[1] + wg_ref[0:1, :] * ug[2] + bg_ref[...]
        vc = wv_ref[2:3, :] * uv[0] + wv_ref[1:2, :] * uv[1] + wv_ref[0:1, :] * uv[2] + bv_ref[...]
        sg = _sigmoid(gc)
        dgv = dg_ref[...]
        dvc = dgv * (gc * sg)
        dgc = dgv * vc * (sg * (1.0 + gc * (1.0 - sg)))
        duc_ref[0] = dgc
        duc_ref[1] = dvc
        for half, (d, us) in enumerate(((dgc, ug), (dvc, uv))):
            dcb_ref[half] += jnp.sum(d, axis=0, keepdims=True)
            for tap in range(3):
                dcw_ref[half, tap:tap + 1, :] += jnp.sum(d * us[2 - tap], axis=0, keepdims=True)

    order = lambda j, i: (i, j)
    return pl.pallas_call(
        body, name=name, grid=(nj, s_len // tm),
        in_specs=_conv_specs(tm, tc, nj, order) + [pl.BlockSpec((tm, tc), lambda j, i: (i, j))],
        out_specs=[pl.BlockSpec((2, tm, tc), lambda j, i: (0, i, j)),
                   pl.BlockSpec((2, 3, tc), lambda j, i: (0, 0, j)),
                   pl.BlockSpec((2, 1, tc), lambda j, i: (0, 0, j))],
        out_shape=[jax.ShapeDtypeStruct((2, s_len, f_dim), F32), jax.ShapeDtypeStruct((2, 3, f_dim), F32),
                   jax.ShapeDtypeStruct((2, 1, f_dim), F32)],
        scratch_shapes=[pltpu.VMEM((tm + 8, tc), F32), pltpu.VMEM((tm + 8, tc), F32)],
        compiler_params=_params(("parallel", "arbitrary")))(u, u, u, u, cw, cw, cb, cb, dg)


def _conv_t(name, duc, cw, tm=512, tc=512):
    _, s_len, f_dim = duc.shape
    tm = _tile(s_len, tm, 8)
    tc = _tile(f_dim, tc)
    nj = f_dim // tc
    nb = s_len // tm
    hb = tm // 8

    def body(d_ref, halo_ref, w_ref, o_ref, ext_ref):
        last = pl.program_id(1) == nb - 1
        ext_ref[0:tm, :] = d_ref[0]
        ext_ref[tm:, :] = jnp.where(last, 0.0, halo_ref[0])
        o_ref[...] = (w_ref[2:3, :] * ext_ref[0:tm, :] + w_ref[1:2, :] * ext_ref[1:1 + tm, :]
                      + w_ref[0:1, :] * ext_ref[2:2 + tm, :]).astype(BF16)

    return pl.pallas_call(
        body, name=name, grid=(2, nb, nj),
        in_specs=[pl.BlockSpec((1, tm, tc), lambda p, i, j: (p, i, j)),
                  pl.BlockSpec((1, 8, tc), lambda p, i, j: (p, jnp.minimum((i + 1) * hb, nb * hb - 1), j)),
                  pl.BlockSpec((3, tc), lambda p, i, j: (0, p * nj + j))],
        out_specs=pl.BlockSpec((tm, tc), lambda p, i, j: (i, p * nj + j)),
        out_shape=jax.ShapeDtypeStruct((s_len, 2 * f_dim), BF16),
        scratch_shapes=[pltpu.VMEM((tm + 8, tc), F32)],
        compiler_params=_params(("parallel", "parallel", "parallel")))(duc, duc, cw)


def _loss_head(name, y, tgt, tm=512):
    s_len, d = y.shape
    tm = _tile(s_len, tm, 8)

    def body(y_ref, t_ref, dy_ref, l_ref):
        @pl.when(pl.program_id(0) == 0)
        def _():
            l_ref[...] = jnp.zeros_like(l_ref)

        err = y_ref[...] - t_ref[...]
        dy_ref[...] = err * (1.0 / d)
        l_ref[...] += 0.5 * jnp.sum(jnp.sum(err * err, axis=1, keepdims=True) * (1.0 / d), axis=0, keepdims=True)

    blk = pl.BlockSpec((tm, d), lambda i: (i, 0))
    return pl.pallas_call(
        body, name=name, grid=(s_len // tm,), in_specs=[blk, blk],
        out_specs=[blk, pl.BlockSpec((1, LANES), lambda i: (0, 0))],
        out_shape=[jax.ShapeDtypeStruct((s_len, d), F32), jax.ShapeDtypeStruct((1, LANES), F32)],
        compiler_params=_params(("arbitrary",)))(y, tgt)


def _adamw(name, parts_list, w, m, v, tr=256):
    n_l = len(parts_list)
    n_parts, rows, cols = parts_list[0].shape
    row_bytes = -(-cols // LANES) * LANES * (2 * n_l * n_parts * parts_list[0].dtype.itemsize + 2 * 7 * 4)
    tr = _tile(rows, min(tr, max(16, ADAMW_VMEM // row_bytes // 16 * 16)), 16)
    nb = rows // tr
    c1 = 1.0 - ADAM_B1 ** ADAM_STEP
    c2 = 1.0 - ADAM_B2 ** ADAM_STEP

    def body(*refs):
        p_refs = refs[:n_l]
        w_ref, m_ref, v_ref, g_ref, d_ref, nm_ref, nv_ref = refs[n_l:]
        for l, p_ref in enumerate(p_refs):
            @pl.when(pl.program_id(0) == l)
            def _(p_ref=p_ref):
                g = p_ref[0].astype(F32)
                for n in range(1, n_parts):
                    g = g + p_ref[n].astype(F32)
                nm = ADAM_B1 * m_ref[...] + (1.0 - ADAM_B1) * g
                nv = ADAM_B2 * v_ref[...] + (1.0 - ADAM_B2) * (g * g)
                g_ref[...] = g
                nm_ref[...] = nm
                nv_ref[...] = nv
                d_ref[...] = -ADAM_LR * ((nm / c1) / (jnp.sqrt(nv / c2) + ADAM_EPS) + ADAM_WD * w_ref[...])

    p_specs = [pl.BlockSpec((n_parts, tr, cols), lambda li, i, l=l: (0, jnp.where(li == l, i, 0), 0))
               for l in range(n_l)]
    blk = pl.BlockSpec((tr, cols), lambda li, i: (li * nb + i, 0))
    return pl.pallas_call(
        body, name=name, grid=(n_l, nb), in_specs=p_specs + [blk, blk, blk],
        out_specs=[blk] * 4, out_shape=[jax.ShapeDtypeStruct((n_l * rows, cols), F32)] * 4,
        compiler_params=_params(("arbitrary", "arbitrary")))(*parts_list, w, m, v)


def _peers():
    x, y, c = lax.axis_index("x"), lax.axis_index("y"), lax.axis_index("c")
    out = []
    for k in range(1, N_DEV):
        fx, fy, fc = (k >> 2) & 1, (k >> 1) & 1, k & 1
        px, py, pc = x ^ fx, y ^ fy, c ^ fc
        out.append((k - 1, (px, py, pc), 4 * px + 2 * py + pc))
    return 4 * x + 2 * y + c, out


def _block_of(ref, axis, index, size):
    if axis is None:
        return ref.at[index]
    idx = [slice(None)] * len(ref.shape)
    idx[axis] = pl.ds(pl.multiple_of(index * size, size), size)
    return ref.at[tuple(idx)]


def _land_shape(shape, axis, scatter):
    shape = list(shape)
    if scatter:
        if axis is None:
            return tuple(shape)
        shape[axis] //= N_DEV
        return (N_DEV, *shape)
    if axis is None:
        return (N_DEV, *shape)
    shape[axis] *= N_DEV
    return tuple(shape)


def _copy_ends(axis, scatter, src, land, me, idx):
    if scatter:
        size = None if axis is None else src.shape[axis] // N_DEV
        return _block_of(src, axis, idx, size), land.at[me]
    return src, _block_of(land, axis, me, None if axis is None else src.shape[axis])


def _remote_copies(axes, scatter, in_refs, land_refs, send_sems, recv_sems):
    me, peers = _peers()
    out = []
    for a, (axis, src, land) in enumerate(zip(axes, in_refs, land_refs)):
        for k, pos, idx in peers:
            s, d = _copy_ends(axis, scatter, src, land, me, idx)
            out.append(pltpu.make_async_remote_copy(
                src_ref=s, dst_ref=d, send_sem=send_sems.at[a * (N_DEV - 1) + k],
                recv_sem=recv_sems.at[a * (N_DEV - 1) + k],
                device_id=pos, device_id_type=MESH))
    return out


def _exchange(name, arrays, axes, scatter):
    n = len(arrays)

    def body(*refs):
        copies = _all_copies(axes, scatter, refs[:n], refs[n:2 * n], *refs[2 * n:])
        for cp in copies:
            cp.start()
        for cp in copies:
            cp.wait()

    any_spec = pl.BlockSpec(memory_space=pl.ANY)
    return pl.pallas_call(
        body, name=name, in_specs=[any_spec] * n, out_specs=[any_spec] * n,
        out_shape=[jax.ShapeDtypeStruct(_land_shape(a.shape, ax, scatter), a.dtype) for a, ax in zip(arrays, axes)],
        scratch_shapes=_exchange_sems(n),
        compiler_params=pltpu.CompilerParams(has_side_effects=True))(*arrays)


def _all_copies(axes, scatter, in_refs, land_refs, send_sems, recv_sems, local_sems):
    me, _ = _peers()
    copies = _remote_copies(axes, scatter, in_refs, land_refs, send_sems, recv_sems)
    for a, (axis, src, land) in enumerate(zip(axes, in_refs, land_refs)):
        s, d = _copy_ends(axis, scatter, src, land, me, me)
        copies.append(pltpu.make_async_copy(s, d, local_sems.at[a]))
    return copies


def _exchange_sems(n):
    return [pltpu.SemaphoreType.DMA((n * (N_DEV - 1),)), pltpu.SemaphoreType.DMA((n * (N_DEV - 1),)),
            pltpu.SemaphoreType.DMA((n,))]


def _hosted_call(body, ride, *, name, grid, in_specs, out_specs, out_shape, sem, args, scratch_shapes=()):
    if ride is None:
        return pl.pallas_call(body, name=name, grid=grid, in_specs=in_specs, out_specs=out_specs, out_shape=out_shape,
                              scratch_shapes=list(scratch_shapes), compiler_params=_params(sem))(*args), None
    arrays, axes, scatter = ride
    n, n_in, n_out, n_scr = len(arrays), len(in_specs), len(out_specs), len(scratch_shapes)

    def hosted(*refs):
        main_in, ride_in = refs[:n_in], refs[n_in:n_in + n]
        o0 = n_in + n
        main_out, lands = refs[o0:o0 + n_out], refs[o0 + n_out:o0 + n_out + n]
        s0 = o0 + n_out + n
        main_scr, sems = refs[s0:s0 + n_scr], refs[s0 + n_scr:]
        ids = [pl.program_id(i) for i in range(len(grid))]
        first = functools.reduce(jnp.logical_and, [i == 0 for i in ids])
        last = functools.reduce(jnp.logical_and, [i == g - 1 for i, g in zip(ids, grid)])

        @pl.when(first)
        def _():
            for cp in _all_copies(axes, scatter, ride_in, lands, *sems):
                cp.start()

        body(*main_in, *main_out, *main_scr)

        @pl.when(last)
        def _():
            for cp in _all_copies(axes, scatter, ride_in, lands, *sems):
                cp.wait()

    any_spec = pl.BlockSpec(memory_space=pl.ANY)
    res = pl.pallas_call(
        hosted, name=name, grid=grid, in_specs=list(in_specs) + [any_spec] * n,
        out_specs=list(out_specs) + [any_spec] * n,
        out_shape=list(out_shape) + [jax.ShapeDtypeStruct(_land_shape(a.shape, ax, scatter), a.dtype)
                                     for a, ax in zip(arrays, axes)],
        scratch_shapes=list(scratch_shapes) + _exchange_sems(n),
        compiler_params=pltpu.CompilerParams(dimension_semantics=("arbitrary",) * len(grid),
                                             vmem_limit_bytes=VMEM_LIMIT, has_side_effects=True))(*args, *arrays)
    return res[:n_out], res[n_out:]


def _row(vec, width=None):
    vec = vec.reshape(1, -1)
    if width is not None and vec.shape[1] < width:
        vec = jnp.pad(vec, ((0, 0), (0, width - vec.shape[1])))
    return vec


def _riding(rides, lands):
    def run(key, fn, *args, **kwargs):
        if rides is None or key not in rides:
            return fn(*args, **kwargs)
        res, lands[key] = fn(*args, ride=rides[key], **kwargs)
        return res
    return run


def _layer_fwd(l, x, sp, bp, rides=None):
    n_heads = x.shape[1] // (2 * HEAD_DIM)
    lands = {}
    run = _riding(rides, lands)
    proj, h = run("in_proj", _mm_nn, f"l{l}_in_proj", [x], [_row(sp["attn_norm"])], bp["w_in"])
    qg, kg, bf = _row(sp["q_norm"]), _row(sp["k_norm"]), _row(sp["b_forget"], LANES)
    pa, c = _prep_fwd(f"l{l}_prep", proj, qg, kg, bf, n_heads)
    ct = c[:, :n_heads].T
    o_a, lse = _fox_fwd(f"l{l}_fox", pa, c, ct, n_heads)
    o_b, tot, kst = _sb_fwd(f"l{l}_sb", pa, n_heads)
    gfox, gsb = _row(sp["out_norm_fox"]), _row(sp["out_norm_sb"])
    x1, merged = run("out_proj", _mm_nn, f"l{l}_out_proj", [o_a, o_b], [gfox, gsb], bp["w_out"], resid=x)
    u, h2 = run("up_proj", _mm_nn, f"l{l}_up_proj", [x1], [_row(sp["ffn_norm"])], bp["w_up"])
    cb = _row(sp["conv_b"])
    g = _conv_gate_fwd(f"l{l}_conv_gate", u, bp["conv_w"], cb)
    x2 = run("down_proj", _mm_nn, f"l{l}_down_proj", [g], [], bp["w_down"], resid=x1, tn_cap=512)
    saved = dict(x=x, h=h, proj=proj, pa=pa, c=c, ct=ct, o_a=o_a, lse=lse, o_b=o_b, tot=tot, kst=kst, merged=merged, x1=x1,
                 h2=h2, u=u, g=g, qg=qg, kg=kg, bf=bf, gfox=gfox, gsb=gsb, cb=cb)
    return x2, saved, lands


def _layer_bwd(l, dx, sp, bp, sv, rides=None):
    n_heads = dx.shape[1] // (2 * HEAD_DIM)
    gr, lands = {}, {}
    run = _riding(rides, lands)
    dg = run("d_down_act", _mm_nt, f"l{l}_d_down_act", dx, bp["w_down"])
    gr["w_down"] = _mm_tn(f"l{l}_d_w_down", sv["g"], dx)
    duc, dcw, dcb = _conv_gate_bwd(f"l{l}_d_conv_gate", sv["u"], dg, bp["conv_w"], sv["cb"])
    gr["conv_w"] = dcw.transpose(1, 0, 2).reshape(3, -1)
    gr["conv_b"] = dcb.reshape(-1)
    du = _conv_t(f"l{l}_d_conv", duc, bp["conv_w"])
    gr["w_up"] = run("d_w_up", _mm_tn, f"l{l}_d_w_up", sv["h2"], du)
    dx1, dffn = run("d_up_act", _mm_nt, f"l{l}_d_up_act", du, bp["w_up"], "rms_bwd", [sv["x1"]],
                    [_row(sp["ffn_norm"])], dres=dx)
    gr["ffn_norm"] = dffn.reshape(-1)
    gr["w_out"] = _mm_tn(f"l{l}_d_w_out", sv["merged"], dx1)
    do_a, do_b, dgfox, dgsb = _mm_nt(f"l{l}_d_out_act", dx1, bp["w_out"], "rms2_bwd",
                                     [sv["o_a"], sv["o_b"]], [sv["gfox"], sv["gsb"]])
    gr["out_norm_fox"], gr["out_norm_sb"] = dgfox.reshape(-1), dgsb.reshape(-1)
    dq_a, dk_a, dv_a, dct, dcq = _fox_bwd(f"l{l}_d_fox", sv["pa"], sv["c"], sv["ct"], sv["o_a"], do_a, sv["lse"],
                                          n_heads)
    d_sb = _sb_bwd(f"l{l}_d_sb", sv["pa"], do_b, sv["tot"], sv["kst"], n_heads)
    dc = jnp.pad((dct + dcq[:, :, 0]).T, ((0, 0), (0, LANES - n_heads)))
    dproj, dqg, dkg, dbf = _prep_bwd(f"l{l}_d_prep", sv["proj"], sv["qg"], sv["kg"], sv["bf"],
                                     (dq_a, dk_a, dv_a), d_sb, dc, n_heads)
    gr["q_norm"], gr["k_norm"], gr["b_forget"] = dqg.reshape(-1), dkg.reshape(-1), dbf.reshape(-1)[:n_heads]
    gr["w_in"] = _mm_tn(f"l{l}_d_w_in", sv["h"], dproj)
    dx0, dattn = run("d_in_act", _mm_nt, f"l{l}_d_in_act", dproj, bp["w_in"], "rms_bwd", [sv["x"]],
                     [_row(sp["attn_norm"])], dres=dx1)
    gr["attn_norm"] = dattn.reshape(-1)
    return dx0, gr, lands


def _local_step(x, tgt, small, big):
    n_layers = len(big)
    saved = []
    for l in range(n_layers):
        x, sv, _ = _layer_fwd(l, x, small[l], big[l])
        saved.append(sv)
    dx, loss_part = _loss_head("loss_head", x, tgt)
    grads = [None] * n_layers
    for l in reversed(range(n_layers)):
        dx, grads[l], _ = _layer_bwd(l, dx, small[l], big[l], saved[l])
    return loss_part, dx, grads


def _w_in_to_internal(w, n_heads):
    w3 = 3 * n_heads * HEAD_DIM
    pad = jnp.zeros(w.shape[:-1] + (LANES - n_heads,), w.dtype)
    return jnp.concatenate([w[..., :w3], w[..., w3 + n_heads:], w[..., w3:w3 + n_heads], pad], axis=-1)


def _w_in_from_internal(w, n_heads):
    w3 = 3 * n_heads * HEAD_DIM
    return jnp.concatenate([w[..., :w3], w[..., 2 * w3:2 * w3 + n_heads], w[..., w3:2 * w3]], axis=-1)


SMALL = ("attn_norm", "b_forget", "q_norm", "k_norm", "out_norm_fox", "out_norm_sb", "ffn_norm", "conv_b")
BIG = ("w_in", "w_out", "w_up", "w_down")
WEIGHTS = ("attn_norm", "w_in", "b_forget", "q_norm", "k_norm", "out_norm_fox", "out_norm_sb", "w_out", "ffn_norm",
           "w_up", "conv_w", "conv_b", "w_down")


def kernel(x, attn_norm, w_in, b_forget, q_norm, k_norm, out_norm_fox, out_norm_sb, w_out, ffn_norm, w_up, conv_w, conv_b, w_down, loss_target, m_attn_norm, m_w_in, m_b_forget, m_q_norm, m_k_norm, m_out_norm_fox, m_out_norm_sb, m_w_out, m_ffn_norm, m_w_up, m_conv_w, m_conv_b, m_w_down, v_attn_norm, v_w_in, v_b_forget, v_q_norm, v_k_norm, v_out_norm_fox, v_out_norm_sb, v_w_out, v_ffn_norm, v_w_up, v_conv_w, v_conv_b, v_w_down):
    w = dict(attn_norm=attn_norm, w_in=w_in, b_forget=b_forget, q_norm=q_norm, k_norm=k_norm,
             out_norm_fox=out_norm_fox, out_norm_sb=out_norm_sb, w_out=w_out, ffn_norm=ffn_norm, w_up=w_up,
             conv_w=conv_w, conv_b=conv_b, w_down=w_down)
    mom = dict(attn_norm=m_attn_norm, w_in=m_w_in, b_forget=m_b_forget, q_norm=m_q_norm, k_norm=m_k_norm,
               out_norm_fox=m_out_norm_fox, out_norm_sb=m_out_norm_sb, w_out=m_w_out, ffn_norm=m_ffn_norm,
               w_up=m_w_up, conv_w=m_conv_w, conv_b=m_conv_b, w_down=m_w_down)
    var = dict(attn_norm=v_attn_norm, w_in=v_w_in, b_forget=v_b_forget, q_norm=v_q_norm, k_norm=v_k_norm,
               out_norm_fox=v_out_norm_fox, out_norm_sb=v_out_norm_sb, w_out=v_w_out, ffn_norm=v_ffn_norm,
               w_up=v_w_up, conv_w=v_conv_w, conv_b=v_conv_b, w_down=v_w_down)
    n_layers, d = attn_norm.shape
    n_heads = d // (2 * HEAD_DIM)
    me = 4 * lax.axis_index("x") + 2 * lax.axis_index("y") + lax.axis_index("c")

    shard = {k: w[k].astype(BF16) for k in BIG}
    axis_of = dict(w_in=None, w_out=0, w_up=1, w_down=0)
    host_fwd = dict(w_in="in_proj", w_out="out_proj", w_up="up_proj", w_down="down_proj")
    host_bwd = dict(w_in="d_up_act", w_out="d_down_act", w_up="d_w_up", w_down="d_in_act")
    whole_w_in = lambda g: _w_in_to_internal(g.transpose(1, 0, 2).reshape(d, -1), n_heads)
    g_in, g_out, g_up, g_down, full_cw = _exchange(
        "gather_l0", [shard[k][0] for k in BIG] + [w["conv_w"]], [axis_of[k] for k in BIG] + [2], scatter=False)
    small = [{k: w[k][l] for k in SMALL} for l in range(n_layers)]
    big = [dict(w_in=whole_w_in(g_in), w_out=g_out, w_up=g_up, w_down=g_down, conv_w=full_cw[0])]

    saved = [None] * n_layers
    act = x[0]
    for l in range(n_layers):
        rides = None
        if l + 1 < n_layers:
            rides = {host_fwd[k]: ([shard[k][l + 1]], [axis_of[k]], False) for k in BIG}
        act, saved[l], lands = _layer_fwd(l, act, small[l], big[l], rides)
        if l + 1 < n_layers:
            nxt = {k: lands[host_fwd[k]][0] for k in BIG}
            big.append(dict(nxt, w_in=whole_w_in(nxt["w_in"]), conv_w=full_cw[l + 1]))
    dx, loss_part = _loss_head("loss_head", act, loss_target[0])
    grads, parts, sends = [None] * n_layers, [None] * n_layers, None
    for l in reversed(range(n_layers)):
        rides = None
        if sends is not None:
            rides = {host_bwd[k]: ([sends[k]], [axis_of[k]], True) for k in BIG}
        dx, grads[l], lands = _layer_bwd(l, dx, small[l], big[l], saved[l], rides)
        if sends is not None:
            parts[l + 1] = [lands[host_bwd[k]][0] for k in BIG]
        g_in = _w_in_from_internal(grads[l]["w_in"], n_heads)
        sends = dict(grads[l], w_in=g_in.reshape(d, N_DEV, -1).transpose(1, 0, 2))
    grad_x = dx
    parts[0] = _exchange("scatter_l0", [sends[k] for k in BIG], [axis_of[k] for k in BIG], scatter=True)

    stack = lambda k: jnp.stack([grads[l][k] for l in range(n_layers)])
    small_names = SMALL + ("conv_w",)
    flat = jnp.concatenate([loss_part.reshape(-1)] + [
        jnp.pad(stack(k).reshape(-1), (0, (-stack(k).size) % LANES)) for k in small_names])
    flat = flat.reshape(-1, LANES)
    (all_small,) = _exchange("gather_small_grads", [flat], [None], scatter=False)

    out = {}

    def update(name, parts_list):
        rc = (-1, parts_list[0].shape[-1])
        res = _adamw("adamw_" + name, parts_list, w[name].reshape(rc), mom[name].reshape(rc), var[name].reshape(rc))
        out[name] = [r.reshape(w[name].shape) for r in res]

    for n, name in enumerate(BIG):
        update(name, [parts[l][n] for l in range(n_layers)])
    n_rows = flat.shape[0]
    w_flat, m_flat, v_flat = [], [], []
    for src, dst in ((w, w_flat), (mom, m_flat), (var, v_flat)):
        dst.append(jnp.zeros((LANES,), F32))
        for k in small_names:
            a = src[k]
            if k == "conv_w":
                a = jnp.zeros((n_layers, 3, conv_w.shape[2] * N_DEV), F32)
            dst.append(jnp.pad(a.reshape(-1), (0, (-a.size) % LANES)))
    pack = lambda parts: jnp.concatenate(parts).reshape(n_rows, LANES)
    res = _adamw("adamw_small", [all_small], pack(w_flat), pack(m_flat), pack(v_flat))
    res = [r.reshape(-1) for r in res]
    loss = res[0][0]
    off = LANES
    g_cw_full = None
    for k in small_names:
        size = n_layers * 3 * conv_w.shape[2] * N_DEV if k == "conv_w" else w[k].size
        if k == "conv_w":
            g_cw_full = res[0][off:off + size].reshape(n_layers, 3, -1)
        else:
            out[k] = [r[off:off + size].reshape(w[k].shape) for r in res]
        off += size + (-size) % LANES
    c_loc = conv_w.shape[2]
    g_cw_mine = lax.dynamic_slice_in_dim(g_cw_full, me * c_loc, c_loc, axis=2)
    update("conv_w", [g_cw_mine.reshape(1, n_layers * 3, c_loc)])

    outs = [loss, grad_x[None]]
    for n in range(4):
        outs += [out[k][n] for k in WEIGHTS]
    return tuple(outs)
```

```python
import functools

import jax
import jax.numpy as jnp
from jax import lax
from jax.experimental import pallas as pl
from jax.experimental.pallas import tpu as pltpu

F32 = jnp.float32
BF16 = jnp.bfloat16
HEAD_DIM = 128
QK_SCALE = HEAD_DIM ** -0.5
SKIP = 110.0
BOUND_SLACK = 1.0 + 2.0 ** -6
LANES = 128
EPS = 1e-6
N_DEV = 8
ADAM_LR = 0.001
ADAM_B1 = 0.9
ADAM_B2 = 0.999
ADAM_EPS = 1e-08
ADAM_WD = 0.01
ADAM_STEP = 10
VMEM_LIMIT = 56 * 1024 * 1024
ADAMW_VMEM = 24 * 1024 * 1024
MESH = pl.DeviceIdType.MESH

NT_DIMS = (((1,), (1,)), ((), ()))
TN_DIMS = (((0,), (0,)), ((), ()))


def _tile(n, cap, mult=LANES):
    t = (min(cap, n) // mult) * mult
    while t >= mult:
        if n % t == 0:
            return t
        t -= mult
    return n


def _params(sem, vmem=VMEM_LIMIT):
    return pltpu.CompilerParams(dimension_semantics=sem, vmem_limit_bytes=vmem)


def _split_dot(x, u, n_split, x_left=True):
    acc = None
    rest = x
    for s in range(n_split):
        piece = rest.astype(BF16)
        if s + 1 < n_split:
            rest = rest - piece.astype(F32)
        d = (jnp.dot(piece, u, preferred_element_type=F32) if x_left
             else jnp.dot(u, piece, preferred_element_type=F32))
        acc = d if acc is None else acc + d
    return acc


def _tri(n, kind):
    r = lax.broadcasted_iota(jnp.int32, (n, n), 0)
    c = lax.broadcasted_iota(jnp.int32, (n, n), 1)
    return jnp.where(r >= c if kind == "ge" else r <= c, 1.0, 0.0).astype(BF16)


def _mm_nn(name, a_list, g_list, w, resid=None, tm=1024, tn_cap=1024, ride=None):
    s_len = a_list[0].shape[0]
    k_dim, n_dim = w.shape
    tm = _tile(s_len, tm, 8)
    tn = _tile(n_dim, tn_cap)
    normed = bool(g_list)

    def body(*refs):
        refs = list(refs)
        a_refs = [refs.pop(0) for _ in a_list]
        g_refs = [refs.pop(0) for _ in g_list]
        w_ref = refs.pop(0)
        r_ref = refs.pop(0) if resid is not None else None
        o_ref = refs.pop(0)
        if normed:
            h_ref = refs.pop(0)

            @pl.when(pl.program_id(1) == 0)
            def _():
                off = 0
                for a_ref, g_ref in zip(a_refs, g_refs):
                    xv = a_ref[...]
                    kk = xv.shape[1]
                    r = lax.rsqrt(jnp.mean(xv * xv, axis=1, keepdims=True) + EPS)
                    h_ref[:, off:off + kk] = (xv * r * g_ref[...]).astype(BF16)
                    off += kk

            a = h_ref[...]
        else:
            a = a_refs[0][...]
        acc = jnp.dot(a, w_ref[...], preferred_element_type=F32)
        if r_ref is not None:
            acc = acc + r_ref[...]
        o_ref[...] = acc

    in_specs = [pl.BlockSpec((tm, a.shape[1]), lambda i, j: (i, 0)) for a in a_list]
    in_specs += [pl.BlockSpec((1, g.shape[1]), lambda i, j: (0, 0)) for g in g_list]
    in_specs += [pl.BlockSpec((k_dim, tn), lambda i, j: (0, j))]
    args = list(a_list) + list(g_list) + [w]
    if resid is not None:
        in_specs.append(pl.BlockSpec((tm, tn), lambda i, j: (i, j)))
        args.append(resid)
    out_shape = [jax.ShapeDtypeStruct((s_len, n_dim), F32)]
    out_specs = [pl.BlockSpec((tm, tn), lambda i, j: (i, j))]
    if normed:
        out_shape.append(jax.ShapeDtypeStruct((s_len, k_dim), BF16))
        out_specs.append(pl.BlockSpec((tm, k_dim), lambda i, j: (i, 0)))
    res, lands = _hosted_call(body, ride, name=name, grid=(s_len // tm, n_dim // tn), in_specs=in_specs,
                              out_specs=out_specs, out_shape=out_shape, sem=("parallel", "arbitrary"), args=args)
    res = res if normed else res[0]
    return res if ride is None else (res, lands)


def _rms_bwd(dh, xv, gv, r=None):
    if r is None:
        r = lax.rsqrt(jnp.mean(xv * xv, axis=1, keepdims=True) + EPS)
    xhat = xv * r
    dxh = dh * gv
    dx = r * (dxh - xhat * jnp.mean(dxh * xhat, axis=1, keepdims=True))
    return dx, dh * xhat


def _mm_nt(name, da, w, mode="plain", xs=(), gs=(), dres=None, tm=512, tk_cap=1024, ride=None):
    s_len, kc = da.shape
    n_out = w.shape[0]
    tm = _tile(s_len, tm, 8)
    tk = _tile(kc, tk_cap)
    nk = kc // tk

    def body(*refs):
        refs = list(refs)
        da_ref, w_ref = refs.pop(0), refs.pop(0)
        x_refs = [refs.pop(0) for _ in xs]
        g_refs = [refs.pop(0) for _ in gs]
        dres_ref = refs.pop(0) if dres is not None else None
        acc_ref = refs[0] if mode == "plain" else refs.pop()
        out_refs = refs
        i, k = pl.program_id(0), pl.program_id(1)

        @pl.when(k == 0)
        def _():
            acc_ref[...] = jnp.zeros_like(acc_ref)

        acc_ref[...] += lax.dot_general(da_ref[...].astype(BF16), w_ref[...], NT_DIMS,
                                        preferred_element_type=F32)

        if mode == "plain":
            return

        @pl.when(k == nk - 1)
        def _():
            n_x = len(xs)
            dx_refs, dg_refs = out_refs[:n_x], out_refs[n_x:]

            @pl.when(i == 0)
            def _():
                for dg_ref in dg_refs:
                    dg_ref[...] = jnp.zeros_like(dg_ref)

            off = 0
            for x_ref, g_ref, dx_ref, dg_ref in zip(x_refs, g_refs, dx_refs, dg_refs):
                kk = x_ref.shape[1]
                dx, dgp = _rms_bwd(acc_ref[:, off:off + kk], x_ref[...], g_ref[...])
                if dres_ref is not None:
                    dx = dx + dres_ref[...]
                dx_ref[...] = dx
                dg_ref[...] += jnp.sum(dgp, axis=0, keepdims=True)
                off += kk

    in_specs = [pl.BlockSpec((tm, tk), lambda i, k: (i, k)), pl.BlockSpec((n_out, tk), lambda i, k: (0, k))]
    in_specs += [pl.BlockSpec((tm, x.shape[1]), lambda i, k: (i, 0)) for x in xs]
    in_specs += [pl.BlockSpec((1, g.shape[1]), lambda i, k: (0, 0)) for g in gs]
    args = [da, w] + list(xs) + list(gs)
    if dres is not None:
        in_specs.append(pl.BlockSpec((tm, n_out), lambda i, k: (i, 0)))
        args.append(dres)
    if mode == "plain":
        out_shape = [jax.ShapeDtypeStruct((s_len, n_out), F32)]
        out_specs = [pl.BlockSpec((tm, n_out), lambda i, k: (i, 0))]
    else:
        out_shape = [jax.ShapeDtypeStruct((s_len, x.shape[1]), F32) for x in xs]
        out_specs = [pl.BlockSpec((tm, x.shape[1]), lambda i, k: (i, 0)) for x in xs]
        out_shape += [jax.ShapeDtypeStruct((1, x.shape[1]), F32) for x in xs]
        out_specs += [pl.BlockSpec((1, x.shape[1]), lambda i, k: (0, 0)) for x in xs]
    res, lands = _hosted_call(body, ride, name=name, grid=(s_len // tm, nk), in_specs=in_specs, out_specs=out_specs,
                              out_shape=out_shape, sem=("arbitrary", "arbitrary"), args=args,
                              scratch_shapes=[] if mode == "plain" else [pltpu.VMEM((tm, n_out), F32)])
    res = res[0] if mode == "plain" else res
    return res if ride is None else (res, lands)


def _mm_tn(name, a, b, tk_cap=1024, tn_cap=2048, tm=1024, ride=None):
    s_len, k_dim = a.shape
    n_dim = b.shape[1]
    tk = _tile(k_dim, tk_cap)
    tn = _tile(n_dim, tn_cap)
    tm = _tile(s_len, tm, 8)
    nm = s_len // tm

    def body(a_ref, b_ref, o_ref, acc_ref):
        m = pl.program_id(2)

        @pl.when(m == 0)
        def _():
            acc_ref[...] = jnp.zeros_like(acc_ref)

        acc_ref[...] += lax.dot_general(a_ref[...].astype(BF16), b_ref[...].astype(BF16), TN_DIMS,
                                        preferred_element_type=F32)

        @pl.when(m == nm - 1)
        def _():
            o_ref[...] = acc_ref[...].astype(BF16)

    res, lands = _hosted_call(
        body, ride, name=name, grid=(k_dim // tk, n_dim // tn, nm),
        in_specs=[pl.BlockSpec((tm, tk), lambda i, j, m: (m, i)), pl.BlockSpec((tm, tn), lambda i, j, m: (m, j))],
        out_specs=[pl.BlockSpec((tk, tn), lambda i, j, m: (i, j))],
        out_shape=[jax.ShapeDtypeStruct((k_dim, n_dim), BF16)],
        scratch_shapes=[pltpu.VMEM((tk, tn), F32)], sem=("parallel", "parallel", "arbitrary"), args=[a, b])
    return res[0] if ride is None else (res[0], lands)


def _neg_softplus(z):
    e = jnp.exp(-jnp.abs(z))
    return -(jnp.maximum(z, 0.0) + jnp.log(1.0 + e)), e


def _prep_fwd(name, proj, qg, kg, bf, n_heads, tm=256):
    s_len, n_p = proj.shape
    w_dim = n_heads * HEAD_DIM
    tm = _tile(s_len, tm, 8)

    def body(p_ref, qg_ref, kg_ref, bf_ref, pa_ref, c_ref, carry_ref):
        @pl.when(pl.program_id(0) == 0)
        def _():
            carry_ref[...] = jnp.zeros_like(carry_ref)

        for base, g_ref, mul in ((0, qg_ref, QK_SCALE), (w_dim, kg_ref, None)):
            for hh in range(n_heads):
                sl = slice(base + hh * HEAD_DIM, base + (hh + 1) * HEAD_DIM)
                xv = p_ref[:, sl]
                r = lax.rsqrt(jnp.mean(xv * xv, axis=1, keepdims=True) + EPS)
                y = xv * r * g_ref[...]
                pa_ref[:, sl] = (y if mul is None else y * mul).astype(BF16)
        pa_ref[:, 2 * w_dim:3 * w_dim] = p_ref[:, 2 * w_dim:3 * w_dim].astype(BF16)
        pa_ref[:, 3 * w_dim:4 * w_dim] = (p_ref[:, 3 * w_dim:4 * w_dim] * QK_SCALE).astype(BF16)
        pa_ref[:, 4 * w_dim:] = p_ref[:, 4 * w_dim:6 * w_dim].astype(BF16)
        f = p_ref[:, 6 * w_dim:] + bf_ref[...]
        lf, _ = _neg_softplus(-f)
        lane = lax.broadcasted_iota(jnp.int32, lf.shape, 1)
        lf = jnp.where(lane < n_heads, lf, 0.0)
        cb = _split_dot(lf, _tri(tm, "ge"), 3, x_left=False) + carry_ref[...]
        c_ref[...] = cb
        carry_ref[...] = cb[tm - 1:tm, :]

    return pl.pallas_call(
        body, name=name, grid=(s_len // tm,),
        in_specs=[pl.BlockSpec((tm, n_p), lambda i: (i, 0))] + [pl.BlockSpec((1, LANES), lambda i: (0, 0))] * 3,
        out_specs=[pl.BlockSpec((tm, 6 * w_dim), lambda i: (i, 0)), pl.BlockSpec((tm, LANES), lambda i: (i, 0))],
        out_shape=[jax.ShapeDtypeStruct((s_len, 6 * w_dim), BF16), jax.ShapeDtypeStruct((s_len, LANES), F32)],
        scratch_shapes=[pltpu.VMEM((1, LANES), F32)],
        compiler_params=_params(("arbitrary",)))(proj, qg, kg, bf)


def _prep_bwd(name, proj, qg, kg, bf, d_fox, d_sb, dc, n_heads, tm=256):
    s_len, n_p = proj.shape
    w_dim = n_heads * HEAD_DIM
    tm = _tile(s_len, tm, 8)
    nb = s_len // tm

    def body(p_ref, qg_ref, kg_ref, bf_ref, dqa_ref, dka_ref, dva_ref, dqb_ref, dkb_ref, dvb_ref, dc_ref,
             dp_ref, dqg_ref, dkg_ref, dbf_ref, carry_ref):
        @pl.when(pl.program_id(0) == 0)
        def _():
            for ref in (carry_ref, dqg_ref, dkg_ref, dbf_ref):
                ref[...] = jnp.zeros_like(ref)

        for base, g_ref, d_ref, dg_ref in ((0, qg_ref, dqa_ref, dqg_ref), (w_dim, kg_ref, dka_ref, dkg_ref)):
            dg = jnp.zeros((1, HEAD_DIM), F32)
            for hh in range(n_heads):
                sl = slice(base + hh * HEAD_DIM, base + (hh + 1) * HEAD_DIM)
                dx, dgp = _rms_bwd(d_ref[:, hh * HEAD_DIM:(hh + 1) * HEAD_DIM], p_ref[:, sl], g_ref[...])
                dp_ref[:, sl] = dx.astype(BF16)
                dg = dg + jnp.sum(dgp, axis=0, keepdims=True)
            dg_ref[...] += dg
        for n, d_ref in enumerate((dva_ref, dqb_ref, dkb_ref, dvb_ref)):
            dp_ref[:, (2 + n) * w_dim:(3 + n) * w_dim] = d_ref[...].astype(BF16)
        dlf = _split_dot(dc_ref[...], _tri(tm, "le"), 3, x_left=False) + carry_ref[...]
        carry_ref[...] = dlf[0:1, :]
        f = p_ref[:, 6 * w_dim:] + bf_ref[...]
        e = jnp.exp(-jnp.abs(f))
        sig_neg = jnp.where(f >= 0, e, 1.0) / (1.0 + e)
        lane = lax.broadcasted_iota(jnp.int32, f.shape, 1)
        df = jnp.where(lane < n_heads, dlf * sig_neg, 0.0)
        dp_ref[:, 6 * w_dim:] = df.astype(BF16)
        dbf_ref[...] += jnp.sum(df, axis=0, keepdims=True)

    rev = lambda i: (nb - 1 - i, 0)
    vec = pl.BlockSpec((1, LANES), lambda i: (0, 0))
    return pl.pallas_call(
        body, name=name, grid=(nb,),
        in_specs=[pl.BlockSpec((tm, n_p), rev), vec, vec, vec] + [pl.BlockSpec((tm, w_dim), rev)] * 6
        + [pl.BlockSpec((tm, LANES), rev)],
        out_specs=[pl.BlockSpec((tm, n_p), rev), vec, vec, vec],
        out_shape=[jax.ShapeDtypeStruct((s_len, n_p), BF16)] + [jax.ShapeDtypeStruct((1, LANES), F32)] * 3,
        scratch_shapes=[pltpu.VMEM((1, LANES), F32)],
        compiler_params=_params(("arbitrary",)))(proj, qg, kg, bf, *d_fox, *d_sb, dc)


def _head_col(c_blk, h):
    lane = lax.broadcasted_iota(jnp.int32, c_blk.shape, 1)
    return jnp.sum(jnp.where(lane == h, c_blk, 0.0), axis=1, keepdims=True)


def _key_norm_max(k_ref):
    kf = k_ref[...].astype(F32)
    return jnp.sqrt(jnp.max(jnp.sum(kf * kf, axis=1, keepdims=True), axis=0, keepdims=True))


def _logit_bound(q, kmax):
    qf = q.astype(F32)
    return jnp.sqrt(jnp.sum(qf * qf, axis=1, keepdims=True)) * kmax * BOUND_SLACK


def _c_block_end(ct_ref, h, kb, tk):
    return jnp.min(ct_ref[pl.ds(h, 1), pl.ds(pl.multiple_of(kb * tk, tk), tk)])


def _fox_fwd(name, pa, c, ct, n_heads, tq=1024):
    s_len = pa.shape[0]
    tq = _tile(s_len, tq, LANES)
    hp = ct.shape[0]

    def body(q_ref, k_ref, v_ref, c_ref, ct_ref, o_ref, lse_ref, kmax_ref):
        h, qi = pl.program_id(0), pl.program_id(1)

        @pl.when(qi == 0)
        def _():
            kmax_ref[...] = _key_norm_max(k_ref)

        q = q_ref[...]
        cq = _head_col(c_ref[...], h)
        top = _logit_bound(q, kmax_ref[...]) + cq
        row = lax.broadcasted_iota(jnp.int32, (tq, tq), 0)
        col = lax.broadcasted_iota(jnp.int32, (tq, tq), 1)

        def step(kb, carry, masked):
            m, l, acc = carry
            ks = pl.multiple_of(kb * tq, tq)
            k = k_ref[pl.ds(ks, tq), :]
            v = v_ref[pl.ds(ks, tq), :]
            ck = ct_ref[pl.ds(h, 1), pl.ds(ks, tq)]
            s = lax.dot_general(q, k, NT_DIMS, preferred_element_type=F32) + (cq - ck)
            if masked:
                s = jnp.where(col <= row, s, -jnp.inf)
            m_new = jnp.maximum(m, jnp.max(s, axis=1, keepdims=True))
            alpha = jnp.exp(m - m_new)
            p = jnp.exp(s - m_new)
            l = alpha * l + jnp.sum(p, axis=1, keepdims=True)
            acc = alpha * acc + jnp.dot(p.astype(BF16), v, preferred_element_type=F32)
            return m_new, l, acc

        def margin(kb, m):
            return jnp.max(top - m) - _c_block_end(ct_ref, h, jnp.maximum(kb, 0), tq)

        def walk(state):
            kb, _, m, l, acc = state
            m, l, acc = step(kb, (m, l, acc), False)
            return kb - 1, margin(kb - 1, m), m, l, acc

        init = (jnp.full((tq, 1), -jnp.inf, F32), jnp.zeros((tq, 1), F32), jnp.zeros((tq, HEAD_DIM), F32))
        carry = step(qi, init, True)
        state = (qi - 1, margin(qi - 1, carry[0])) + carry
        _, _, m, l, acc = lax.while_loop(lambda st: (st[0] >= 0) & (st[1] > -SKIP), walk, state)
        o_ref[...] = acc / l
        lse_ref[0] = m + jnp.log(l)

    return pl.pallas_call(
        body, name=name, grid=(n_heads, s_len // tq),
        in_specs=[pl.BlockSpec((tq, HEAD_DIM), lambda h, i: (i, h)),
                  pl.BlockSpec((s_len, HEAD_DIM), lambda h, i: (0, n_heads + h)),
                  pl.BlockSpec((s_len, HEAD_DIM), lambda h, i: (0, 2 * n_heads + h)),
                  pl.BlockSpec((tq, LANES), lambda h, i: (i, 0)),
                  pl.BlockSpec((hp, s_len), lambda h, i: (0, 0))],
        out_specs=[pl.BlockSpec((tq, HEAD_DIM), lambda h, i: (i, h)),
                   pl.BlockSpec((1, tq, 1), lambda h, i: (h, i, 0))],
        out_shape=[jax.ShapeDtypeStruct((s_len, n_heads * HEAD_DIM), F32),
                   jax.ShapeDtypeStruct((n_heads, s_len, 1), F32)],
        scratch_shapes=[pltpu.VMEM((1, 1), F32)],
        compiler_params=_params(("parallel", "arbitrary")))(pa, pa, pa, c, ct)


def _fox_bwd(name, pa, c, ct, o, do, lse, n_heads, tq=1024):
    s_len = pa.shape[0]
    tq = _tile(s_len, tq, LANES)
    hp = ct.shape[0]
    w_dim = n_heads * HEAD_DIM

    def body(q_ref, k_ref, v_ref, c_ref, ct_ref, o_ref, do_ref, lse_ref, dq_ref, dk_ref, dv_ref, dct_ref, dcq_ref,
             kmax_ref):
        h, qi = pl.program_id(0), pl.program_id(1)

        @pl.when(qi == 0)
        def _():
            dk_ref[...] = jnp.zeros_like(dk_ref)
            dv_ref[...] = jnp.zeros_like(dv_ref)
            kmax_ref[...] = _key_norm_max(k_ref)

        @pl.when((qi == 0) & (h == 0))
        def _():
            dct_ref[...] = jnp.zeros_like(dct_ref)

        q = q_ref[...]
        do32 = do_ref[...]
        dob = do32.astype(BF16)
        dsum = jnp.sum(do32 * o_ref[...], axis=1, keepdims=True)
        lse_v = lse_ref[0]
        cq = _head_col(c_ref[...], h)
        row = lax.broadcasted_iota(jnp.int32, (tq, tq), 0)
        col = lax.broadcasted_iota(jnp.int32, (tq, tq), 1)

        def step(kb, carry, masked):
            dq, dcq = carry
            ks = pl.multiple_of(kb * tq, tq)
            k = k_ref[pl.ds(ks, tq), :]
            v = v_ref[pl.ds(ks, tq), :]
            ck = ct_ref[pl.ds(h, 1), pl.ds(ks, tq)]
            s = lax.dot_general(q, k, NT_DIMS, preferred_element_type=F32) + (cq - ck)
            p = jnp.exp(s - lse_v)
            if masked:
                p = jnp.where(col <= row, p, 0.0)
            dp = lax.dot_general(dob, v, NT_DIMS, preferred_element_type=F32)
            ds = p * (dp - dsum)
            dsb = ds.astype(BF16)
            dk_ref[pl.ds(ks, tq), :] += lax.dot_general(dsb, q, TN_DIMS, preferred_element_type=F32)
            dv_ref[pl.ds(ks, tq), :] += lax.dot_general(p.astype(BF16), dob, TN_DIMS, preferred_element_type=F32)
            dct_ref[pl.ds(h, 1), pl.ds(ks, tq)] -= jnp.sum(ds, axis=0, keepdims=True)
            return dq + jnp.dot(dsb, k, preferred_element_type=F32), dcq + jnp.sum(ds, axis=1, keepdims=True)

        top = jnp.max(_logit_bound(q, kmax_ref[...]) + cq - lse_v)

        def margin(kb):
            return top - _c_block_end(ct_ref, h, jnp.maximum(kb, 0), tq)

        def walk(state):
            kb, _, dq, dcq = state
            dq, dcq = step(kb, (dq, dcq), False)
            return kb - 1, margin(kb - 1), dq, dcq

        init = (jnp.zeros((tq, HEAD_DIM), F32), jnp.zeros((tq, 1), F32))
        state = (qi - 1, margin(qi - 1)) + step(qi, init, True)
        _, _, dq, dcq = lax.while_loop(lambda st: (st[0] >= 0) & (st[1] > -SKIP), walk, state)
        dq_ref[...] = dq * QK_SCALE
        dcq_ref[0] = dcq

    blk = pl.BlockSpec((tq, HEAD_DIM), lambda h, i: (i, h))
    full = pl.BlockSpec((s_len, HEAD_DIM), lambda h, i: (0, h))
    return pl.pallas_call(
        body, name=name, grid=(n_heads, s_len // tq),
        in_specs=[blk,
                  pl.BlockSpec((s_len, HEAD_DIM), lambda h, i: (0, n_heads + h)),
                  pl.BlockSpec((s_len, HEAD_DIM), lambda h, i: (0, 2 * n_heads + h)),
                  pl.BlockSpec((tq, LANES), lambda h, i: (i, 0)),
                  pl.BlockSpec((hp, s_len), lambda h, i: (0, 0)),
                  blk, blk,
                  pl.BlockSpec((1, tq, 1), lambda h, i: (h, i, 0))],
        out_specs=[blk, full, full, pl.BlockSpec((hp, s_len), lambda h, i: (0, 0)),
                   pl.BlockSpec((1, tq, 1), lambda h, i: (h, i, 0))],
        out_shape=[jax.ShapeDtypeStruct((s_len, w_dim), F32)] * 3 + [jax.ShapeDtypeStruct((hp, s_len), F32),
                                                                     jax.ShapeDtypeStruct((n_heads, s_len, 1), F32)],
        scratch_shapes=[pltpu.VMEM((1, 1), F32)],
        compiler_params=_params(("arbitrary", "arbitrary")))(pa, pa, pa, c, ct, o, do, lse)


def _sb_fwd(name, pa, n_heads, tq=512, tk=256):
    s_len = pa.shape[0]
    tq = _tile(s_len, tq, LANES)
    tk = _tile(tq, tk, LANES)
    nsub = tq // tk

    def body(q_ref, k_ref, v_ref, o_ref, tot_ref, kst_ref, kmax_ref):
        qi = pl.program_id(1)

        @pl.when(qi == 0)
        def _():
            kmax_ref[...] = _key_norm_max(k_ref)

        q = q_ref[...]
        z_bound = _logit_bound(q, kmax_ref[...])
        u = _tri(tk, "ge")
        row = lax.broadcasted_iota(jnp.int32, (tq, tk), 0)
        col = lax.broadcasted_iota(jnp.int32, (tq, tk), 1)

        def block(ks, carry, mask_off):
            r, acc = carry
            k = k_ref[pl.ds(ks, tk), :]
            v = v_ref[pl.ds(ks, tk), :]
            z = lax.dot_general(q, k, NT_DIMS, preferred_element_type=F32)
            a, _ = _neg_softplus(z)
            if mask_off is not None:
                valid = col + mask_off < row
                a = jnp.where(valid, a, 0.0)
            rin = _split_dot(a, u, 2)
            w = jnp.exp(z + (r + rin))
            if mask_off is not None:
                w = jnp.where(valid, w, 0.0)
            acc = acc + jnp.dot(w.astype(BF16), v, preferred_element_type=F32)
            return r + rin[:, 0:1], acc

        carry = (jnp.zeros((tq, 1), F32), jnp.zeros((tq, HEAD_DIM), F32))
        q0 = pl.multiple_of(qi * tq, tq)
        for j in reversed(range(nsub)):
            carry = block(q0 + j * tk, carry, j * tk)
        def walk(state):
            kb, _, r, acc = state
            r, acc = block(pl.multiple_of(kb * tk, tk), (r, acc), None)
            return kb - 1, jnp.max(r + z_bound), r, acc

        state = (qi * nsub - 1, jnp.max(carry[0] + z_bound)) + carry
        kb, _, r, acc = lax.while_loop(lambda st: (st[0] >= 0) & (st[1] > -SKIP), walk, state)
        o_ref[...] = acc
        tot_ref[0] = r
        kst_ref[...] = jnp.full(kst_ref.shape, (kb + 1).astype(F32))

    nq = s_len // tq
    return pl.pallas_call(
        body, name=name, grid=(n_heads, nq),
        in_specs=[pl.BlockSpec((tq, HEAD_DIM), lambda h, i: (i, 3 * n_heads + h)),
                  pl.BlockSpec((s_len, HEAD_DIM), lambda h, i: (0, 4 * n_heads + h)),
                  pl.BlockSpec((s_len, HEAD_DIM), lambda h, i: (0, 5 * n_heads + h))],
        out_specs=[pl.BlockSpec((tq, HEAD_DIM), lambda h, i: (i, h)),
                   pl.BlockSpec((1, tq, 1), lambda h, i: (h, i, 0)),
                   pl.BlockSpec((1, 1, 8, LANES), lambda h, i: (h, i, 0, 0))],
        out_shape=[jax.ShapeDtypeStruct((s_len, n_heads * HEAD_DIM), F32),
                   jax.ShapeDtypeStruct((n_heads, s_len, 1), F32),
                   jax.ShapeDtypeStruct((n_heads, nq, 8, LANES), F32)],
        scratch_shapes=[pltpu.VMEM((1, 1), F32)],
        compiler_params=_params(("parallel", "arbitrary")))(pa, pa, pa)


def _sb_bwd(name, pa, do, tot, kst, n_heads, tq=512, tk=256):
    s_len = pa.shape[0]
    tq = _tile(s_len, tq, LANES)
    tk = _tile(tq, tk, LANES)
    nsub = tq // tk
    w_dim = n_heads * HEAD_DIM

    def body(q_ref, k_ref, v_ref, do_ref, tot_ref, kst_ref, dq_ref, dk_ref, dv_ref):
        qi = pl.program_id(1)

        @pl.when(qi == 0)
        def _():
            dk_ref[...] = jnp.zeros_like(dk_ref)
            dv_ref[...] = jnp.zeros_like(dv_ref)

        q = q_ref[...]
        dob = do_ref[...].astype(BF16)
        u = _tri(tk, "le")
        row = lax.broadcasted_iota(jnp.int32, (tq, tk), 0)
        col = lax.broadcasted_iota(jnp.int32, (tq, tk), 1)

        def block(ks, carry, mask_off):
            rem, cpre, dq = carry
            k = k_ref[pl.ds(ks, tk), :]
            v = v_ref[pl.ds(ks, tk), :]
            z = lax.dot_general(q, k, NT_DIMS, preferred_element_type=F32)
            a, e = _neg_softplus(z)
            if mask_off is not None:
                valid = col + mask_off < row
                a = jnp.where(valid, a, 0.0)
            pin = _split_dot(a, u, 2)
            w = jnp.exp(z + (rem - (pin - a)))
            if mask_off is not None:
                w = jnp.where(valid, w, 0.0)
            g = w * lax.dot_general(dob, v, NT_DIMS, preferred_element_type=F32)
            cin = _split_dot(g, u, 2)
            beta = jnp.where(z >= 0, 1.0, e) / (1.0 + e)
            dz = g - beta * (cpre + cin)
            if mask_off is not None:
                dz = jnp.where(valid, dz, 0.0)
            dzb = dz.astype(BF16)
            dk_ref[pl.ds(ks, tk), :] += lax.dot_general(dzb, q, TN_DIMS, preferred_element_type=F32)
            dv_ref[pl.ds(ks, tk), :] += lax.dot_general(w.astype(BF16), dob, TN_DIMS, preferred_element_type=F32)
            dq = dq + jnp.dot(dzb, k, preferred_element_type=F32)
            return rem - pin[:, tk - 1:tk], cpre + cin[:, tk - 1:tk], dq

        carry = (tot_ref[0], jnp.zeros((tq, 1), F32), jnp.zeros((tq, HEAD_DIM), F32))
        first = jnp.max(kst_ref[0, 0]).astype(jnp.int32)
        carry = lax.fori_loop(first, qi * nsub, lambda n, cr: block(pl.multiple_of(n * tk, tk), cr, None), carry)
        q0 = pl.multiple_of(qi * tq, tq)
        for j in range(nsub):
            carry = block(q0 + j * tk, carry, j * tk)
        dq_ref[...] = carry[2] * QK_SCALE

    blk = pl.BlockSpec((tq, HEAD_DIM), lambda h, i: (i, h))
    full = pl.BlockSpec((s_len, HEAD_DIM), lambda h, i: (0, h))
    return pl.pallas_call(
        body, name=name, grid=(n_heads, s_len // tq),
        in_specs=[pl.BlockSpec((tq, HEAD_DIM), lambda h, i: (i, 3 * n_heads + h)),
                  pl.BlockSpec((s_len, HEAD_DIM), lambda h, i: (0, 4 * n_heads + h)),
                  pl.BlockSpec((s_len, HEAD_DIM), lambda h, i: (0, 5 * n_heads + h)),
                  blk,
                  pl.BlockSpec((1, tq, 1), lambda h, i: (h, i, 0)),
                  pl.BlockSpec((1, 1, 8, LANES), lambda h, i: (h, i, 0, 0))],
        out_specs=[blk, full, full],
        out_shape=[jax.ShapeDtypeStruct((s_len, w_dim), F32)] * 3,
        compiler_params=_params(("arbitrary", "arbitrary")))(pa, pa, pa, do, tot, kst)


def _conv_rows(ext_ref, u_ref, halo_ref, first, tm):
    ext_ref[0:8, :] = jnp.where(first, 0.0, halo_ref[...])
    ext_ref[8:, :] = u_ref[...]
    return ext_ref[8:8 + tm, :], ext_ref[7:7 + tm, :], ext_ref[6:6 + tm, :]


def _sigmoid(x):
    return 1.0 / (1.0 + jnp.exp(-x))


def _conv_specs(tm, tc, nj, order):
    hb = tm // 8
    ij = order

    def at(f):
        return lambda *g: f(*ij(*g))

    return [pl.BlockSpec((tm, tc), at(lambda i, j: (i, j))),
            pl.BlockSpec((tm, tc), at(lambda i, j: (i, j + nj))),
            pl.BlockSpec((8, tc), at(lambda i, j: (jnp.maximum(i * hb - 1, 0), j))),
            pl.BlockSpec((8, tc), at(lambda i, j: (jnp.maximum(i * hb - 1, 0), j + nj))),
            pl.BlockSpec((3, tc), at(lambda i, j: (0, j))),
            pl.BlockSpec((3, tc), at(lambda i, j: (0, j + nj))),
            pl.BlockSpec((1, tc), at(lambda i, j: (0, j))),
            pl.BlockSpec((1, tc), at(lambda i, j: (0, j + nj)))]


def _conv_gate_fwd(name, u, cw, cb, tm=512, tc=512):
    s_len, f2 = u.shape
    f_dim = f2 // 2
    tm = _tile(s_len, tm, 8)
    tc = _tile(f_dim, tc)
    nj = f_dim // tc

    def body(ug_ref, uv_ref, hg_ref, hv_ref, wg_ref, wv_ref, bg_ref, bv_ref, g_ref, ext_ref):
        first = pl.program_id(0) == 0

        def conv(u_ref, h_ref, w_ref, b_ref):
            u0, u1, u2 = _conv_rows(ext_ref, u_ref, h_ref, first, tm)
            return w_ref[2:3, :] * u0 + w_ref[1:2, :] * u1 + w_ref[0:1, :] * u2 + b_ref[...]

        gc = conv(ug_ref, hg_ref, wg_ref, bg_ref)
        vc = conv(uv_ref, hv_ref, wv_ref, bv_ref)
        g_ref[...] = (gc * _sigmoid(gc) * vc).astype(BF16)

    return pl.pallas_call(
        body, name=name, grid=(s_len // tm, nj),
        in_specs=_conv_specs(tm, tc, nj, lambda i, j: (i, j)),
        out_specs=pl.BlockSpec((tm, tc), lambda i, j: (i, j)),
        out_shape=jax.ShapeDtypeStruct((s_len, f_dim), BF16),
        scratch_shapes=[pltpu.VMEM((tm + 8, tc), F32)],
        compiler_params=_params(("parallel", "parallel")))(u, u, u, u, cw, cw, cb, cb)


def _conv_gate_bwd(name, u, dg, cw, cb, tm=512, tc=512):
    s_len, f2 = u.shape
    f_dim = f2 // 2
    tm = _tile(s_len, tm, 8)
    tc = _tile(f_dim, tc)
    nj = f_dim // tc

    def body(ug_ref, uv_ref, hg_ref, hv_ref, wg_ref, wv_ref, bg_ref, bv_ref, dg_ref, duc_ref, dcw_ref, dcb_ref,
             eg_ref, ev_ref):
        first = pl.program_id(1) == 0

        @pl.when(first)
        def _():
            dcw_ref[...] = jnp.zeros_like(dcw_ref)
            dcb_ref[...] = jnp.zeros_like(dcb_ref)

        ug = _conv_rows(eg_ref, ug_ref, hg_ref, first, tm)
        uv = _conv_rows(ev_ref, uv_ref, hv_ref, first, tm)
        gc = wg_ref[2:3, :] * ug[0] + wg_ref[1:2, :] * ug[1] + wg_ref[0:1, :] * ug[2] + bg_ref[...]
        vc = wv_ref[2:3, :] * uv[0] + wv_ref[1:2, :] * uv[1] + wv_ref[0:1, :] * uv[2] + bv_ref[...]
        sg = _sigmoid(gc)
        dgv = dg_ref[...]
        dvc = dgv * (gc * sg)
        dgc = dgv * vc * (sg * (1.0 + gc * (1.0 - sg)))
        duc_ref[0] = dgc
        duc_ref[1] = dvc
        for half, (d, us) in enumerate(((dgc, ug), (dvc, uv))):
            dcb_ref[half] += jnp.sum(d, axis=0, keepdims=True)
            for tap in range(3):
                dcw_ref[half, tap:tap + 1, :] += jnp.sum(d * us[2 - tap], axis=0, keepdims=True)

    order = lambda j, i: (i, j)
    return pl.pallas_call(
        body, name=name, grid=(nj, s_len // tm),
        in_specs=_conv_specs(tm, tc, nj, order) + [pl.BlockSpec((tm, tc), lambda j, i: (i, j))],
        out_specs=[pl.BlockSpec((2, tm, tc), lambda j, i: (0, i, j)),
                   pl.BlockSpec((2, 3, tc), lambda j, i: (0, 0, j)),
                   pl.BlockSpec((2, 1, tc), lambda j, i: (0, 0, j))],
        out_shape=[jax.ShapeDtypeStruct((2, s_len, f_dim), F32), jax.ShapeDtypeStruct((2, 3, f_dim), F32),
                   jax.ShapeDtypeStruct((2, 1, f_dim), F32)],
        scratch_shapes=[pltpu.VMEM((tm + 8, tc), F32), pltpu.VMEM((tm + 8, tc), F32)],
        compiler_params=_params(("parallel", "arbitrary")))(u, u, u, u, cw, cw, cb, cb, dg)


def _conv_t(name, duc, cw, tm=512, tc=512):
    _, s_len, f_dim = duc.shape
    tm = _tile(s_len, tm, 8)
    tc = _tile(f_dim, tc)
    nj = f_dim // tc
    nb = s_len // tm
    hb = tm // 8

    def body(d_ref, halo_ref, w_ref, o_ref, ext_ref):
        last = pl.program_id(1) == nb - 1
        ext_ref[0:tm, :] = d_ref[0]
        ext_ref[tm:, :] = jnp.where(last, 0.0, halo_ref[0])
        o_ref[...] = (w_ref[2:3, :] * ext_ref[0:tm, :] + w_ref[1:2, :] * ext_ref[1:1 + tm, :]
                      + w_ref[0:1, :] * ext_ref[2:2 + tm, :]).astype(BF16)

    return pl.pallas_call(
        body, name=name, grid=(2, nb, nj),
        in_specs=[pl.BlockSpec((1, tm, tc), lambda p, i, j: (p, i, j)),
                  pl.BlockSpec((1, 8, tc), lambda p, i, j: (p, jnp.minimum((i + 1) * hb, nb * hb - 1), j)),
                  pl.BlockSpec((3, tc), lambda p, i, j: (0, p * nj + j))],
        out_specs=pl.BlockSpec((tm, tc), lambda p, i, j: (i, p * nj + j)),
        out_shape=jax.ShapeDtypeStruct((s_len, 2 * f_dim), BF16),
        scratch_shapes=[pltpu.VMEM((tm + 8, tc), F32)],
        compiler_params=_params(("parallel", "parallel", "parallel")))(duc, duc, cw)


def _loss_head(name, y, tgt, tm=512):
    s_len, d = y.shape
    tm = _tile(s_len, tm, 8)

    def body(y_ref, t_ref, dy_ref, l_ref):
        @pl.when(pl.program_id(0) == 0)
        def _():
            l_ref[...] = jnp.zeros_like(l_ref)

        err = y_ref[...] - t_ref[...]
        dy_ref[...] = err * (1.0 / d)
        l_ref[...] += 0.5 * jnp.sum(jnp.sum(err * err, axis=1, keepdims=True) * (1.0 / d), axis=0, keepdims=True)

    blk = pl.BlockSpec((tm, d), lambda i: (i, 0))
    return pl.pallas_call(
        body, name=name, grid=(s_len // tm,), in_specs=[blk, blk],
        out_specs=[blk, pl.BlockSpec((1, LANES), lambda i: (0, 0))],
        out_shape=[jax.ShapeDtypeStruct((s_len, d), F32), jax.ShapeDtypeStruct((1, LANES), F32)],
        compiler_params=_params(("arbitrary",)))(y, tgt)


def _adamw(name, parts_list, w, m, v, tr=256):
    n_l = len(parts_list)
    n_parts, rows, cols = parts_list[0].shape
    row_bytes = -(-cols // LANES) * LANES * (2 * n_l * n_parts * parts_list[0].dtype.itemsize + 2 * 7 * 4)
    tr = _tile(rows, min(tr, max(16, ADAMW_VMEM // row_bytes // 16 * 16)), 16)
    nb = rows // tr
    c1 = 1.0 - ADAM_B1 ** ADAM_STEP
    c2 = 1.0 - ADAM_B2 ** ADAM_STEP

    def body(*refs):
        p_refs = refs[:n_l]
        w_ref, m_ref, v_ref, g_ref, d_ref, nm_ref, nv_ref = refs[n_l:]
        for l, p_ref in enumerate(p_refs):
            @pl.when(pl.program_id(0) == l)
            def _(p_ref=p_ref):
                g = p_ref[0].astype(F32)
                for n in range(1, n_parts):
                    g = g + p_ref[n].astype(F32)
                nm = ADAM_B1 * m_ref[...] + (1.0 - ADAM_B1) * g
                nv = ADAM_B2 * v_ref[...] + (1.0 - ADAM_B2) * (g * g)
                g_ref[...] = g
                nm_ref[...] = nm
                nv_ref[...] = nv
                d_ref[...] = -ADAM_LR * ((nm / c1) / (jnp.sqrt(nv / c2) + ADAM_EPS) + ADAM_WD * w_ref[...])

    p_specs = [pl.BlockSpec((n_parts, tr, cols), lambda li, i, l=l: (0, jnp.where(li == l, i, 0), 0))
               for l in range(n_l)]
    blk = pl.BlockSpec((tr, cols), lambda li, i: (li * nb + i, 0))
    return pl.pallas_call(
        body, name=name, grid=(n_l, nb), in_specs=p_specs + [blk, blk, blk],
        out_specs=[blk] * 4, out_shape=[jax.ShapeDtypeStruct((n_l * rows, cols), F32)] * 4,
        compiler_params=_params(("arbitrary", "arbitrary")))(*parts_list, w, m, v)


def _peers():
    x, y, c = lax.axis_index("x"), lax.axis_index("y"), lax.axis_index("c")
    out = []
    for k in range(1, N_DEV):
        fx, fy, fc = (k >> 2) & 1, (k >> 1) & 1, k & 1
        px, py, pc = x ^ fx, y ^ fy, c ^ fc
        out.append((k - 1, (px, py, pc), 4 * px + 2 * py + pc))
    return 4 * x + 2 * y + c, out


def _block_of(ref, axis, index, size):
    if axis is None:
        return ref.at[index]
    idx = [slice(None)] * len(ref.shape)
    idx[axis] = pl.ds(pl.multiple_of(index * size, size), size)
    return ref.at[tuple(idx)]


def _land_shape(shape, axis, scatter):
    shape = list(shape)
    if scatter:
        if axis is None:
            return tuple(shape)
        shape[axis] //= N_DEV
        return (N_DEV, *shape)
    if axis is None:
        return (N_DEV, *shape)
    shape[axis] *= N_DEV
    return tuple(shape)


def _copy_ends(axis, scatter, src, land, me, idx):
    if scatter:
        size = None if axis is None else src.shape[axis] // N_DEV
        return _block_of(src, axis, idx, size), land.at[me]
    return src, _block_of(land, axis, me, None if axis is None else src.shape[axis])


def _remote_copies(axes, scatter, in_refs, land_refs, send_sems, recv_sems):
    me, peers = _peers()
    out = []
    for a, (axis, src, land) in enumerate(zip(axes, in_refs, land_refs)):
        for k, pos, idx in peers:
            s, d = _copy_ends(axis, scatter, src, land, me, idx)
            out.append(pltpu.make_async_remote_copy(
                src_ref=s, dst_ref=d, send_sem=send_sems.at[a * (N_DEV - 1) + k],
                recv_sem=recv_sems.at[a * (N_DEV - 1) + k],
                device_id=pos, device_id_type=MESH))
    return out


def _exchange(name, arrays, axes, scatter):
    n = len(arrays)

    def body(*refs):
        copies = _all_copies(axes, scatter, refs[:n], refs[n:2 * n], *refs[2 * n:])
        for cp in copies:
            cp.start()
        for cp in copies:
            cp.wait()

    any_spec = pl.BlockSpec(memory_space=pl.ANY)
    return pl.pallas_call(
        body, name=name, in_specs=[any_spec] * n, out_specs=[any_spec] * n,
        out_shape=[jax.ShapeDtypeStruct(_land_shape(a.shape, ax, scatter), a.dtype) for a, ax in zip(arrays, axes)],
        scratch_shapes=_exchange_sems(n),
        compiler_params=pltpu.CompilerParams(has_side_effects=True))(*arrays)


def _all_copies(axes, scatter, in_refs, land_refs, send_sems, recv_sems, local_sems):
    me, _ = _peers()
    copies = _remote_copies(axes, scatter, in_refs, land_refs, send_sems, recv_sems)
    for a, (axis, src, land) in enumerate(zip(axes, in_refs, land_refs)):
        s, d = _copy_ends(axis, scatter, src, land, me, me)
        copies.append(pltpu.make_async_copy(s, d, local_sems.at[a]))
    return copies


def _exchange_sems(n):
    return [pltpu.SemaphoreType.DMA((n * (N_DEV - 1),)), pltpu.SemaphoreType.DMA((n * (N_DEV - 1),)),
            pltpu.SemaphoreType.DMA((n,))]


def _hosted_call(body, ride, *, name, grid, in_specs, out_specs, out_shape, sem, args, scratch_shapes=()):
    if ride is None:
        return pl.pallas_call(body, name=name, grid=grid, in_specs=in_specs, out_specs=out_specs, out_shape=out_shape,
                              scratch_shapes=list(scratch_shapes), compiler_params=_params(sem))(*args), None
    arrays, axes, scatter = ride
    n, n_in, n_out, n_scr = len(arrays), len(in_specs), len(out_specs), len(scratch_shapes)

    def hosted(*refs):
        main_in, ride_in = refs[:n_in], refs[n_in:n_in + n]
        o0 = n_in + n
        main_out, lands = refs[o0:o0 + n_out], refs[o0 + n_out:o0 + n_out + n]
        s0 = o0 + n_out + n
        main_scr, sems = refs[s0:s0 + n_scr], refs[s0 + n_scr:]
        ids = [pl.program_id(i) for i in range(len(grid))]
        first = functools.reduce(jnp.logical_and, [i == 0 for i in ids])
        last = functools.reduce(jnp.logical_and, [i == g - 1 for i, g in zip(ids, grid)])

        @pl.when(first)
        def _():
            for cp in _all_copies(axes, scatter, ride_in, lands, *sems):
                cp.start()

        body(*main_in, *main_out, *main_scr)

        @pl.when(last)
        def _():
            for cp in _all_copies(axes, scatter, ride_in, lands, *sems):
                cp.wait()

    any_spec = pl.BlockSpec(memory_space=pl.ANY)
    res = pl.pallas_call(
        hosted, name=name, grid=grid, in_specs=list(in_specs) + [any_spec] * n,
        out_specs=list(out_specs) + [any_spec] * n,
        out_shape=list(out_shape) + [jax.ShapeDtypeStruct(_land_shape(a.shape, ax, scatter), a.dtype)
                                     for a, ax in zip(arrays, axes)],
        scratch_shapes=list(scratch_shapes) + _exchange_sems(n),
        compiler_params=pltpu.CompilerParams(dimension_semantics=("arbitrary",) * len(grid),
                                             vmem_limit_bytes=VMEM_LIMIT, has_side_effects=True))(*args, *arrays)
    return res[:n_out], res[n_out:]


def _row(vec, width=None):
    vec = vec.reshape(1, -1)
    if width is not None and vec.shape[1] < width:
        vec = jnp.pad(vec, ((0, 0), (0, width - vec.shape[1])))
    return vec


def _riding(rides, lands, own=None):
    def run(key, fn, *args, **kwargs):
        if rides is None or key not in rides:
            return fn(*args, **kwargs)
        arrays, axes, scatter = rides[key]
        arrays = [own[a] if isinstance(a, str) else a for a in arrays]
        res, lands[key] = fn(*args, ride=(arrays, axes, scatter), **kwargs)
        return res
    return run


def _layer_fwd(l, x, sp, bp, rides=None):
    n_heads = x.shape[1] // (2 * HEAD_DIM)
    lands = {}
    run = _riding(rides, lands)
    proj, h = run("in_proj", _mm_nn, f"l{l}_in_proj", [x], [_row(sp["attn_norm"])], bp["w_in"])
    qg, kg, bf = _row(sp["q_norm"]), _row(sp["k_norm"]), _row(sp["b_forget"], LANES)
    pa, c = _prep_fwd(f"l{l}_prep", proj, qg, kg, bf, n_heads)
    ct = c[:, :n_heads].T
    o_a, lse = _fox_fwd(f"l{l}_fox", pa, c, ct, n_heads)
    o_b, tot, kst = _sb_fwd(f"l{l}_sb", pa, n_heads)
    gfox, gsb = _row(sp["out_norm_fox"]), _row(sp["out_norm_sb"])
    x1, merged = run("out_proj", _mm_nn, f"l{l}_out_proj", [o_a, o_b], [gfox, gsb], bp["w_out"], resid=x)
    u, h2 = run("up_proj", _mm_nn, f"l{l}_up_proj", [x1], [_row(sp["ffn_norm"])], bp["w_up"])
    cb = _row(sp["conv_b"])
    g = _conv_gate_fwd(f"l{l}_conv_gate", u, bp["conv_w"], cb)
    x2 = run("down_proj", _mm_nn, f"l{l}_down_proj", [g], [], bp["w_down"], resid=x1, tn_cap=512)
    saved = dict(x=x, h=h, proj=proj, pa=pa, c=c, ct=ct, o_a=o_a, lse=lse, o_b=o_b, tot=tot, kst=kst, merged=merged, x1=x1,
                 h2=h2, u=u, g=g, qg=qg, kg=kg, bf=bf, gfox=gfox, gsb=gsb, cb=cb)
    return x2, saved, lands


def _layer_bwd(l, dx, sp, bp, sv, rides=None):
    n_heads = dx.shape[1] // (2 * HEAD_DIM)
    gr, lands = {}, {}
    run = _riding(rides, lands, gr)
    dg = run("d_down_act", _mm_nt, f"l{l}_d_down_act", dx, bp["w_down"])
    gr["w_down"] = _mm_tn(f"l{l}_d_w_down", sv["g"], dx)
    duc, dcw, dcb = _conv_gate_bwd(f"l{l}_d_conv_gate", sv["u"], dg, bp["conv_w"], sv["cb"])
    gr["conv_w"] = dcw.transpose(1, 0, 2).reshape(3, -1)
    gr["conv_b"] = dcb.reshape(-1)
    du = _conv_t(f"l{l}_d_conv", duc, bp["conv_w"])
    gr["w_up"] = run("d_w_up", _mm_tn, f"l{l}_d_w_up", sv["h2"], du)
    dx1, dffn = run("d_up_act", _mm_nt, f"l{l}_d_up_act", du, bp["w_up"], "rms_bwd", [sv["x1"]],
                    [_row(sp["ffn_norm"])], dres=dx)
    gr["ffn_norm"] = dffn.reshape(-1)
    gr["w_out"] = _mm_tn(f"l{l}_d_w_out", sv["merged"], dx1)
    do_a, do_b, dgfox, dgsb = _mm_nt(f"l{l}_d_out_act", dx1, bp["w_out"], "rms2_bwd",
                                     [sv["o_a"], sv["o_b"]], [sv["gfox"], sv["gsb"]])
    gr["out_norm_fox"], gr["out_norm_sb"] = dgfox.reshape(-1), dgsb.reshape(-1)
    dq_a, dk_a, dv_a, dct, dcq = _fox_bwd(f"l{l}_d_fox", sv["pa"], sv["c"], sv["ct"], sv["o_a"], do_a, sv["lse"],
                                          n_heads)
    d_sb = _sb_bwd(f"l{l}_d_sb", sv["pa"], do_b, sv["tot"], sv["kst"], n_heads)
    dc = jnp.pad((dct + dcq[:, :, 0]).T, ((0, 0), (0, LANES - n_heads)))
    dproj, dqg, dkg, dbf = _prep_bwd(f"l{l}_d_prep", sv["proj"], sv["qg"], sv["kg"], sv["bf"],
                                     (dq_a, dk_a, dv_a), d_sb, dc, n_heads)
    gr["q_norm"], gr["k_norm"], gr["b_forget"] = dqg.reshape(-1), dkg.reshape(-1), dbf.reshape(-1)[:n_heads]
    gr["w_in"] = run("d_w_in", _mm_tn, f"l{l}_d_w_in", sv["h"], dproj)
    dx0, dattn = run("d_in_act", _mm_nt, f"l{l}_d_in_act", dproj, bp["w_in"], "rms_bwd", [sv["x"]],
                     [_row(sp["attn_norm"])], dres=dx1)
    gr["attn_norm"] = dattn.reshape(-1)
    return dx0, gr, lands


def _local_step(x, tgt, small, big):
    n_layers = len(big)
    saved = []
    for l in range(n_layers):
        x, sv, _ = _layer_fwd(l, x, small[l], big[l])
        saved.append(sv)
    dx, loss_part = _loss_head("loss_head", x, tgt)
    grads = [None] * n_layers
    for l in reversed(range(n_layers)):
        dx, grads[l], _ = _layer_bwd(l, dx, small[l], big[l], saved[l])
    return loss_part, dx, grads


def _w_in_to_internal(w, n_heads):
    w3 = 3 * n_heads * HEAD_DIM
    pad = jnp.zeros(w.shape[:-1] + (LANES - n_heads,), w.dtype)
    return jnp.concatenate([w[..., :w3], w[..., w3 + n_heads:], w[..., w3:w3 + n_heads], pad], axis=-1)


def _w_in_from_internal(w, n_heads):
    w3 = 3 * n_heads * HEAD_DIM
    return jnp.concatenate([w[..., :w3], w[..., 2 * w3:2 * w3 + n_heads], w[..., w3:2 * w3]], axis=-1)


SMALL = ("attn_norm", "b_forget", "q_norm", "k_norm", "out_norm_fox", "out_norm_sb", "ffn_norm", "conv_b")
BIG = ("w_in", "w_out", "w_up", "w_down")
WEIGHTS = ("attn_norm", "w_in", "b_forget", "q_norm", "k_norm", "out_norm_fox", "out_norm_sb", "w_out", "ffn_norm",
           "w_up", "conv_w", "conv_b", "w_down")


def kernel(x, attn_norm, w_in, b_forget, q_norm, k_norm, out_norm_fox, out_norm_sb, w_out, ffn_norm, w_up, conv_w, conv_b, w_down, loss_target, m_attn_norm, m_w_in, m_b_forget, m_q_norm, m_k_norm, m_out_norm_fox, m_out_norm_sb, m_w_out, m_ffn_norm, m_w_up, m_conv_w, m_conv_b, m_w_down, v_attn_norm, v_w_in, v_b_forget, v_q_norm, v_k_norm, v_out_norm_fox, v_out_norm_sb, v_w_out, v_ffn_norm, v_w_up, v_conv_w, v_conv_b, v_w_down):
    w = dict(attn_norm=attn_norm, w_in=w_in, b_forget=b_forget, q_norm=q_norm, k_norm=k_norm,
             out_norm_fox=out_norm_fox, out_norm_sb=out_norm_sb, w_out=w_out, ffn_norm=ffn_norm, w_up=w_up,
             conv_w=conv_w, conv_b=conv_b, w_down=w_down)
    mom = dict(attn_norm=m_attn_norm, w_in=m_w_in, b_forget=m_b_forget, q_norm=m_q_norm, k_norm=m_k_norm,
               out_norm_fox=m_out_norm_fox, out_norm_sb=m_out_norm_sb, w_out=m_w_out, ffn_norm=m_ffn_norm,
               w_up=m_w_up, conv_w=m_conv_w, conv_b=m_conv_b, w_down=m_w_down)
    var = dict(attn_norm=v_attn_norm, w_in=v_w_in, b_forget=v_b_forget, q_norm=v_q_norm, k_norm=v_k_norm,
               out_norm_fox=v_out_norm_fox, out_norm_sb=v_out_norm_sb, w_out=v_w_out, ffn_norm=v_ffn_norm,
               w_up=v_w_up, conv_w=v_conv_w, conv_b=v_conv_b, w_down=v_w_down)
    n_layers, d = attn_norm.shape
    n_heads = d // (2 * HEAD_DIM)
    me = 4 * lax.axis_index("x") + 2 * lax.axis_index("y") + lax.axis_index("c")

    shard = {k: w[k].astype(BF16) for k in BIG}
    axis_of = dict(w_in=None, w_out=0, w_up=1, w_down=0)
    host_fwd = dict(w_in="in_proj", w_out="out_proj", w_up="up_proj", w_down="down_proj")
    host_bwd = dict(w_in="d_up_act", w_out="d_down_act", w_up="d_w_up", w_down="d_in_act")
    whole_w_in = lambda g: _w_in_to_internal(g.transpose(1, 0, 2).reshape(d, -1), n_heads)
    g_in, g_out, g_up, g_down, full_cw = _exchange(
        "gather_l0", [shard[k][0] for k in BIG] + [w["conv_w"]], [axis_of[k] for k in BIG] + [2], scatter=False)
    small = [{k: w[k][l] for k in SMALL} for l in range(n_layers)]
    big = [dict(w_in=whole_w_in(g_in), w_out=g_out, w_up=g_up, w_down=g_down, conv_w=full_cw[0])]

    saved = [None] * n_layers
    act = x[0]
    for l in range(n_layers):
        rides = None
        if l + 1 < n_layers:
            rides = {host_fwd[k]: ([shard[k][l + 1]], [axis_of[k]], False) for k in BIG}
        act, saved[l], lands = _layer_fwd(l, act, small[l], big[l], rides)
        if l + 1 < n_layers:
            nxt = {k: lands[host_fwd[k]][0] for k in BIG}
            big.append(dict(nxt, w_in=whole_w_in(nxt["w_in"]), conv_w=full_cw[l + 1]))
    dx, loss_part = _loss_head("loss_head", act, loss_target[0])
    grads, parts, sends = [None] * n_layers, [None] * n_layers, None
    for l in reversed(range(n_layers)):
        rides = {}
        if sends is not None:
            rides = {host_bwd[k]: ([sends[k]], [axis_of[k]], True) for k in BIG}
        if l == 0:
            for host, k in (("d_up_act", "w_down"), ("d_in_act", "w_out"), ("d_w_in", "w_up")):
                arrays, axes, _ = rides.get(host, ([], [], True))
                rides[host] = (arrays + [k], axes + [axis_of[k]], True)
        dx, grads[l], lands = _layer_bwd(l, dx, small[l], big[l], saved[l], rides)
        if sends is not None:
            parts[l + 1] = [lands[host_bwd[k]][0] for k in BIG]
        g_in = _w_in_from_internal(grads[l]["w_in"], n_heads)
        sends = dict(grads[l], w_in=g_in.reshape(d, N_DEV, -1).transpose(1, 0, 2))
    grad_x = dx
    (p_in,) = _exchange("scatter_l0_w_in", [sends["w_in"]], [None], scatter=True)
    parts[0] = [p_in, lands["d_in_act"][-1], lands["d_w_in"][-1], lands["d_up_act"][-1]]

    stack = lambda k: jnp.stack([grads[l][k] for l in range(n_layers)])
    small_names = SMALL + ("conv_w",)
    flat = jnp.concatenate([loss_part.reshape(-1)] + [
        jnp.pad(stack(k).reshape(-1), (0, (-stack(k).size) % LANES)) for k in small_names])
    flat = flat.reshape(-1, LANES)
    (all_small,) = _exchange("gather_small_grads", [flat], [None], scatter=False)

    out = {}

    def update(name, parts_list):
        rc = (-1, parts_list[0].shape[-1])
        res = _adamw("adamw_" + name, parts_list, w[name].reshape(rc), mom[name].reshape(rc), var[name].reshape(rc))
        out[name] = [r.reshape(w[name].shape) for r in res]

    for n, name in enumerate(BIG):
        update(name, [parts[l][n] for l in range(n_layers)])
    n_rows = flat.shape[0]
    w_flat, m_flat, v_flat = [], [], []
    for src, dst in ((w, w_flat), (mom, m_flat), (var, v_flat)):
        dst.append(jnp.zeros((LANES,), F32))
        for k in small_names:
            a = src[k]
            if k == "conv_w":
                a = jnp.zeros((n_layers, 3, conv_w.shape[2] * N_DEV), F32)
            dst.append(jnp.pad(a.reshape(-1), (0, (-a.size) % LANES)))
    pack = lambda parts: jnp.concatenate(parts).reshape(n_rows, LANES)
    res = _adamw("adamw_small", [all_small], pack(w_flat), pack(m_flat), pack(v_flat))
    res = [r.reshape(-1) for r in res]
    loss = res[0][0]
    off = LANES
    g_cw_full = None
    for k in small_names:
        size = n_layers * 3 * conv_w.shape[2] * N_DEV if k == "conv_w" else w[k].size
        if k == "conv_w":
            g_cw_full = res[0][off:off + size].reshape(n_layers, 3, -1)
        else:
            out[k] = [r[off:off + size].reshape(w[k].shape) for r in res]
        off += size + (-size) % LANES
    c_loc = conv_w.shape[2]
    g_cw_mine = lax.dynamic_slice_in_dim(g_cw_full, me * c_loc, c_loc, axis=2)
    update("conv_w", [g_cw_mine.reshape(1, n_layers * 3, c_loc)])

    outs = [loss, grad_x[None]]
    for n in range(4):
        outs += [out[k][n] for k in WEIGHTS]
    return tuple(outs)
```

```python
import functools

import jax
import jax.numpy as jnp
from jax import lax
from jax.experimental import pallas as pl
from jax.experimental.pallas import tpu as pltpu

F32 = jnp.float32
BF16 = jnp.bfloat16
HEAD_DIM = 128
QK_SCALE = HEAD_DIM ** -0.5
SKIP = 110.0
BOUND_SLACK = 1.0 + 2.0 ** -6
LANES = 128
EPS = 1e-6
N_DEV = 8
ADAM_LR = 0.001
ADAM_B1 = 0.9
ADAM_B2 = 0.999
ADAM_EPS = 1e-08
ADAM_WD = 0.01
ADAM_STEP = 10
VMEM_LIMIT = 56 * 1024 * 1024
ADAMW_VMEM = 24 * 1024 * 1024
MESH = pl.DeviceIdType.MESH

NT_DIMS = (((1,), (1,)), ((), ()))
TN_DIMS = (((0,), (0,)), ((), ()))


def _tile(n, cap, mult=LANES):
    t = (min(cap, n) // mult) * mult
    while t >= mult:
        if n % t == 0:
            return t
        t -= mult
    return n


def _params(sem, vmem=VMEM_LIMIT):
    return pltpu.CompilerParams(dimension_semantics=sem, vmem_limit_bytes=vmem)


def _split_dot(x, u, n_split, x_left=True):
    acc = None
    rest = x
    for s in range(n_split):
        piece = rest.astype(BF16)
        if s + 1 < n_split:
            rest = rest - piece.astype(F32)
        d = (jnp.dot(piece, u, preferred_element_type=F32) if x_left
             else jnp.dot(u, piece, preferred_element_type=F32))
        acc = d if acc is None else acc + d
    return acc


def _tri(n, kind):
    r = lax.broadcasted_iota(jnp.int32, (n, n), 0)
    c = lax.broadcasted_iota(jnp.int32, (n, n), 1)
    return jnp.where(r >= c if kind == "ge" else r <= c, 1.0, 0.0).astype(BF16)


def _mm_nn(name, a_list, g_list, w, resid=None, tm=1024, tn_cap=1024, ride=None):
    s_len = a_list[0].shape[0]
    k_dim, n_dim = w.shape
    tm = _tile(s_len, tm, 8)
    tn = _tile(n_dim, tn_cap)
    normed = bool(g_list)

    def body(*refs):
        refs = list(refs)
        a_refs = [refs.pop(0) for _ in a_list]
        g_refs = [refs.pop(0) for _ in g_list]
        w_ref = refs.pop(0)
        r_ref = refs.pop(0) if resid is not None else None
        o_ref = refs.pop(0)
        if normed:
            h_ref = refs.pop(0)

            @pl.when(pl.program_id(1) == 0)
            def _():
                off = 0
                for a_ref, g_ref in zip(a_refs, g_refs):
                    xv = a_ref[...]
                    kk = xv.shape[1]
                    r = lax.rsqrt(jnp.mean(xv * xv, axis=1, keepdims=True) + EPS)
                    h_ref[:, off:off + kk] = (xv * r * g_ref[...]).astype(BF16)
                    off += kk

            a = h_ref[...]
        else:
            a = a_refs[0][...]
        acc = jnp.dot(a, w_ref[...], preferred_element_type=F32)
        if r_ref is not None:
            acc = acc + r_ref[...]
        o_ref[...] = acc

    in_specs = [pl.BlockSpec((tm, a.shape[1]), lambda i, j: (i, 0)) for a in a_list]
    in_specs += [pl.BlockSpec((1, g.shape[1]), lambda i, j: (0, 0)) for g in g_list]
    in_specs += [pl.BlockSpec((k_dim, tn), lambda i, j: (0, j))]
    args = list(a_list) + list(g_list) + [w]
    if resid is not None:
        in_specs.append(pl.BlockSpec((tm, tn), lambda i, j: (i, j)))
        args.append(resid)
    out_shape = [jax.ShapeDtypeStruct((s_len, n_dim), F32)]
    out_specs = [pl.BlockSpec((tm, tn), lambda i, j: (i, j))]
    if normed:
        out_shape.append(jax.ShapeDtypeStruct((s_len, k_dim), BF16))
        out_specs.append(pl.BlockSpec((tm, k_dim), lambda i, j: (i, 0)))
    res, lands = _hosted_call(body, ride, name=name, grid=(s_len // tm, n_dim // tn), in_specs=in_specs,
                              out_specs=out_specs, out_shape=out_shape, sem=("parallel", "arbitrary"), args=args)
    res = res if normed else res[0]
    return res if ride is None else (res, lands)


def _rms_bwd(dh, xv, gv, r=None):
    if r is None:
        r = lax.rsqrt(jnp.mean(xv * xv, axis=1, keepdims=True) + EPS)
    xhat = xv * r
    dxh = dh * gv
    dx = r * (dxh - xhat * jnp.mean(dxh * xhat, axis=1, keepdims=True))
    return dx, dh * xhat


def _mm_nt(name, da, w, mode="plain", xs=(), gs=(), dres=None, tm=512, tk_cap=1024, ride=None):
    s_len, kc = da.shape
    n_out = w.shape[0]
    tm = _tile(s_len, tm, 8)
    tk = _tile(kc, tk_cap)
    nk = kc // tk

    def body(*refs):
        refs = list(refs)
        da_ref, w_ref = refs.pop(0), refs.pop(0)
        x_refs = [refs.pop(0) for _ in xs]
        g_refs = [refs.pop(0) for _ in gs]
        dres_ref = refs.pop(0) if dres is not None else None
        acc_ref = refs[0] if mode == "plain" else refs.pop()
        out_refs = refs
        i, k = pl.program_id(0), pl.program_id(1)

        @pl.when(k == 0)
        def _():
            acc_ref[...] = jnp.zeros_like(acc_ref)

        acc_ref[...] += lax.dot_general(da_ref[...].astype(BF16), w_ref[...], NT_DIMS,
                                        preferred_element_type=F32)

        if mode == "plain":
            return

        @pl.when(k == nk - 1)
        def _():
            n_x = len(xs)
            dx_refs, dg_refs = out_refs[:n_x], out_refs[n_x:]

            @pl.when(i == 0)
            def _():
                for dg_ref in dg_refs:
                    dg_ref[...] = jnp.zeros_like(dg_ref)

            off = 0
            for x_ref, g_ref, dx_ref, dg_ref in zip(x_refs, g_refs, dx_refs, dg_refs):
                kk = x_ref.shape[1]
                dx, dgp = _rms_bwd(acc_ref[:, off:off + kk], x_ref[...], g_ref[...])
                if dres_ref is not None:
                    dx = dx + dres_ref[...]
                dx_ref[...] = dx
                dg_ref[...] += jnp.sum(dgp, axis=0, keepdims=True)
                off += kk

    in_specs = [pl.BlockSpec((tm, tk), lambda i, k: (i, k)), pl.BlockSpec((n_out, tk), lambda i, k: (0, k))]
    in_specs += [pl.BlockSpec((tm, x.shape[1]), lambda i, k: (i, 0)) for x in xs]
    in_specs += [pl.BlockSpec((1, g.shape[1]), lambda i, k: (0, 0)) for g in gs]
    args = [da, w] + list(xs) + list(gs)
    if dres is not None:
        in_specs.append(pl.BlockSpec((tm, n_out), lambda i, k: (i, 0)))
        args.append(dres)
    if mode == "plain":
        out_shape = [jax.ShapeDtypeStruct((s_len, n_out), F32)]
        out_specs = [pl.BlockSpec((tm, n_out), lambda i, k: (i, 0))]
    else:
        out_shape = [jax.ShapeDtypeStruct((s_len, x.shape[1]), F32) for x in xs]
        out_specs = [pl.BlockSpec((tm, x.shape[1]), lambda i, k: (i, 0)) for x in xs]
        out_shape += [jax.ShapeDtypeStruct((1, x.shape[1]), F32) for x in xs]
        out_specs += [pl.BlockSpec((1, x.shape[1]), lambda i, k: (0, 0)) for x in xs]
    res, lands = _hosted_call(body, ride, name=name, grid=(s_len // tm, nk), in_specs=in_specs, out_specs=out_specs,
                              out_shape=out_shape, sem=("arbitrary", "arbitrary"), args=args,
                              scratch_shapes=[] if mode == "plain" else [pltpu.VMEM((tm, n_out), F32)])
    res = res[0] if mode == "plain" else res
    return res if ride is None else (res, lands)


def _mm_tn(name, a, b, tk_cap=1024, tn_cap=2048, tm=1024, ride=None):
    s_len, k_dim = a.shape
    n_dim = b.shape[1]
    tk = _tile(k_dim, tk_cap)
    tn = _tile(n_dim, tn_cap)
    tm = _tile(s_len, tm, 8)
    nm = s_len // tm

    def body(a_ref, b_ref, o_ref, acc_ref):
        m = pl.program_id(2)

        @pl.when(m == 0)
        def _():
            acc_ref[...] = jnp.zeros_like(acc_ref)

        acc_ref[...] += lax.dot_general(a_ref[...].astype(BF16), b_ref[...].astype(BF16), TN_DIMS,
                                        preferred_element_type=F32)

        @pl.when(m == nm - 1)
        def _():
            o_ref[...] = acc_ref[...].astype(BF16)

    res, lands = _hosted_call(
        body, ride, name=name, grid=(k_dim // tk, n_dim // tn, nm),
        in_specs=[pl.BlockSpec((tm, tk), lambda i, j, m: (m, i)), pl.BlockSpec((tm, tn), lambda i, j, m: (m, j))],
        out_specs=[pl.BlockSpec((tk, tn), lambda i, j, m: (i, j))],
        out_shape=[jax.ShapeDtypeStruct((k_dim, n_dim), BF16)],
        scratch_shapes=[pltpu.VMEM((tk, tn), F32)], sem=("parallel", "parallel", "arbitrary"), args=[a, b])
    return res[0] if ride is None else (res[0], lands)


def _neg_softplus(z):
    e = jnp.exp(-jnp.abs(z))
    return -(jnp.maximum(z, 0.0) + jnp.log(1.0 + e)), e


def _prep_fwd(name, proj, qg, kg, bf, n_heads, tm=256):
    s_len, n_p = proj.shape
    w_dim = n_heads * HEAD_DIM
    tm = _tile(s_len, tm, 8)

    def body(p_ref, qg_ref, kg_ref, bf_ref, pa_ref, c_ref, carry_ref):
        @pl.when(pl.program_id(0) == 0)
        def _():
            carry_ref[...] = jnp.zeros_like(carry_ref)

        for base, g_ref, mul in ((0, qg_ref, QK_SCALE), (w_dim, kg_ref, None)):
            for hh in range(n_heads):
                sl = slice(base + hh * HEAD_DIM, base + (hh + 1) * HEAD_DIM)
                xv = p_ref[:, sl]
                r = lax.rsqrt(jnp.mean(xv * xv, axis=1, keepdims=True) + EPS)
                y = xv * r * g_ref[...]
                pa_ref[:, sl] = (y if mul is None else y * mul).astype(BF16)
        pa_ref[:, 2 * w_dim:3 * w_dim] = p_ref[:, 2 * w_dim:3 * w_dim].astype(BF16)
        pa_ref[:, 3 * w_dim:4 * w_dim] = (p_ref[:, 3 * w_dim:4 * w_dim] * QK_SCALE).astype(BF16)
        pa_ref[:, 4 * w_dim:] = p_ref[:, 4 * w_dim:6 * w_dim].astype(BF16)
        f = p_ref[:, 6 * w_dim:] + bf_ref[...]
        lf, _ = _neg_softplus(-f)
        lane = lax.broadcasted_iota(jnp.int32, lf.shape, 1)
        lf = jnp.where(lane < n_heads, lf, 0.0)
        cb = _split_dot(lf, _tri(tm, "ge"), 3, x_left=False) + carry_ref[...]
        c_ref[...] = cb
        carry_ref[...] = cb[tm - 1:tm, :]

    return pl.pallas_call(
        body, name=name, grid=(s_len // tm,),
        in_specs=[pl.BlockSpec((tm, n_p), lambda i: (i, 0))] + [pl.BlockSpec((1, LANES), lambda i: (0, 0))] * 3,
        out_specs=[pl.BlockSpec((tm, 6 * w_dim), lambda i: (i, 0)), pl.BlockSpec((tm, LANES), lambda i: (i, 0))],
        out_shape=[jax.ShapeDtypeStruct((s_len, 6 * w_dim), BF16), jax.ShapeDtypeStruct((s_len, LANES), F32)],
        scratch_shapes=[pltpu.VMEM((1, LANES), F32)],
        compiler_params=_params(("arbitrary",)))(proj, qg, kg, bf)


def _prep_bwd(name, proj, qg, kg, bf, d_fox, d_sb, dc, n_heads, tm=256):
    s_len, n_p = proj.shape
    w_dim = n_heads * HEAD_DIM
    tm = _tile(s_len, tm, 8)
    nb = s_len // tm

    def body(p_ref, qg_ref, kg_ref, bf_ref, dqa_ref, dka_ref, dva_ref, dqb_ref, dkb_ref, dvb_ref, dc_ref,
             dp_ref, dqg_ref, dkg_ref, dbf_ref, carry_ref):
        @pl.when(pl.program_id(0) == 0)
        def _():
            for ref in (carry_ref, dqg_ref, dkg_ref, dbf_ref):
                ref[...] = jnp.zeros_like(ref)

        for base, g_ref, d_ref, dg_ref in ((0, qg_ref, dqa_ref, dqg_ref), (w_dim, kg_ref, dka_ref, dkg_ref)):
            dg = jnp.zeros((1, HEAD_DIM), F32)
            for hh in range(n_heads):
                sl = slice(base + hh * HEAD_DIM, base + (hh + 1) * HEAD_DIM)
                dx, dgp = _rms_bwd(d_ref[:, hh * HEAD_DIM:(hh + 1) * HEAD_DIM], p_ref[:, sl], g_ref[...])
                dp_ref[:, sl] = dx.astype(BF16)
                dg = dg + jnp.sum(dgp, axis=0, keepdims=True)
            dg_ref[...] += dg
        for n, d_ref in enumerate((dva_ref, dqb_ref, dkb_ref, dvb_ref)):
            dp_ref[:, (2 + n) * w_dim:(3 + n) * w_dim] = d_ref[...].astype(BF16)
        dlf = _split_dot(dc_ref[...], _tri(tm, "le"), 3, x_left=False) + carry_ref[...]
        carry_ref[...] = dlf[0:1, :]
        f = p_ref[:, 6 * w_dim:] + bf_ref[...]
        e = jnp.exp(-jnp.abs(f))
        sig_neg = jnp.where(f >= 0, e, 1.0) / (1.0 + e)
        lane = lax.broadcasted_iota(jnp.int32, f.shape, 1)
        df = jnp.where(lane < n_heads, dlf * sig_neg, 0.0)
        dp_ref[:, 6 * w_dim:] = df.astype(BF16)
        dbf_ref[...] += jnp.sum(df, axis=0, keepdims=True)

    rev = lambda i: (nb - 1 - i, 0)
    vec = pl.BlockSpec((1, LANES), lambda i: (0, 0))
    return pl.pallas_call(
        body, name=name, grid=(nb,),
        in_specs=[pl.BlockSpec((tm, n_p), rev), vec, vec, vec] + [pl.BlockSpec((tm, w_dim), rev)] * 6
        + [pl.BlockSpec((tm, LANES), rev)],
        out_specs=[pl.BlockSpec((tm, n_p), rev), vec, vec, vec],
        out_shape=[jax.ShapeDtypeStruct((s_len, n_p), BF16)] + [jax.ShapeDtypeStruct((1, LANES), F32)] * 3,
        scratch_shapes=[pltpu.VMEM((1, LANES), F32)],
        compiler_params=_params(("arbitrary",)))(proj, qg, kg, bf, *d_fox, *d_sb, dc)


def _head_col(c_blk, h):
    lane = lax.broadcasted_iota(jnp.int32, c_blk.shape, 1)
    return jnp.sum(jnp.where(lane == h, c_blk, 0.0), axis=1, keepdims=True)


def _key_norm_max(k_ref):
    kf = k_ref[...].astype(F32)
    return jnp.sqrt(jnp.max(jnp.sum(kf * kf, axis=1, keepdims=True), axis=0, keepdims=True))


def _logit_bound(q, kmax):
    qf = q.astype(F32)
    return jnp.sqrt(jnp.sum(qf * qf, axis=1, keepdims=True)) * kmax * BOUND_SLACK


def _c_block_end(ct_ref, h, kb, tk):
    return jnp.min(ct_ref[pl.ds(h, 1), pl.ds(pl.multiple_of(kb * tk, tk), tk)])


def _fox_fwd(name, pa, c, ct, n_heads, tq=1024):
    s_len = pa.shape[0]
    tq = _tile(s_len, tq, LANES)
    hp = ct.shape[0]

    def body(q_ref, k_ref, v_ref, c_ref, ct_ref, o_ref, lse_ref, kmax_ref):
        h, qi = pl.program_id(0), pl.program_id(1)

        @pl.when(qi == 0)
        def _():
            kmax_ref[...] = _key_norm_max(k_ref)

        q = q_ref[...]
        cq = _head_col(c_ref[...], h)
        top = _logit_bound(q, kmax_ref[...]) + cq
        row = lax.broadcasted_iota(jnp.int32, (tq, tq), 0)
        col = lax.broadcasted_iota(jnp.int32, (tq, tq), 1)

        def step(kb, carry, masked):
            m, l, acc = carry
            ks = pl.multiple_of(kb * tq, tq)
            k = k_ref[pl.ds(ks, tq), :]
            v = v_ref[pl.ds(ks, tq), :]
            ck = ct_ref[pl.ds(h, 1), pl.ds(ks, tq)]
            s = lax.dot_general(q, k, NT_DIMS, preferred_element_type=F32) + (cq - ck)
            if masked:
                s = jnp.where(col <= row, s, -jnp.inf)
            m_new = jnp.maximum(m, jnp.max(s, axis=1, keepdims=True))
            alpha = jnp.exp(m - m_new)
            p = jnp.exp(s - m_new)
            l = alpha * l + jnp.sum(p, axis=1, keepdims=True)
            acc = alpha * acc + jnp.dot(p.astype(BF16), v, preferred_element_type=F32)
            return m_new, l, acc

        def margin(kb, m):
            return jnp.max(top - m) - _c_block_end(ct_ref, h, jnp.maximum(kb, 0), tq)

        def walk(state):
            kb, _, m, l, acc = state
            m, l, acc = step(kb, (m, l, acc), False)
            return kb - 1, margin(kb - 1, m), m, l, acc

        init = (jnp.full((tq, 1), -jnp.inf, F32), jnp.zeros((tq, 1), F32), jnp.zeros((tq, HEAD_DIM), F32))
        carry = step(qi, init, True)
        state = (qi - 1, margin(qi - 1, carry[0])) + carry
        _, _, m, l, acc = lax.while_loop(lambda st: (st[0] >= 0) & (st[1] > -SKIP), walk, state)
        o_ref[...] = acc / l
        lse_ref[0] = m + jnp.log(l)

    return pl.pallas_call(
        body, name=name, grid=(n_heads, s_len // tq),
        in_specs=[pl.BlockSpec((tq, HEAD_DIM), lambda h, i: (i, h)),
                  pl.BlockSpec((s_len, HEAD_DIM), lambda h, i: (0, n_heads + h)),
                  pl.BlockSpec((s_len, HEAD_DIM), lambda h, i: (0, 2 * n_heads + h)),
                  pl.BlockSpec((tq, LANES), lambda h, i: (i, 0)),
                  pl.BlockSpec((hp, s_len), lambda h, i: (0, 0))],
        out_specs=[pl.BlockSpec((tq, HEAD_DIM), lambda h, i: (i, h)),
                   pl.BlockSpec((1, tq, 1), lambda h, i: (h, i, 0))],
        out_shape=[jax.ShapeDtypeStruct((s_len, n_heads * HEAD_DIM), F32),
                   jax.ShapeDtypeStruct((n_heads, s_len, 1), F32)],
        scratch_shapes=[pltpu.VMEM((1, 1), F32)],
        compiler_params=_params(("parallel", "arbitrary")))(pa, pa, pa, c, ct)


def _fox_bwd(name, pa, c, ct, o, do, lse, n_heads, tq=1024):
    s_len = pa.shape[0]
    tq = _tile(s_len, tq, LANES)
    hp = ct.shape[0]
    w_dim = n_heads * HEAD_DIM

    def body(q_ref, k_ref, v_ref, c_ref, ct_ref, o_ref, do_ref, lse_ref, dq_ref, dk_ref, dv_ref, dct_ref, dcq_ref,
             kmax_ref):
        h, qi = pl.program_id(0), pl.program_id(1)

        @pl.when(qi == 0)
        def _():
            dk_ref[...] = jnp.zeros_like(dk_ref)
            dv_ref[...] = jnp.zeros_like(dv_ref)
            kmax_ref[...] = _key_norm_max(k_ref)

        @pl.when((qi == 0) & (h == 0))
        def _():
            dct_ref[...] = jnp.zeros_like(dct_ref)

        q = q_ref[...]
        do32 = do_ref[...]
        dob = do32.astype(BF16)
        dsum = jnp.sum(do32 * o_ref[...], axis=1, keepdims=True)
        lse_v = lse_ref[0]
        cq = _head_col(c_ref[...], h)
        row = lax.broadcasted_iota(jnp.int32, (tq, tq), 0)
        col = lax.broadcasted_iota(jnp.int32, (tq, tq), 1)

        def step(kb, carry, masked):
            dq, dcq = carry
            ks = pl.multiple_of(kb * tq, tq)
            k = k_ref[pl.ds(ks, tq), :]
            v = v_ref[pl.ds(ks, tq), :]
            ck = ct_ref[pl.ds(h, 1), pl.ds(ks, tq)]
            s = lax.dot_general(q, k, NT_DIMS, preferred_element_type=F32) + (cq - ck)
            p = jnp.exp(s - lse_v)
            if masked:
                p = jnp.where(col <= row, p, 0.0)
            dp = lax.dot_general(dob, v, NT_DIMS, preferred_element_type=F32)
            ds = p * (dp - dsum)
            dsb = ds.astype(BF16)
            dk_ref[pl.ds(ks, tq), :] += lax.dot_general(dsb, q, TN_DIMS, preferred_element_type=F32)
            dv_ref[pl.ds(ks, tq), :] += lax.dot_general(p.astype(BF16), dob, TN_DIMS, preferred_element_type=F32)
            dct_ref[pl.ds(h, 1), pl.ds(ks, tq)] -= jnp.sum(ds, axis=0, keepdims=True)
            return dq + jnp.dot(dsb, k, preferred_element_type=F32), dcq + jnp.sum(ds, axis=1, keepdims=True)

        top = jnp.max(_logit_bound(q, kmax_ref[...]) + cq - lse_v)

        def margin(kb):
            return top - _c_block_end(ct_ref, h, jnp.maximum(kb, 0), tq)

        def walk(state):
            kb, _, dq, dcq = state
            dq, dcq = step(kb, (dq, dcq), False)
            return kb - 1, margin(kb - 1), dq, dcq

        init = (jnp.zeros((tq, HEAD_DIM), F32), jnp.zeros((tq, 1), F32))
        state = (qi - 1, margin(qi - 1)) + step(qi, init, True)
        _, _, dq, dcq = lax.while_loop(lambda st: (st[0] >= 0) & (st[1] > -SKIP), walk, state)
        dq_ref[...] = dq * QK_SCALE
        dcq_ref[0] = dcq

    blk = pl.BlockSpec((tq, HEAD_DIM), lambda h, i: (i, h))
    full = pl.BlockSpec((s_len, HEAD_DIM), lambda h, i: (0, h))
    return pl.pallas_call(
        body, name=name, grid=(n_heads, s_len // tq),
        in_specs=[blk,
                  pl.BlockSpec((s_len, HEAD_DIM), lambda h, i: (0, n_heads + h)),
                  pl.BlockSpec((s_len, HEAD_DIM), lambda h, i: (0, 2 * n_heads + h)),
                  pl.BlockSpec((tq, LANES), lambda h, i: (i, 0)),
                  pl.BlockSpec((hp, s_len), lambda h, i: (0, 0)),
                  blk, blk,
                  pl.BlockSpec((1, tq, 1), lambda h, i: (h, i, 0))],
        out_specs=[blk, full, full, pl.BlockSpec((hp, s_len), lambda h, i: (0, 0)),
                   pl.BlockSpec((1, tq, 1), lambda h, i: (h, i, 0))],
        out_shape=[jax.ShapeDtypeStruct((s_len, w_dim), F32)] * 3 + [jax.ShapeDtypeStruct((hp, s_len), F32),
                                                                     jax.ShapeDtypeStruct((n_heads, s_len, 1), F32)],
        scratch_shapes=[pltpu.VMEM((1, 1), F32)],
        compiler_params=_params(("arbitrary", "arbitrary")))(pa, pa, pa, c, ct, o, do, lse)


def _sb_fwd(name, pa, n_heads, tq=512, tk=256):
    s_len = pa.shape[0]
    tq = _tile(s_len, tq, LANES)
    tk = _tile(tq, tk, LANES)
    nsub = tq // tk

    def body(q_ref, k_ref, v_ref, o_ref, tot_ref, kst_ref, kmax_ref):
        qi = pl.program_id(1)

        @pl.when(qi == 0)
        def _():
            kmax_ref[...] = _key_norm_max(k_ref)

        q = q_ref[...]
        z_bound = _logit_bound(q, kmax_ref[...])
        u = _tri(tk, "ge")
        row = lax.broadcasted_iota(jnp.int32, (tq, tk), 0)
        col = lax.broadcasted_iota(jnp.int32, (tq, tk), 1)

        def block(ks, carry, mask_off):
            r, acc = carry
            k = k_ref[pl.ds(ks, tk), :]
            v = v_ref[pl.ds(ks, tk), :]
            z = lax.dot_general(q, k, NT_DIMS, preferred_element_type=F32)
            a, _ = _neg_softplus(z)
            if mask_off is not None:
                valid = col + mask_off < row
                a = jnp.where(valid, a, 0.0)
            rin = _split_dot(a, u, 2)
            w = jnp.exp(z + (r + rin))
            if mask_off is not None:
                w = jnp.where(valid, w, 0.0)
            acc = acc + jnp.dot(w.astype(BF16), v, preferred_element_type=F32)
            return r + rin[:, 0:1], acc

        carry = (jnp.zeros((tq, 1), F32), jnp.zeros((tq, HEAD_DIM), F32))
        q0 = pl.multiple_of(qi * tq, tq)
        for j in reversed(range(nsub)):
            carry = block(q0 + j * tk, carry, j * tk)
        def walk(state):
            kb, _, r, acc = state
            r, acc = block(pl.multiple_of(kb * tk, tk), (r, acc), None)
            return kb - 1, jnp.max(r + z_bound), r, acc

        state = (qi * nsub - 1, jnp.max(carry[0] + z_bound)) + carry
        kb, _, r, acc = lax.while_loop(lambda st: (st[0] >= 0) & (st[1] > -SKIP), walk, state)
        o_ref[...] = acc
        tot_ref[0] = r
        kst_ref[...] = jnp.full(kst_ref.shape, (kb + 1).astype(F32))

    nq = s_len // tq
    return pl.pallas_call(
        body, name=name, grid=(n_heads, nq),
        in_specs=[pl.BlockSpec((tq, HEAD_DIM), lambda h, i: (i, 3 * n_heads + h)),
                  pl.BlockSpec((s_len, HEAD_DIM), lambda h, i: (0, 4 * n_heads + h)),
                  pl.BlockSpec((s_len, HEAD_DIM), lambda h, i: (0, 5 * n_heads + h))],
        out_specs=[pl.BlockSpec((tq, HEAD_DIM), lambda h, i: (i, h)),
                   pl.BlockSpec((1, tq, 1), lambda h, i: (h, i, 0)),
                   pl.BlockSpec((1, 1, 8, LANES), lambda h, i: (h, i, 0, 0))],
        out_shape=[jax.ShapeDtypeStruct((s_len, n_heads * HEAD_DIM), F32),
                   jax.ShapeDtypeStruct((n_heads, s_len, 1), F32),
                   jax.ShapeDtypeStruct((n_heads, nq, 8, LANES), F32)],
        scratch_shapes=[pltpu.VMEM((1, 1), F32)],
        compiler_params=_params(("parallel", "arbitrary")))(pa, pa, pa)


def _sb_bwd(name, pa, do, tot, kst, n_heads, tq=512, tk=256):
    s_len = pa.shape[0]
    tq = _tile(s_len, tq, LANES)
    tk = _tile(tq, tk, LANES)
    nsub = tq // tk
    w_dim = n_heads * HEAD_DIM

    def body(q_ref, k_ref, v_ref, do_ref, tot_ref, kst_ref, dq_ref, dk_ref, dv_ref):
        qi = pl.program_id(1)

        @pl.when(qi == 0)
        def _():
            dk_ref[...] = jnp.zeros_like(dk_ref)
            dv_ref[...] = jnp.zeros_like(dv_ref)

        q = q_ref[...]
        dob = do_ref[...].astype(BF16)
        u = _tri(tk, "le")
        row = lax.broadcasted_iota(jnp.int32, (tq, tk), 0)
        col = lax.broadcasted_iota(jnp.int32, (tq, tk), 1)

        def block(ks, carry, mask_off):
            rem, cpre, dq = carry
            k = k_ref[pl.ds(ks, tk), :]
            v = v_ref[pl.ds(ks, tk), :]
            z = lax.dot_general(q, k, NT_DIMS, preferred_element_type=F32)
            a, e = _neg_softplus(z)
            if mask_off is not None:
                valid = col + mask_off < row
                a = jnp.where(valid, a, 0.0)
            pin = _split_dot(a, u, 2)
            w = jnp.exp(z + (rem - (pin - a)))
            if mask_off is not None:
                w = jnp.where(valid, w, 0.0)
            g = w * lax.dot_general(dob, v, NT_DIMS, preferred_element_type=F32)
            cin = _split_dot(g, u, 2)
            beta = jnp.where(z >= 0, 1.0, e) / (1.0 + e)
            dz = g - beta * (cpre + cin)
            if mask_off is not None:
                dz = jnp.where(valid, dz, 0.0)
            dzb = dz.astype(BF16)
            dk_ref[pl.ds(ks, tk), :] += lax.dot_general(dzb, q, TN_DIMS, preferred_element_type=F32)
            dv_ref[pl.ds(ks, tk), :] += lax.dot_general(w.astype(BF16), dob, TN_DIMS, preferred_element_type=F32)
            dq = dq + jnp.dot(dzb, k, preferred_element_type=F32)
            return rem - pin[:, tk - 1:tk], cpre + cin[:, tk - 1:tk], dq

        carry = (tot_ref[0], jnp.zeros((tq, 1), F32), jnp.zeros((tq, HEAD_DIM), F32))
        first = jnp.max(kst_ref[0, 0]).astype(jnp.int32)
        carry = lax.fori_loop(first, qi * nsub, lambda n, cr: block(pl.multiple_of(n * tk, tk), cr, None), carry)
        q0 = pl.multiple_of(qi * tq, tq)
        for j in range(nsub):
            carry = block(q0 + j * tk, carry, j * tk)
        dq_ref[...] = carry[2] * QK_SCALE

    blk = pl.BlockSpec((tq, HEAD_DIM), lambda h, i: (i, h))
    full = pl.BlockSpec((s_len, HEAD_DIM), lambda h, i: (0, h))
    return pl.pallas_call(
        body, name=name, grid=(n_heads, s_len // tq),
        in_specs=[pl.BlockSpec((tq, HEAD_DIM), lambda h, i: (i, 3 * n_heads + h)),
                  pl.BlockSpec((s_len, HEAD_DIM), lambda h, i: (0, 4 * n_heads + h)),
                  pl.BlockSpec((s_len, HEAD_DIM), lambda h, i: (0, 5 * n_heads + h)),
                  blk,
                  pl.BlockSpec((1, tq, 1), lambda h, i: (h, i, 0)),
                  pl.BlockSpec((1, 1, 8, LANES), lambda h, i: (h, i, 0, 0))],
        out_specs=[blk, full, full],
        out_shape=[jax.ShapeDtypeStruct((s_len, w_dim), F32)] * 3,
        compiler_params=_params(("arbitrary", "arbitrary")))(pa, pa, pa, do, tot, kst)


CONV_HEAD = 16
CONV_CHUNK = 32


def _stage_head(ext_ref, u_ref, halo_ref, first):
    ext_ref[0:8, :] = jnp.where(first, 0.0, halo_ref[...])
    ext_ref[8:8 + CONV_HEAD, :] = u_ref[0:CONV_HEAD, :]


def _taps(ref, r0, n):
    return ref[r0:r0 + n, :], ref[r0 - 1:r0 - 1 + n, :], ref[r0 - 2:r0 - 2 + n, :]


def _row_chunks(tm):
    return [(True, 0, CONV_HEAD)] + [(False, r0, min(CONV_CHUNK, tm - r0)) for r0 in range(CONV_HEAD, tm, CONV_CHUNK)]


def _conv3(us, w_ref, b_ref):
    return w_ref[2:3, :] * us[0] + w_ref[1:2, :] * us[1] + w_ref[0:1, :] * us[2] + b_ref[...]


def _fold8(x):
    return jnp.sum(x.reshape(x.shape[0] // 8, 8, x.shape[1]), axis=0)


def _sigmoid(x):
    return 0.5 * jnp.tanh(0.5 * x) + 0.5


def _conv_specs(tm, tc, nj, order):
    hb = tm // 8
    ij = order

    def at(f):
        return lambda *g: f(*ij(*g))

    return [pl.BlockSpec((tm, tc), at(lambda i, j: (i, j))),
            pl.BlockSpec((tm, tc), at(lambda i, j: (i, j + nj))),
            pl.BlockSpec((8, tc), at(lambda i, j: (jnp.maximum(i * hb - 1, 0), j))),
            pl.BlockSpec((8, tc), at(lambda i, j: (jnp.maximum(i * hb - 1, 0), j + nj))),
            pl.BlockSpec((3, tc), at(lambda i, j: (0, j))),
            pl.BlockSpec((3, tc), at(lambda i, j: (0, j + nj))),
            pl.BlockSpec((1, tc), at(lambda i, j: (0, j))),
            pl.BlockSpec((1, tc), at(lambda i, j: (0, j + nj)))]


def _conv_gate_fwd(name, u, cw, cb, tm=1024, tc=512):
    s_len, f2 = u.shape
    f_dim = f2 // 2
    tm = _tile(s_len, tm, 8)
    tc = _tile(f_dim, tc)
    nj = f_dim // tc

    def body(ug_ref, uv_ref, hg_ref, hv_ref, wg_ref, wv_ref, bg_ref, bv_ref, g_ref, eg_ref, ev_ref):
        first = pl.program_id(0) == 0
        _stage_head(eg_ref, ug_ref, hg_ref, first)
        _stage_head(ev_ref, uv_ref, hv_ref, first)
        for head, r0, n in _row_chunks(tm):
            gc = _conv3(_taps(eg_ref, 8, n) if head else _taps(ug_ref, r0, n), wg_ref, bg_ref)
            vc = _conv3(_taps(ev_ref, 8, n) if head else _taps(uv_ref, r0, n), wv_ref, bv_ref)
            g_ref[r0:r0 + n, :] = (gc * _sigmoid(gc) * vc).astype(BF16)

    return pl.pallas_call(
        body, name=name, grid=(s_len // tm, nj),
        in_specs=_conv_specs(tm, tc, nj, lambda i, j: (i, j)),
        out_specs=pl.BlockSpec((tm, tc), lambda i, j: (i, j)),
        out_shape=jax.ShapeDtypeStruct((s_len, f_dim), BF16),
        scratch_shapes=[pltpu.VMEM((8 + CONV_HEAD, tc), F32)] * 2,
        compiler_params=_params(("parallel", "parallel")))(u, u, u, u, cw, cw, cb, cb)


def _conv_gate_bwd(name, u, dg, cw, cb, tm=1024, tc=512):
    s_len, f2 = u.shape
    f_dim = f2 // 2
    tm = _tile(s_len, tm, 8)
    tc = _tile(f_dim, tc)
    nj = f_dim // tc

    def body(ug_ref, uv_ref, hg_ref, hv_ref, wg_ref, wv_ref, bg_ref, bv_ref, dg_ref, duc_ref, dcw_ref, dcb_ref,
             eg_ref, ev_ref):
        first = pl.program_id(1) == 0

        @pl.when(first)
        def _():
            dcw_ref[...] = jnp.zeros_like(dcw_ref)
            dcb_ref[...] = jnp.zeros_like(dcb_ref)

        _stage_head(eg_ref, ug_ref, hg_ref, first)
        _stage_head(ev_ref, uv_ref, hv_ref, first)
        sums = [[jnp.zeros((8, tc), F32) for _ in range(4)] for _ in range(2)]
        for head, r0, n in _row_chunks(tm):
            ug = _taps(eg_ref, 8, n) if head else _taps(ug_ref, r0, n)
            uv = _taps(ev_ref, 8, n) if head else _taps(uv_ref, r0, n)
            gc = _conv3(ug, wg_ref, bg_ref)
            vc = _conv3(uv, wv_ref, bv_ref)
            sg = _sigmoid(gc)
            dgv = dg_ref[r0:r0 + n, :]
            dvc = dgv * (gc * sg)
            dgc = dgv * vc * (sg * (1.0 + gc * (1.0 - sg)))
            duc_ref[0, r0:r0 + n, :] = dgc
            duc_ref[1, r0:r0 + n, :] = dvc
            for half, (d, us) in enumerate(((dgc, ug), (dvc, uv))):
                sums[half][3] = sums[half][3] + _fold8(d)
                for tap in range(3):
                    sums[half][tap] = sums[half][tap] + _fold8(d * us[2 - tap])
        for half in range(2):
            dcb_ref[half] += jnp.sum(sums[half][3], axis=0, keepdims=True)
            for tap in range(3):
                dcw_ref[half, tap:tap + 1, :] += jnp.sum(sums[half][tap], axis=0, keepdims=True)

    order = lambda j, i: (i, j)
    return pl.pallas_call(
        body, name=name, grid=(nj, s_len // tm),
        in_specs=_conv_specs(tm, tc, nj, order) + [pl.BlockSpec((tm, tc), lambda j, i: (i, j))],
        out_specs=[pl.BlockSpec((2, tm, tc), lambda j, i: (0, i, j)),
                   pl.BlockSpec((2, 3, tc), lambda j, i: (0, 0, j)),
                   pl.BlockSpec((2, 1, tc), lambda j, i: (0, 0, j))],
        out_shape=[jax.ShapeDtypeStruct((2, s_len, f_dim), F32), jax.ShapeDtypeStruct((2, 3, f_dim), F32),
                   jax.ShapeDtypeStruct((2, 1, f_dim), F32)],
        scratch_shapes=[pltpu.VMEM((8 + CONV_HEAD, tc), F32)] * 2,
        compiler_params=_params(("parallel", "arbitrary")))(u, u, u, u, cw, cw, cb, cb, dg)


def _conv_t(name, duc, cw, tm=1024, tc=512):
    _, s_len, f_dim = duc.shape
    tm = _tile(s_len, tm, 8)
    tc = _tile(f_dim, tc)
    nj = f_dim // tc
    nb = s_len // tm
    hb = tm // 8

    def body(d_ref, halo_ref, w_ref, o_ref, ext_ref):
        last = pl.program_id(1) == nb - 1
        tail = tm - CONV_HEAD
        ext_ref[0:CONV_HEAD, :] = d_ref[0, tail:tm, :]
        ext_ref[CONV_HEAD:, :] = jnp.where(last, 0.0, halo_ref[0])

        def out(ref, r0, n):
            return (w_ref[2:3, :] * ref[r0:r0 + n, :] + w_ref[1:2, :] * ref[r0 + 1:r0 + 1 + n, :]
                    + w_ref[0:1, :] * ref[r0 + 2:r0 + 2 + n, :]).astype(BF16)

        for r0 in range(0, tail, CONV_CHUNK):
            n = min(CONV_CHUNK, tail - r0)
            o_ref[r0:r0 + n, :] = out(d_ref.at[0], r0, n)
        o_ref[tail:tm, :] = out(ext_ref, 0, CONV_HEAD)

    return pl.pallas_call(
        body, name=name, grid=(2, nb, nj),
        in_specs=[pl.BlockSpec((1, tm, tc), lambda p, i, j: (p, i, j)),
                  pl.BlockSpec((1, 8, tc), lambda p, i, j: (p, jnp.minimum((i + 1) * hb, nb * hb - 1), j)),
                  pl.BlockSpec((3, tc), lambda p, i, j: (0, p * nj + j))],
        out_specs=pl.BlockSpec((tm, tc), lambda p, i, j: (i, p * nj + j)),
        out_shape=jax.ShapeDtypeStruct((s_len, 2 * f_dim), BF16),
        scratch_shapes=[pltpu.VMEM((CONV_HEAD + 8, tc), F32)],
        compiler_params=_params(("parallel", "parallel", "parallel")))(duc, duc, cw)


def _loss_head(name, y, tgt, tm=512):
    s_len, d = y.shape
    tm = _tile(s_len, tm, 8)

    def body(y_ref, t_ref, dy_ref, l_ref):
        @pl.when(pl.program_id(0) == 0)
        def _():
            l_ref[...] = jnp.zeros_like(l_ref)

        err = y_ref[...] - t_ref[...]
        dy_ref[...] = err * (1.0 / d)
        l_ref[...] += 0.5 * jnp.sum(jnp.sum(err * err, axis=1, keepdims=True) * (1.0 / d), axis=0, keepdims=True)

    blk = pl.BlockSpec((tm, d), lambda i: (i, 0))
    return pl.pallas_call(
        body, name=name, grid=(s_len // tm,), in_specs=[blk, blk],
        out_specs=[blk, pl.BlockSpec((1, LANES), lambda i: (0, 0))],
        out_shape=[jax.ShapeDtypeStruct((s_len, d), F32), jax.ShapeDtypeStruct((1, LANES), F32)],
        compiler_params=_params(("arbitrary",)))(y, tgt)


def _adamw(name, parts_list, w, m, v, tr=256):
    n_l = len(parts_list)
    n_parts, rows, cols = parts_list[0].shape
    row_bytes = -(-cols // LANES) * LANES * (2 * n_l * n_parts * parts_list[0].dtype.itemsize + 2 * 7 * 4)
    tr = _tile(rows, min(tr, max(16, ADAMW_VMEM // row_bytes // 16 * 16)), 16)
    nb = rows // tr
    c1 = 1.0 - ADAM_B1 ** ADAM_STEP
    c2 = 1.0 - ADAM_B2 ** ADAM_STEP

    def body(*refs):
        p_refs = refs[:n_l]
        w_ref, m_ref, v_ref, g_ref, d_ref, nm_ref, nv_ref = refs[n_l:]
        for l, p_ref in enumerate(p_refs):
            @pl.when(pl.program_id(0) == l)
            def _(p_ref=p_ref):
                g = p_ref[0].astype(F32)
                for n in range(1, n_parts):
                    g = g + p_ref[n].astype(F32)
                nm = ADAM_B1 * m_ref[...] + (1.0 - ADAM_B1) * g
                nv = ADAM_B2 * v_ref[...] + (1.0 - ADAM_B2) * (g * g)
                g_ref[...] = g
                nm_ref[...] = nm
                nv_ref[...] = nv
                d_ref[...] = -ADAM_LR * ((nm / c1) / (jnp.sqrt(nv / c2) + ADAM_EPS) + ADAM_WD * w_ref[...])

    p_specs = [pl.BlockSpec((n_parts, tr, cols), lambda li, i, l=l: (0, jnp.where(li == l, i, 0), 0))
               for l in range(n_l)]
    blk = pl.BlockSpec((tr, cols), lambda li, i: (li * nb + i, 0))
    return pl.pallas_call(
        body, name=name, grid=(n_l, nb), in_specs=p_specs + [blk, blk, blk],
        out_specs=[blk] * 4, out_shape=[jax.ShapeDtypeStruct((n_l * rows, cols), F32)] * 4,
        compiler_params=_params(("arbitrary", "arbitrary")))(*parts_list, w, m, v)


def _peers():
    x, y, c = lax.axis_index("x"), lax.axis_index("y"), lax.axis_index("c")
    out = []
    for k in range(1, N_DEV):
        fx, fy, fc = (k >> 2) & 1, (k >> 1) & 1, k & 1
        px, py, pc = x ^ fx, y ^ fy, c ^ fc
        out.append((k - 1, (px, py, pc), 4 * px + 2 * py + pc))
    return 4 * x + 2 * y + c, out


def _block_of(ref, axis, index, size):
    if axis is None:
        return ref.at[index]
    idx = [slice(None)] * len(ref.shape)
    idx[axis] = pl.ds(pl.multiple_of(index * size, size), size)
    return ref.at[tuple(idx)]


def _land_shape(shape, axis, scatter):
    shape = list(shape)
    if scatter:
        if axis is None:
            return tuple(shape)
        shape[axis] //= N_DEV
        return (N_DEV, *shape)
    if axis is None:
        return (N_DEV, *shape)
    shape[axis] *= N_DEV
    return tuple(shape)


def _copy_ends(axis, scatter, src, land, me, idx):
    if scatter:
        size = None if axis is None else src.shape[axis] // N_DEV
        return _block_of(src, axis, idx, size), land.at[me]
    return src, _block_of(land, axis, me, None if axis is None else src.shape[axis])


def _remote_copies(axes, scatter, in_refs, land_refs, send_sems, recv_sems):
    me, peers = _peers()
    out = []
    for a, (axis, src, land) in enumerate(zip(axes, in_refs, land_refs)):
        for k, pos, idx in peers:
            s, d = _copy_ends(axis, scatter, src, land, me, idx)
            out.append(pltpu.make_async_remote_copy(
                src_ref=s, dst_ref=d, send_sem=send_sems.at[a * (N_DEV - 1) + k],
                recv_sem=recv_sems.at[a * (N_DEV - 1) + k],
                device_id=pos, device_id_type=MESH))
    return out


def _exchange(name, arrays, axes, scatter):
    n = len(arrays)

    def body(*refs):
        copies = _all_copies(axes, scatter, refs[:n], refs[n:2 * n], *refs[2 * n:])
        for cp in copies:
            cp.start()
        for cp in copies:
            cp.wait()

    any_spec = pl.BlockSpec(memory_space=pl.ANY)
    return pl.pallas_call(
        body, name=name, in_specs=[any_spec] * n, out_specs=[any_spec] * n,
        out_shape=[jax.ShapeDtypeStruct(_land_shape(a.shape, ax, scatter), a.dtype) for a, ax in zip(arrays, axes)],
        scratch_shapes=_exchange_sems(n),
        compiler_params=pltpu.CompilerParams(has_side_effects=True))(*arrays)


def _all_copies(axes, scatter, in_refs, land_refs, send_sems, recv_sems, local_sems):
    me, _ = _peers()
    copies = _remote_copies(axes, scatter, in_refs, land_refs, send_sems, recv_sems)
    for a, (axis, src, land) in enumerate(zip(axes, in_refs, land_refs)):
        s, d = _copy_ends(axis, scatter, src, land, me, me)
        copies.append(pltpu.make_async_copy(s, d, local_sems.at[a]))
    return copies


def _exchange_sems(n):
    return [pltpu.SemaphoreType.DMA((n * (N_DEV - 1),)), pltpu.SemaphoreType.DMA((n * (N_DEV - 1),)),
            pltpu.SemaphoreType.DMA((n,))]


def _hosted_call(body, ride, *, name, grid, in_specs, out_specs, out_shape, sem, args, scratch_shapes=()):
    if ride is None:
        return pl.pallas_call(body, name=name, grid=grid, in_specs=in_specs, out_specs=out_specs, out_shape=out_shape,
                              scratch_shapes=list(scratch_shapes), compiler_params=_params(sem))(*args), None
    arrays, axes, scatter = ride
    n, n_in, n_out, n_scr = len(arrays), len(in_specs), len(out_specs), len(scratch_shapes)

    def hosted(*refs):
        main_in, ride_in = refs[:n_in], refs[n_in:n_in + n]
        o0 = n_in + n
        main_out, lands = refs[o0:o0 + n_out], refs[o0 + n_out:o0 + n_out + n]
        s0 = o0 + n_out + n
        main_scr, sems = refs[s0:s0 + n_scr], refs[s0 + n_scr:]
        ids = [pl.program_id(i) for i in range(len(grid))]
        first = functools.reduce(jnp.logical_and, [i == 0 for i in ids])
        last = functools.reduce(jnp.logical_and, [i == g - 1 for i, g in zip(ids, grid)])

        @pl.when(first)
        def _():
            for cp in _all_copies(axes, scatter, ride_in, lands, *sems):
                cp.start()

        body(*main_in, *main_out, *main_scr)

        @pl.when(last)
        def _():
            for cp in _all_copies(axes, scatter, ride_in, lands, *sems):
                cp.wait()

    any_spec = pl.BlockSpec(memory_space=pl.ANY)
    res = pl.pallas_call(
        hosted, name=name, grid=grid, in_specs=list(in_specs) + [any_spec] * n,
        out_specs=list(out_specs) + [any_spec] * n,
        out_shape=list(out_shape) + [jax.ShapeDtypeStruct(_land_shape(a.shape, ax, scatter), a.dtype)
                                     for a, ax in zip(arrays, axes)],
        scratch_shapes=list(scratch_shapes) + _exchange_sems(n),
        compiler_params=pltpu.CompilerParams(dimension_semantics=("arbitrary",) * len(grid),
                                             vmem_limit_bytes=VMEM_LIMIT, has_side_effects=True))(*args, *arrays)
    return res[:n_out], res[n_out:]


def _row(vec, width=None):
    vec = vec.reshape(1, -1)
    if width is not None and vec.shape[1] < width:
        vec = jnp.pad(vec, ((0, 0), (0, width - vec.shape[1])))
    return vec


def _riding(rides, lands, own=None):
    def run(key, fn, *args, **kwargs):
        if rides is None or key not in rides:
            return fn(*args, **kwargs)
        arrays, axes, scatter = rides[key]
        arrays = [own[a] if isinstance(a, str) else a for a in arrays]
        res, lands[key] = fn(*args, ride=(arrays, axes, scatter), **kwargs)
        return res
    return run


def _layer_fwd(l, x, sp, bp, rides=None):
    n_heads = x.shape[1] // (2 * HEAD_DIM)
    lands = {}
    run = _riding(rides, lands)
    proj, h = run("in_proj", _mm_nn, f"l{l}_in_proj", [x], [_row(sp["attn_norm"])], bp["w_in"])
    qg, kg, bf = _row(sp["q_norm"]), _row(sp["k_norm"]), _row(sp["b_forget"], LANES)
    pa, c = _prep_fwd(f"l{l}_prep", proj, qg, kg, bf, n_heads)
    ct = c[:, :n_heads].T
    o_a, lse = _fox_fwd(f"l{l}_fox", pa, c, ct, n_heads)
    o_b, tot, kst = _sb_fwd(f"l{l}_sb", pa, n_heads)
    gfox, gsb = _row(sp["out_norm_fox"]), _row(sp["out_norm_sb"])
    x1, merged = run("out_proj", _mm_nn, f"l{l}_out_proj", [o_a, o_b], [gfox, gsb], bp["w_out"], resid=x)
    u, h2 = run("up_proj", _mm_nn, f"l{l}_up_proj", [x1], [_row(sp["ffn_norm"])], bp["w_up"])
    cb = _row(sp["conv_b"])
    g = _conv_gate_fwd(f"l{l}_conv_gate", u, bp["conv_w"], cb)
    x2 = run("down_proj", _mm_nn, f"l{l}_down_proj", [g], [], bp["w_down"], resid=x1, tn_cap=512)
    saved = dict(x=x, h=h, proj=proj, pa=pa, c=c, ct=ct, o_a=o_a, lse=lse, o_b=o_b, tot=tot, kst=kst, merged=merged, x1=x1,
                 h2=h2, u=u, g=g, qg=qg, kg=kg, bf=bf, gfox=gfox, gsb=gsb, cb=cb)
    return x2, saved, lands


def _layer_bwd(l, dx, sp, bp, sv, rides=None):
    n_heads = dx.shape[1] // (2 * HEAD_DIM)
    gr, lands = {}, {}
    run = _riding(rides, lands, gr)
    dg = run("d_down_act", _mm_nt, f"l{l}_d_down_act", dx, bp["w_down"])
    gr["w_down"] = _mm_tn(f"l{l}_d_w_down", sv["g"], dx)
    duc, dcw, dcb = _conv_gate_bwd(f"l{l}_d_conv_gate", sv["u"], dg, bp["conv_w"], sv["cb"])
    gr["conv_w"] = dcw.transpose(1, 0, 2).reshape(3, -1)
    gr["conv_b"] = dcb.reshape(-1)
    du = _conv_t(f"l{l}_d_conv", duc, bp["conv_w"])
    gr["w_up"] = run("d_w_up", _mm_tn, f"l{l}_d_w_up", sv["h2"], du)
    dx1, dffn = run("d_up_act", _mm_nt, f"l{l}_d_up_act", du, bp["w_up"], "rms_bwd", [sv["x1"]],
                    [_row(sp["ffn_norm"])], dres=dx)
    gr["ffn_norm"] = dffn.reshape(-1)
    gr["w_out"] = _mm_tn(f"l{l}_d_w_out", sv["merged"], dx1)
    do_a, do_b, dgfox, dgsb = _mm_nt(f"l{l}_d_out_act", dx1, bp["w_out"], "rms2_bwd",
                                     [sv["o_a"], sv["o_b"]], [sv["gfox"], sv["gsb"]])
    gr["out_norm_fox"], gr["out_norm_sb"] = dgfox.reshape(-1), dgsb.reshape(-1)
    dq_a, dk_a, dv_a, dct, dcq = _fox_bwd(f"l{l}_d_fox", sv["pa"], sv["c"], sv["ct"], sv["o_a"], do_a, sv["lse"],
                                          n_heads)
    d_sb = _sb_bwd(f"l{l}_d_sb", sv["pa"], do_b, sv["tot"], sv["kst"], n_heads)
    dc = jnp.pad((dct + dcq[:, :, 0]).T, ((0, 0), (0, LANES - n_heads)))
    dproj, dqg, dkg, dbf = _prep_bwd(f"l{l}_d_prep", sv["proj"], sv["qg"], sv["kg"], sv["bf"],
                                     (dq_a, dk_a, dv_a), d_sb, dc, n_heads)
    gr["q_norm"], gr["k_norm"], gr["b_forget"] = dqg.reshape(-1), dkg.reshape(-1), dbf.reshape(-1)[:n_heads]
    gr["w_in"] = run("d_w_in", _mm_tn, f"l{l}_d_w_in", sv["h"], dproj)
    dx0, dattn = run("d_in_act", _mm_nt, f"l{l}_d_in_act", dproj, bp["w_in"], "rms_bwd", [sv["x"]],
                     [_row(sp["attn_norm"])], dres=dx1)
    gr["attn_norm"] = dattn.reshape(-1)
    return dx0, gr, lands


def _local_step(x, tgt, small, big):
    n_layers = len(big)
    saved = []
    for l in range(n_layers):
        x, sv, _ = _layer_fwd(l, x, small[l], big[l])
        saved.append(sv)
    dx, loss_part = _loss_head("loss_head", x, tgt)
    grads = [None] * n_layers
    for l in reversed(range(n_layers)):
        dx, grads[l], _ = _layer_bwd(l, dx, small[l], big[l], saved[l])
    return loss_part, dx, grads


def _w_in_to_internal(w, n_heads):
    w3 = 3 * n_heads * HEAD_DIM
    pad = jnp.zeros(w.shape[:-1] + (LANES - n_heads,), w.dtype)
    return jnp.concatenate([w[..., :w3], w[..., w3 + n_heads:], w[..., w3:w3 + n_heads], pad], axis=-1)


def _w_in_from_internal(w, n_heads):
    w3 = 3 * n_heads * HEAD_DIM
    return jnp.concatenate([w[..., :w3], w[..., 2 * w3:2 * w3 + n_heads], w[..., w3:2 * w3]], axis=-1)


SMALL = ("attn_norm", "b_forget", "q_norm", "k_norm", "out_norm_fox", "out_norm_sb", "ffn_norm", "conv_b")
BIG = ("w_in", "w_out", "w_up", "w_down")
WEIGHTS = ("attn_norm", "w_in", "b_forget", "q_norm", "k_norm", "out_norm_fox", "out_norm_sb", "w_out", "ffn_norm",
           "w_up", "conv_w", "conv_b", "w_down")


def kernel(x, attn_norm, w_in, b_forget, q_norm, k_norm, out_norm_fox, out_norm_sb, w_out, ffn_norm, w_up, conv_w, conv_b, w_down, loss_target, m_attn_norm, m_w_in, m_b_forget, m_q_norm, m_k_norm, m_out_norm_fox, m_out_norm_sb, m_w_out, m_ffn_norm, m_w_up, m_conv_w, m_conv_b, m_w_down, v_attn_norm, v_w_in, v_b_forget, v_q_norm, v_k_norm, v_out_norm_fox, v_out_norm_sb, v_w_out, v_ffn_norm, v_w_up, v_conv_w, v_conv_b, v_w_down):
    w = dict(attn_norm=attn_norm, w_in=w_in, b_forget=b_forget, q_norm=q_norm, k_norm=k_norm,
             out_norm_fox=out_norm_fox, out_norm_sb=out_norm_sb, w_out=w_out, ffn_norm=ffn_norm, w_up=w_up,
             conv_w=conv_w, conv_b=conv_b, w_down=w_down)
    mom = dict(attn_norm=m_attn_norm, w_in=m_w_in, b_forget=m_b_forget, q_norm=m_q_norm, k_norm=m_k_norm,
               out_norm_fox=m_out_norm_fox, out_norm_sb=m_out_norm_sb, w_out=m_w_out, ffn_norm=m_ffn_norm,
               w_up=m_w_up, conv_w=m_conv_w, conv_b=m_conv_b, w_down=m_w_down)
    var = dict(attn_norm=v_attn_norm, w_in=v_w_in, b_forget=v_b_forget, q_norm=v_q_norm, k_norm=v_k_norm,
               out_norm_fox=v_out_norm_fox, out_norm_sb=v_out_norm_sb, w_out=v_w_out, ffn_norm=v_ffn_norm,
               w_up=v_w_up, conv_w=v_conv_w, conv_b=v_conv_b, w_down=v_w_down)
    n_layers, d = attn_norm.shape
    n_heads = d // (2 * HEAD_DIM)
    me = 4 * lax.axis_index("x") + 2 * lax.axis_index("y") + lax.axis_index("c")

    shard = {k: w[k].astype(BF16) for k in BIG}
    axis_of = dict(w_in=None, w_out=0, w_up=1, w_down=0)
    host_fwd = dict(w_in="in_proj", w_out="out_proj", w_up="up_proj", w_down="down_proj")
    host_bwd = dict(w_in="d_up_act", w_out="d_down_act", w_up="d_w_up", w_down="d_in_act")
    whole_w_in = lambda g: _w_in_to_internal(g.transpose(1, 0, 2).reshape(d, -1), n_heads)
    g_in, g_out, g_up, g_down, full_cw = _exchange(
        "gather_l0", [shard[k][0] for k in BIG] + [w["conv_w"]], [axis_of[k] for k in BIG] + [2], scatter=False)
    small = [{k: w[k][l] for k in SMALL} for l in range(n_layers)]
    big = [dict(w_in=whole_w_in(g_in), w_out=g_out, w_up=g_up, w_down=g_down, conv_w=full_cw[0])]

    saved = [None] * n_layers
    act = x[0]
    for l in range(n_layers):
        rides = None
        if l + 1 < n_layers:
            rides = {host_fwd[k]: ([shard[k][l + 1]], [axis_of[k]], False) for k in BIG}
        act, saved[l], lands = _layer_fwd(l, act, small[l], big[l], rides)
        if l + 1 < n_layers:
            nxt = {k: lands[host_fwd[k]][0] for k in BIG}
            big.append(dict(nxt, w_in=whole_w_in(nxt["w_in"]), conv_w=full_cw[l + 1]))
    dx, loss_part = _loss_head("loss_head", act, loss_target[0])
    grads, parts, sends = [None] * n_layers, [None] * n_layers, None
    for l in reversed(range(n_layers)):
        rides = {}
        if sends is not None:
            rides = {host_bwd[k]: ([sends[k]], [axis_of[k]], True) for k in BIG}
        if l == 0:
            for host, k in (("d_up_act", "w_down"), ("d_in_act", "w_out"), ("d_w_in", "w_up")):
                arrays, axes, _ = rides.get(host, ([], [], True))
                rides[host] = (arrays + [k], axes + [axis_of[k]], True)
        dx, grads[l], lands = _layer_bwd(l, dx, small[l], big[l], saved[l], rides)
        if sends is not None:
            parts[l + 1] = [lands[host_bwd[k]][0] for k in BIG]
        g_in = _w_in_from_internal(grads[l]["w_in"], n_heads)
        sends = dict(grads[l], w_in=g_in.reshape(d, N_DEV, -1).transpose(1, 0, 2))
    grad_x = dx
    (p_in,) = _exchange("scatter_l0_w_in", [sends["w_in"]], [None], scatter=True)
    parts[0] = [p_in, lands["d_in_act"][-1], lands["d_w_in"][-1], lands["d_up_act"][-1]]

    stack = lambda k: jnp.stack([grads[l][k] for l in range(n_layers)])
    small_names = SMALL + ("conv_w",)
    flat = jnp.concatenate([loss_part.reshape(-1)] + [
        jnp.pad(stack(k).reshape(-1), (0, (-stack(k).size) % LANES)) for k in small_names])
    flat = flat.reshape(-1, LANES)
    (all_small,) = _exchange("gather_small_grads", [flat], [None], scatter=False)

    out = {}

    def update(name, parts_list):
        rc = (-1, parts_list[0].shape[-1])
        res = _adamw("adamw_" + name, parts_list, w[name].reshape(rc), mom[name].reshape(rc), var[name].reshape(rc))
        out[name] = [r.reshape(w[name].shape) for r in res]

    for n, name in enumerate(BIG):
        update(name, [parts[l][n] for l in range(n_layers)])
    n_rows = flat.shape[0]
    w_flat, m_flat, v_flat = [], [], []
    for src, dst in ((w, w_flat), (mom, m_flat), (var, v_flat)):
        dst.append(jnp.zeros((LANES,), F32))
        for k in small_names:
            a = src[k]
            if k == "conv_w":
                a = jnp.zeros((n_layers, 3, conv_w.shape[2] * N_DEV), F32)
            dst.append(jnp.pad(a.reshape(-1), (0, (-a.size) % LANES)))
    pack = lambda parts: jnp.concatenate(parts).reshape(n_rows, LANES)
    res = _adamw("adamw_small", [all_small], pack(w_flat), pack(m_flat), pack(v_flat))
    res = [r.reshape(-1) for r in res]
    loss = res[0][0]
    off = LANES
    g_cw_full = None
    for k in small_names:
        size = n_layers * 3 * conv_w.shape[2] * N_DEV if k == "conv_w" else w[k].size
        if k == "conv_w":
            g_cw_full = res[0][off:off + size].reshape(n_layers, 3, -1)
        else:
            out[k] = [r[off:off + size].reshape(w[k].shape) for r in res]
        off += size + (-size) % LANES
    c_loc = conv_w.shape[2]
    g_cw_mine = lax.dynamic_slice_in_dim(g_cw_full, me * c_loc, c_loc, axis=2)
    update("conv_w", [g_cw_mine.reshape(1, n_layers * 3, c_loc)])

    outs = [loss, grad_x[None]]
    for n in range(4):
        outs += [out[k][n] for k in WEIGHTS]
    return tuple(outs)
```

```python
import functools

import jax
import jax.numpy as jnp
from jax import lax
from jax.experimental import pallas as pl
from jax.experimental.pallas import tpu as pltpu

F32 = jnp.float32
BF16 = jnp.bfloat16
HEAD_DIM = 128
QK_SCALE = HEAD_DIM ** -0.5
SKIP = 110.0
BOUND_SLACK = 1.0 + 2.0 ** -6
LANES = 128
EPS = 1e-6
N_DEV = 8
ADAM_LR = 0.001
ADAM_B1 = 0.9
ADAM_B2 = 0.999
ADAM_EPS = 1e-08
ADAM_WD = 0.01
ADAM_STEP = 10
VMEM_LIMIT = 56 * 1024 * 1024
ADAMW_VMEM = 24 * 1024 * 1024
MESH = pl.DeviceIdType.MESH

NT_DIMS = (((1,), (1,)), ((), ()))
TN_DIMS = (((0,), (0,)), ((), ()))


def _tile(n, cap, mult=LANES):
    t = (min(cap, n) // mult) * mult
    while t >= mult:
        if n % t == 0:
            return t
        t -= mult
    return n


def _params(sem, vmem=VMEM_LIMIT):
    return pltpu.CompilerParams(dimension_semantics=sem, vmem_limit_bytes=vmem)


def _split_dot(x, u, n_split, x_left=True):
    acc = None
    rest = x
    for s in range(n_split):
        piece = rest.astype(BF16)
        if s + 1 < n_split:
            rest = rest - piece.astype(F32)
        d = (jnp.dot(piece, u, preferred_element_type=F32) if x_left
             else jnp.dot(u, piece, preferred_element_type=F32))
        acc = d if acc is None else acc + d
    return acc


def _tri(n, kind):
    r = lax.broadcasted_iota(jnp.int32, (n, n), 0)
    c = lax.broadcasted_iota(jnp.int32, (n, n), 1)
    return jnp.where(r >= c if kind == "ge" else r <= c, 1.0, 0.0).astype(BF16)


def _mm_nn(name, a_list, g_list, w, resid=None, tm=1024, tn_cap=1024, ride=None):
    s_len = a_list[0].shape[0]
    k_dim, n_dim = w.shape
    tm = _tile(s_len, tm, 8)
    tn = _tile(n_dim, tn_cap)
    normed = bool(g_list)

    def body(*refs):
        refs = list(refs)
        a_refs = [refs.pop(0) for _ in a_list]
        g_refs = [refs.pop(0) for _ in g_list]
        w_ref = refs.pop(0)
        r_ref = refs.pop(0) if resid is not None else None
        o_ref = refs.pop(0)
        if normed:
            h_ref = refs.pop(0)

            @pl.when(pl.program_id(1) == 0)
            def _():
                off = 0
                for a_ref, g_ref in zip(a_refs, g_refs):
                    xv = a_ref[...]
                    kk = xv.shape[1]
                    r = lax.rsqrt(jnp.mean(xv * xv, axis=1, keepdims=True) + EPS)
                    h_ref[:, off:off + kk] = (xv * r * g_ref[...]).astype(BF16)
                    off += kk

            a = h_ref[...]
        else:
            a = a_refs[0][...]
        acc = jnp.dot(a, w_ref[...], preferred_element_type=F32)
        if r_ref is not None:
            acc = acc + r_ref[...]
        o_ref[...] = acc

    in_specs = [pl.BlockSpec((tm, a.shape[1]), lambda i, j: (i, 0)) for a in a_list]
    in_specs += [pl.BlockSpec((1, g.shape[1]), lambda i, j: (0, 0)) for g in g_list]
    in_specs += [pl.BlockSpec((k_dim, tn), lambda i, j: (0, j))]
    args = list(a_list) + list(g_list) + [w]
    if resid is not None:
        in_specs.append(pl.BlockSpec((tm, tn), lambda i, j: (i, j)))
        args.append(resid)
    out_shape = [jax.ShapeDtypeStruct((s_len, n_dim), F32)]
    out_specs = [pl.BlockSpec((tm, tn), lambda i, j: (i, j))]
    if normed:
        out_shape.append(jax.ShapeDtypeStruct((s_len, k_dim), BF16))
        out_specs.append(pl.BlockSpec((tm, k_dim), lambda i, j: (i, 0)))
    res, lands = _hosted_call(body, ride, name=name, grid=(s_len // tm, n_dim // tn), in_specs=in_specs,
                              out_specs=out_specs, out_shape=out_shape, sem=("parallel", "arbitrary"), args=args)
    res = res if normed else res[0]
    return res if ride is None else (res, lands)


def _rms_bwd(dh, xv, gv, r=None):
    if r is None:
        r = lax.rsqrt(jnp.mean(xv * xv, axis=1, keepdims=True) + EPS)
    xhat = xv * r
    dxh = dh * gv
    dx = r * (dxh - xhat * jnp.mean(dxh * xhat, axis=1, keepdims=True))
    return dx, dh * xhat


def _mm_nt(name, da, w, mode="plain", xs=(), gs=(), dres=None, tm=512, tk_cap=1024, ride=None):
    s_len, kc = da.shape
    n_out = w.shape[0]
    tm = _tile(s_len, tm, 8)
    tk = _tile(kc, tk_cap)
    nk = kc // tk

    def body(*refs):
        refs = list(refs)
        da_ref, w_ref = refs.pop(0), refs.pop(0)
        x_refs = [refs.pop(0) for _ in xs]
        g_refs = [refs.pop(0) for _ in gs]
        dres_ref = refs.pop(0) if dres is not None else None
        acc_ref = refs[0] if mode == "plain" else refs.pop()
        out_refs = refs
        i, k = pl.program_id(0), pl.program_id(1)

        @pl.when(k == 0)
        def _():
            acc_ref[...] = jnp.zeros_like(acc_ref)

        acc_ref[...] += lax.dot_general(da_ref[...].astype(BF16), w_ref[...], NT_DIMS,
                                        preferred_element_type=F32)

        if mode == "plain":
            return

        @pl.when(k == nk - 1)
        def _():
            n_x = len(xs)
            dx_refs, dg_refs = out_refs[:n_x], out_refs[n_x:]

            @pl.when(i == 0)
            def _():
                for dg_ref in dg_refs:
                    dg_ref[...] = jnp.zeros_like(dg_ref)

            off = 0
            for x_ref, g_ref, dx_ref, dg_ref in zip(x_refs, g_refs, dx_refs, dg_refs):
                kk = x_ref.shape[1]
                dx, dgp = _rms_bwd(acc_ref[:, off:off + kk], x_ref[...], g_ref[...])
                if dres_ref is not None:
                    dx = dx + dres_ref[...]
                dx_ref[...] = dx
                dg_ref[...] += jnp.sum(dgp, axis=0, keepdims=True)
                off += kk

    in_specs = [pl.BlockSpec((tm, tk), lambda i, k: (i, k)), pl.BlockSpec((n_out, tk), lambda i, k: (0, k))]
    in_specs += [pl.BlockSpec((tm, x.shape[1]), lambda i, k: (i, 0)) for x in xs]
    in_specs += [pl.BlockSpec((1, g.shape[1]), lambda i, k: (0, 0)) for g in gs]
    args = [da, w] + list(xs) + list(gs)
    if dres is not None:
        in_specs.append(pl.BlockSpec((tm, n_out), lambda i, k: (i, 0)))
        args.append(dres)
    if mode == "plain":
        out_shape = [jax.ShapeDtypeStruct((s_len, n_out), F32)]
        out_specs = [pl.BlockSpec((tm, n_out), lambda i, k: (i, 0))]
    else:
        out_shape = [jax.ShapeDtypeStruct((s_len, x.shape[1]), F32) for x in xs]
        out_specs = [pl.BlockSpec((tm, x.shape[1]), lambda i, k: (i, 0)) for x in xs]
        out_shape += [jax.ShapeDtypeStruct((1, x.shape[1]), F32) for x in xs]
        out_specs += [pl.BlockSpec((1, x.shape[1]), lambda i, k: (0, 0)) for x in xs]
    res, lands = _hosted_call(body, ride, name=name, grid=(s_len // tm, nk), in_specs=in_specs, out_specs=out_specs,
                              out_shape=out_shape, sem=("arbitrary", "arbitrary"), args=args,
                              scratch_shapes=[] if mode == "plain" else [pltpu.VMEM((tm, n_out), F32)])
    res = res[0] if mode == "plain" else res
    return res if ride is None else (res, lands)


def _mm_tn(name, a, b, tk_cap=1024, tn_cap=2048, tm=1024, ride=None):
    s_len, k_dim = a.shape
    n_dim = b.shape[1]
    tk = _tile(k_dim, tk_cap)
    tn = _tile(n_dim, tn_cap)
    tm = _tile(s_len, tm, 8)
    nm = s_len // tm

    def body(a_ref, b_ref, o_ref, acc_ref):
        m = pl.program_id(2)

        @pl.when(m == 0)
        def _():
            acc_ref[...] = jnp.zeros_like(acc_ref)

        acc_ref[...] += lax.dot_general(a_ref[...].astype(BF16), b_ref[...].astype(BF16), TN_DIMS,
                                        preferred_element_type=F32)

        @pl.when(m == nm - 1)
        def _():
            o_ref[...] = acc_ref[...].astype(BF16)

    res, lands = _hosted_call(
        body, ride, name=name, grid=(k_dim // tk, n_dim // tn, nm),
        in_specs=[pl.BlockSpec((tm, tk), lambda i, j, m: (m, i)), pl.BlockSpec((tm, tn), lambda i, j, m: (m, j))],
        out_specs=[pl.BlockSpec((tk, tn), lambda i, j, m: (i, j))],
        out_shape=[jax.ShapeDtypeStruct((k_dim, n_dim), BF16)],
        scratch_shapes=[pltpu.VMEM((tk, tn), F32)], sem=("parallel", "parallel", "arbitrary"), args=[a, b])
    return res[0] if ride is None else (res[0], lands)


def _neg_softplus(z):
    e = jnp.exp(-jnp.abs(z))
    return -(jnp.maximum(z, 0.0) + jnp.log(1.0 + e)), e


def _prep_fwd(name, proj, qg, kg, bf, n_heads, tm=256):
    s_len, n_p = proj.shape
    w_dim = n_heads * HEAD_DIM
    tm = _tile(s_len, tm, 8)

    def body(p_ref, qg_ref, kg_ref, bf_ref, pa_ref, c_ref, carry_ref):
        @pl.when(pl.program_id(0) == 0)
        def _():
            carry_ref[...] = jnp.zeros_like(carry_ref)

        for base, g_ref, mul in ((0, qg_ref, QK_SCALE), (w_dim, kg_ref, None)):
            for hh in range(n_heads):
                sl = slice(base + hh * HEAD_DIM, base + (hh + 1) * HEAD_DIM)
                xv = p_ref[:, sl]
                r = lax.rsqrt(jnp.mean(xv * xv, axis=1, keepdims=True) + EPS)
                y = xv * r * g_ref[...]
                pa_ref[:, sl] = (y if mul is None else y * mul).astype(BF16)
        pa_ref[:, 2 * w_dim:3 * w_dim] = p_ref[:, 2 * w_dim:3 * w_dim].astype(BF16)
        pa_ref[:, 3 * w_dim:4 * w_dim] = (p_ref[:, 3 * w_dim:4 * w_dim] * QK_SCALE).astype(BF16)
        pa_ref[:, 4 * w_dim:] = p_ref[:, 4 * w_dim:6 * w_dim].astype(BF16)
        f = p_ref[:, 6 * w_dim:] + bf_ref[...]
        lf, _ = _neg_softplus(-f)
        lane = lax.broadcasted_iota(jnp.int32, lf.shape, 1)
        lf = jnp.where(lane < n_heads, lf, 0.0)
        cb = _split_dot(lf, _tri(tm, "ge"), 3, x_left=False) + carry_ref[...]
        c_ref[...] = cb
        carry_ref[...] = cb[tm - 1:tm, :]

    return pl.pallas_call(
        body, name=name, grid=(s_len // tm,),
        in_specs=[pl.BlockSpec((tm, n_p), lambda i: (i, 0))] + [pl.BlockSpec((1, LANES), lambda i: (0, 0))] * 3,
        out_specs=[pl.BlockSpec((tm, 6 * w_dim), lambda i: (i, 0)), pl.BlockSpec((tm, LANES), lambda i: (i, 0))],
        out_shape=[jax.ShapeDtypeStruct((s_len, 6 * w_dim), BF16), jax.ShapeDtypeStruct((s_len, LANES), F32)],
        scratch_shapes=[pltpu.VMEM((1, LANES), F32)],
        compiler_params=_params(("arbitrary",)))(proj, qg, kg, bf)


def _prep_bwd(name, proj, qg, kg, bf, d_fox, d_sb, dc, n_heads, tm=256):
    s_len, n_p = proj.shape
    w_dim = n_heads * HEAD_DIM
    tm = _tile(s_len, tm, 8)
    nb = s_len // tm

    def body(p_ref, qg_ref, kg_ref, bf_ref, dqa_ref, dka_ref, dva_ref, dqb_ref, dkb_ref, dvb_ref, dc_ref,
             dp_ref, dqg_ref, dkg_ref, dbf_ref, carry_ref):
        @pl.when(pl.program_id(0) == 0)
        def _():
            for ref in (carry_ref, dqg_ref, dkg_ref, dbf_ref):
                ref[...] = jnp.zeros_like(ref)

        for base, g_ref, d_ref, dg_ref in ((0, qg_ref, dqa_ref, dqg_ref), (w_dim, kg_ref, dka_ref, dkg_ref)):
            dg = jnp.zeros((1, HEAD_DIM), F32)
            for hh in range(n_heads):
                sl = slice(base + hh * HEAD_DIM, base + (hh + 1) * HEAD_DIM)
                dx, dgp = _rms_bwd(d_ref[:, hh * HEAD_DIM:(hh + 1) * HEAD_DIM], p_ref[:, sl], g_ref[...])
                dp_ref[:, sl] = dx.astype(BF16)
                dg = dg + jnp.sum(dgp, axis=0, keepdims=True)
            dg_ref[...] += dg
        for n, d_ref in enumerate((dva_ref, dqb_ref, dkb_ref, dvb_ref)):
            dp_ref[:, (2 + n) * w_dim:(3 + n) * w_dim] = d_ref[...].astype(BF16)
        dlf = _split_dot(dc_ref[...], _tri(tm, "le"), 3, x_left=False) + carry_ref[...]
        carry_ref[...] = dlf[0:1, :]
        f = p_ref[:, 6 * w_dim:] + bf_ref[...]
        e = jnp.exp(-jnp.abs(f))
        sig_neg = jnp.where(f >= 0, e, 1.0) / (1.0 + e)
        lane = lax.broadcasted_iota(jnp.int32, f.shape, 1)
        df = jnp.where(lane < n_heads, dlf * sig_neg, 0.0)
        dp_ref[:, 6 * w_dim:] = df.astype(BF16)
        dbf_ref[...] += jnp.sum(df, axis=0, keepdims=True)

    rev = lambda i: (nb - 1 - i, 0)
    vec = pl.BlockSpec((1, LANES), lambda i: (0, 0))
    return pl.pallas_call(
        body, name=name, grid=(nb,),
        in_specs=[pl.BlockSpec((tm, n_p), rev), vec, vec, vec] + [pl.BlockSpec((tm, w_dim), rev)] * 6
        + [pl.BlockSpec((tm, LANES), rev)],
        out_specs=[pl.BlockSpec((tm, n_p), rev), vec, vec, vec],
        out_shape=[jax.ShapeDtypeStruct((s_len, n_p), BF16)] + [jax.ShapeDtypeStruct((1, LANES), F32)] * 3,
        scratch_shapes=[pltpu.VMEM((1, LANES), F32)],
        compiler_params=_params(("arbitrary",)))(proj, qg, kg, bf, *d_fox, *d_sb, dc)


def _head_col(c_blk, h):
    lane = lax.broadcasted_iota(jnp.int32, c_blk.shape, 1)
    return jnp.sum(jnp.where(lane == h, c_blk, 0.0), axis=1, keepdims=True)


def _key_norm_max(k_ref):
    kf = k_ref[...].astype(F32)
    return jnp.sqrt(jnp.max(jnp.sum(kf * kf, axis=1, keepdims=True), axis=0, keepdims=True))


def _logit_bound(q, kmax):
    qf = q.astype(F32)
    return jnp.sqrt(jnp.sum(qf * qf, axis=1, keepdims=True)) * kmax * BOUND_SLACK


def _c_block_end(ct_ref, h, kb, tk):
    return jnp.min(ct_ref[pl.ds(h, 1), pl.ds(pl.multiple_of(kb * tk, tk), tk)])


def _fox_fwd(name, pa, c, ct, n_heads, tq=1024, ride=None):
    s_len = pa.shape[0]
    tq = _tile(s_len, tq, LANES)
    hp = ct.shape[0]

    def body(q_ref, k_ref, v_ref, c_ref, ct_ref, o_ref, lse_ref, kmax_ref):
        h, qi = pl.program_id(0), pl.program_id(1)

        @pl.when(qi == 0)
        def _():
            kmax_ref[...] = _key_norm_max(k_ref)

        q = q_ref[...]
        cq = _head_col(c_ref[...], h)
        top = _logit_bound(q, kmax_ref[...]) + cq
        row = lax.broadcasted_iota(jnp.int32, (tq, tq), 0)
        col = lax.broadcasted_iota(jnp.int32, (tq, tq), 1)

        def step(kb, carry, masked):
            m, l, acc = carry
            ks = pl.multiple_of(kb * tq, tq)
            k = k_ref[pl.ds(ks, tq), :]
            v = v_ref[pl.ds(ks, tq), :]
            ck = ct_ref[pl.ds(h, 1), pl.ds(ks, tq)]
            s = lax.dot_general(q, k, NT_DIMS, preferred_element_type=F32) + (cq - ck)
            if masked:
                s = jnp.where(col <= row, s, -jnp.inf)
            m_new = jnp.maximum(m, jnp.max(s, axis=1, keepdims=True))
            alpha = jnp.exp(m - m_new)
            p = jnp.exp(s - m_new)
            l = alpha * l + jnp.sum(p, axis=1, keepdims=True)
            acc = alpha * acc + jnp.dot(p.astype(BF16), v, preferred_element_type=F32)
            return m_new, l, acc

        def margin(kb, m):
            return jnp.max(top - m) - _c_block_end(ct_ref, h, jnp.maximum(kb, 0), tq)

        def walk(state):
            kb, _, m, l, acc = state
            m, l, acc = step(kb, (m, l, acc), False)
            return kb - 1, margin(kb - 1, m), m, l, acc

        init = (jnp.full((tq, 1), -jnp.inf, F32), jnp.zeros((tq, 1), F32), jnp.zeros((tq, HEAD_DIM), F32))
        carry = step(qi, init, True)
        state = (qi - 1, margin(qi - 1, carry[0])) + carry
        _, _, m, l, acc = lax.while_loop(lambda st: (st[0] >= 0) & (st[1] > -SKIP), walk, state)
        o_ref[...] = acc / l
        lse_ref[0] = m + jnp.log(l)

    res, lands = _hosted_call(
        body, ride, name=name, grid=(n_heads, s_len // tq),
        in_specs=[pl.BlockSpec((tq, HEAD_DIM), lambda h, i: (i, h)),
                  pl.BlockSpec((s_len, HEAD_DIM), lambda h, i: (0, n_heads + h)),
                  pl.BlockSpec((s_len, HEAD_DIM), lambda h, i: (0, 2 * n_heads + h)),
                  pl.BlockSpec((tq, LANES), lambda h, i: (i, 0)),
                  pl.BlockSpec((hp, s_len), lambda h, i: (0, 0))],
        out_specs=[pl.BlockSpec((tq, HEAD_DIM), lambda h, i: (i, h)),
                   pl.BlockSpec((1, tq, 1), lambda h, i: (h, i, 0))],
        out_shape=[jax.ShapeDtypeStruct((s_len, n_heads * HEAD_DIM), F32),
                   jax.ShapeDtypeStruct((n_heads, s_len, 1), F32)],
        scratch_shapes=[pltpu.VMEM((1, 1), F32)], sem=("parallel", "arbitrary"), args=[pa, pa, pa, c, ct])
    return res if ride is None else (res, lands)


def _fox_bwd(name, pa, c, ct, o, do, lse, n_heads, tq=1024):
    s_len = pa.shape[0]
    tq = _tile(s_len, tq, LANES)
    hp = ct.shape[0]
    w_dim = n_heads * HEAD_DIM

    def body(q_ref, k_ref, v_ref, c_ref, ct_ref, o_ref, do_ref, lse_ref, dq_ref, dk_ref, dv_ref, dct_ref, dcq_ref,
             kmax_ref):
        h, qi = pl.program_id(0), pl.program_id(1)

        @pl.when(qi == 0)
        def _():
            dk_ref[...] = jnp.zeros_like(dk_ref)
            dv_ref[...] = jnp.zeros_like(dv_ref)
            kmax_ref[...] = _key_norm_max(k_ref)

        @pl.when((qi == 0) & (h == 0))
        def _():
            dct_ref[...] = jnp.zeros_like(dct_ref)

        q = q_ref[...]
        do32 = do_ref[...]
        dob = do32.astype(BF16)
        dsum = jnp.sum(do32 * o_ref[...], axis=1, keepdims=True)
        lse_v = lse_ref[0]
        cq = _head_col(c_ref[...], h)
        row = lax.broadcasted_iota(jnp.int32, (tq, tq), 0)
        col = lax.broadcasted_iota(jnp.int32, (tq, tq), 1)

        def step(kb, carry, masked):
            dq, dcq = carry
            ks = pl.multiple_of(kb * tq, tq)
            k = k_ref[pl.ds(ks, tq), :]
            v = v_ref[pl.ds(ks, tq), :]
            ck = ct_ref[pl.ds(h, 1), pl.ds(ks, tq)]
            s = lax.dot_general(q, k, NT_DIMS, preferred_element_type=F32) + (cq - ck)
            p = jnp.exp(s - lse_v)
            if masked:
                p = jnp.where(col <= row, p, 0.0)
            dp = lax.dot_general(dob, v, NT_DIMS, preferred_element_type=F32)
            ds = p * (dp - dsum)
            dsb = ds.astype(BF16)
            dk_ref[pl.ds(ks, tq), :] += lax.dot_general(dsb, q, TN_DIMS, preferred_element_type=F32)
            dv_ref[pl.ds(ks, tq), :] += lax.dot_general(p.astype(BF16), dob, TN_DIMS, preferred_element_type=F32)
            dct_ref[pl.ds(h, 1), pl.ds(ks, tq)] -= jnp.sum(ds, axis=0, keepdims=True)
            return dq + jnp.dot(dsb, k, preferred_element_type=F32), dcq + jnp.sum(ds, axis=1, keepdims=True)

        top = jnp.max(_logit_bound(q, kmax_ref[...]) + cq - lse_v)

        def margin(kb):
            return top - _c_block_end(ct_ref, h, jnp.maximum(kb, 0), tq)

        def walk(state):
            kb, _, dq, dcq = state
            dq, dcq = step(kb, (dq, dcq), False)
            return kb - 1, margin(kb - 1), dq, dcq

        init = (jnp.zeros((tq, HEAD_DIM), F32), jnp.zeros((tq, 1), F32))
        state = (qi - 1, margin(qi - 1)) + step(qi, init, True)
        _, _, dq, dcq = lax.while_loop(lambda st: (st[0] >= 0) & (st[1] > -SKIP), walk, state)
        dq_ref[...] = dq * QK_SCALE
        dcq_ref[0] = dcq

    blk = pl.BlockSpec((tq, HEAD_DIM), lambda h, i: (i, h))
    full = pl.BlockSpec((s_len, HEAD_DIM), lambda h, i: (0, h))
    return pl.pallas_call(
        body, name=name, grid=(n_heads, s_len // tq),
        in_specs=[blk,
                  pl.BlockSpec((s_len, HEAD_DIM), lambda h, i: (0, n_heads + h)),
                  pl.BlockSpec((s_len, HEAD_DIM), lambda h, i: (0, 2 * n_heads + h)),
                  pl.BlockSpec((tq, LANES), lambda h, i: (i, 0)),
                  pl.BlockSpec((hp, s_len), lambda h, i: (0, 0)),
                  blk, blk,
                  pl.BlockSpec((1, tq, 1), lambda h, i: (h, i, 0))],
        out_specs=[blk, full, full, pl.BlockSpec((hp, s_len), lambda h, i: (0, 0)),
                   pl.BlockSpec((1, tq, 1), lambda h, i: (h, i, 0))],
        out_shape=[jax.ShapeDtypeStruct((s_len, w_dim), F32)] * 3 + [jax.ShapeDtypeStruct((hp, s_len), F32),
                                                                     jax.ShapeDtypeStruct((n_heads, s_len, 1), F32)],
        scratch_shapes=[pltpu.VMEM((1, 1), F32)],
        compiler_params=_params(("arbitrary", "arbitrary")))(pa, pa, pa, c, ct, o, do, lse)


def _sb_fwd(name, pa, n_heads, tq=512, tk=256, ride=None):
    s_len = pa.shape[0]
    tq = _tile(s_len, tq, LANES)
    tk = _tile(tq, tk, LANES)
    nsub = tq // tk

    def body(q_ref, k_ref, v_ref, o_ref, tot_ref, kst_ref, kmax_ref):
        qi = pl.program_id(1)

        @pl.when(qi == 0)
        def _():
            kmax_ref[...] = _key_norm_max(k_ref)

        q = q_ref[...]
        z_bound = _logit_bound(q, kmax_ref[...])
        u = _tri(tk, "ge")
        row = lax.broadcasted_iota(jnp.int32, (tq, tk), 0)
        col = lax.broadcasted_iota(jnp.int32, (tq, tk), 1)

        def block(ks, carry, mask_off):
            r, acc = carry
            k = k_ref[pl.ds(ks, tk), :]
            v = v_ref[pl.ds(ks, tk), :]
            z = lax.dot_general(q, k, NT_DIMS, preferred_element_type=F32)
            a, _ = _neg_softplus(z)
            if mask_off is not None:
                valid = col + mask_off < row
                a = jnp.where(valid, a, 0.0)
            rin = _split_dot(a, u, 2)
            w = jnp.exp(z + (r + rin))
            if mask_off is not None:
                w = jnp.where(valid, w, 0.0)
            acc = acc + jnp.dot(w.astype(BF16), v, preferred_element_type=F32)
            return r + rin[:, 0:1], acc

        carry = (jnp.zeros((tq, 1), F32), jnp.zeros((tq, HEAD_DIM), F32))
        q0 = pl.multiple_of(qi * tq, tq)
        for j in reversed(range(nsub)):
            carry = block(q0 + j * tk, carry, j * tk)
        def walk(state):
            kb, _, r, acc = state
            r, acc = block(pl.multiple_of(kb * tk, tk), (r, acc), None)
            return kb - 1, jnp.max(r + z_bound), r, acc

        state = (qi * nsub - 1, jnp.max(carry[0] + z_bound)) + carry
        kb, _, r, acc = lax.while_loop(lambda st: (st[0] >= 0) & (st[1] > -SKIP), walk, state)
        o_ref[...] = acc
        tot_ref[0] = r
        kst_ref[...] = jnp.full(kst_ref.shape, (kb + 1).astype(F32))

    nq = s_len // tq
    res, lands = _hosted_call(
        body, ride, name=name, grid=(n_heads, nq),
        in_specs=[pl.BlockSpec((tq, HEAD_DIM), lambda h, i: (i, 3 * n_heads + h)),
                  pl.BlockSpec((s_len, HEAD_DIM), lambda h, i: (0, 4 * n_heads + h)),
                  pl.BlockSpec((s_len, HEAD_DIM), lambda h, i: (0, 5 * n_heads + h))],
        out_specs=[pl.BlockSpec((tq, HEAD_DIM), lambda h, i: (i, h)),
                   pl.BlockSpec((1, tq, 1), lambda h, i: (h, i, 0)),
                   pl.BlockSpec((1, 1, 8, LANES), lambda h, i: (h, i, 0, 0))],
        out_shape=[jax.ShapeDtypeStruct((s_len, n_heads * HEAD_DIM), F32),
                   jax.ShapeDtypeStruct((n_heads, s_len, 1), F32),
                   jax.ShapeDtypeStruct((n_heads, nq, 8, LANES), F32)],
        scratch_shapes=[pltpu.VMEM((1, 1), F32)], sem=("parallel", "arbitrary"), args=[pa, pa, pa])
    return res if ride is None else (res, lands)


def _sb_bwd(name, pa, do, tot, kst, n_heads, tq=512, tk=256):
    s_len = pa.shape[0]
    tq = _tile(s_len, tq, LANES)
    tk = _tile(tq, tk, LANES)
    nsub = tq // tk
    w_dim = n_heads * HEAD_DIM

    def body(q_ref, k_ref, v_ref, do_ref, tot_ref, kst_ref, dq_ref, dk_ref, dv_ref):
        qi = pl.program_id(1)

        @pl.when(qi == 0)
        def _():
            dk_ref[...] = jnp.zeros_like(dk_ref)
            dv_ref[...] = jnp.zeros_like(dv_ref)

        q = q_ref[...]
        dob = do_ref[...].astype(BF16)
        u = _tri(tk, "le")
        row = lax.broadcasted_iota(jnp.int32, (tq, tk), 0)
        col = lax.broadcasted_iota(jnp.int32, (tq, tk), 1)

        def block(ks, carry, mask_off):
            rem, cpre, dq = carry
            k = k_ref[pl.ds(ks, tk), :]
            v = v_ref[pl.ds(ks, tk), :]
            z = lax.dot_general(q, k, NT_DIMS, preferred_element_type=F32)
            a, e = _neg_softplus(z)
            if mask_off is not None:
                valid = col + mask_off < row
                a = jnp.where(valid, a, 0.0)
            pin = _split_dot(a, u, 2)
            w = jnp.exp(z + (rem - (pin - a)))
            if mask_off is not None:
                w = jnp.where(valid, w, 0.0)
            g = w * lax.dot_general(dob, v, NT_DIMS, preferred_element_type=F32)
            cin = _split_dot(g, u, 2)
            beta = jnp.where(z >= 0, 1.0, e) / (1.0 + e)
            dz = g - beta * (cpre + cin)
            if mask_off is not None:
                dz = jnp.where(valid, dz, 0.0)
            dzb = dz.astype(BF16)
            dk_ref[pl.ds(ks, tk), :] += lax.dot_general(dzb, q, TN_DIMS, preferred_element_type=F32)
            dv_ref[pl.ds(ks, tk), :] += lax.dot_general(w.astype(BF16), dob, TN_DIMS, preferred_element_type=F32)
            dq = dq + jnp.dot(dzb, k, preferred_element_type=F32)
            return rem - pin[:, tk - 1:tk], cpre + cin[:, tk - 1:tk], dq

        carry = (tot_ref[0], jnp.zeros((tq, 1), F32), jnp.zeros((tq, HEAD_DIM), F32))
        first = jnp.max(kst_ref[0, 0]).astype(jnp.int32)
        carry = lax.fori_loop(first, qi * nsub, lambda n, cr: block(pl.multiple_of(n * tk, tk), cr, None), carry)
        q0 = pl.multiple_of(qi * tq, tq)
        for j in range(nsub):
            carry = block(q0 + j * tk, carry, j * tk)
        dq_ref[...] = carry[2] * QK_SCALE

    blk = pl.BlockSpec((tq, HEAD_DIM), lambda h, i: (i, h))
    full = pl.BlockSpec((s_len, HEAD_DIM), lambda h, i: (0, h))
    return pl.pallas_call(
        body, name=name, grid=(n_heads, s_len // tq),
        in_specs=[pl.BlockSpec((tq, HEAD_DIM), lambda h, i: (i, 3 * n_heads + h)),
                  pl.BlockSpec((s_len, HEAD_DIM), lambda h, i: (0, 4 * n_heads + h)),
                  pl.BlockSpec((s_len, HEAD_DIM), lambda h, i: (0, 5 * n_heads + h)),
                  blk,
                  pl.BlockSpec((1, tq, 1), lambda h, i: (h, i, 0)),
                  pl.BlockSpec((1, 1, 8, LANES), lambda h, i: (h, i, 0, 0))],
        out_specs=[blk, full, full],
        out_shape=[jax.ShapeDtypeStruct((s_len, w_dim), F32)] * 3,
        compiler_params=_params(("arbitrary", "arbitrary")))(pa, pa, pa, do, tot, kst)


CONV_HEAD = 16
CONV_CHUNK = 32


def _stage_head(ext_ref, u_ref, halo_ref, first):
    ext_ref[0:8, :] = jnp.where(first, 0.0, halo_ref[...])
    ext_ref[8:8 + CONV_HEAD, :] = u_ref[0:CONV_HEAD, :]


def _taps(ref, r0, n):
    return ref[r0:r0 + n, :], ref[r0 - 1:r0 - 1 + n, :], ref[r0 - 2:r0 - 2 + n, :]


def _row_chunks(tm):
    return [(True, 0, CONV_HEAD)] + [(False, r0, min(CONV_CHUNK, tm - r0)) for r0 in range(CONV_HEAD, tm, CONV_CHUNK)]


def _conv3(us, w_ref, b_ref):
    return w_ref[2:3, :] * us[0] + w_ref[1:2, :] * us[1] + w_ref[0:1, :] * us[2] + b_ref[...]


def _fold8(x):
    return jnp.sum(x.reshape(x.shape[0] // 8, 8, x.shape[1]), axis=0)


def _sigmoid(x):
    return 0.5 * jnp.tanh(0.5 * x) + 0.5


def _conv_specs(tm, tc, nj, order):
    hb = tm // 8
    ij = order

    def at(f):
        return lambda *g: f(*ij(*g))

    return [pl.BlockSpec((tm, tc), at(lambda i, j: (i, j))),
            pl.BlockSpec((tm, tc), at(lambda i, j: (i, j + nj))),
            pl.BlockSpec((8, tc), at(lambda i, j: (jnp.maximum(i * hb - 1, 0), j))),
            pl.BlockSpec((8, tc), at(lambda i, j: (jnp.maximum(i * hb - 1, 0), j + nj))),
            pl.BlockSpec((3, tc), at(lambda i, j: (0, j))),
            pl.BlockSpec((3, tc), at(lambda i, j: (0, j + nj))),
            pl.BlockSpec((1, tc), at(lambda i, j: (0, j))),
            pl.BlockSpec((1, tc), at(lambda i, j: (0, j + nj)))]


def _conv_gate_fwd(name, u, cw, cb, tm=1024, tc=512):
    s_len, f2 = u.shape
    f_dim = f2 // 2
    tm = _tile(s_len, tm, 8)
    tc = _tile(f_dim, tc)
    nj = f_dim // tc

    def body(ug_ref, uv_ref, hg_ref, hv_ref, wg_ref, wv_ref, bg_ref, bv_ref, g_ref, eg_ref, ev_ref):
        first = pl.program_id(0) == 0
        _stage_head(eg_ref, ug_ref, hg_ref, first)
        _stage_head(ev_ref, uv_ref, hv_ref, first)
        for head, r0, n in _row_chunks(tm):
            gc = _conv3(_taps(eg_ref, 8, n) if head else _taps(ug_ref, r0, n), wg_ref, bg_ref)
            vc = _conv3(_taps(ev_ref, 8, n) if head else _taps(uv_ref, r0, n), wv_ref, bv_ref)
            g_ref[r0:r0 + n, :] = (gc * _sigmoid(gc) * vc).astype(BF16)

    return pl.pallas_call(
        body, name=name, grid=(s_len // tm, nj),
        in_specs=_conv_specs(tm, tc, nj, lambda i, j: (i, j)),
        out_specs=pl.BlockSpec((tm, tc), lambda i, j: (i, j)),
        out_shape=jax.ShapeDtypeStruct((s_len, f_dim), BF16),
        scratch_shapes=[pltpu.VMEM((8 + CONV_HEAD, tc), F32)] * 2,
        compiler_params=_params(("parallel", "parallel")))(u, u, u, u, cw, cw, cb, cb)


def _conv_gate_bwd(name, u, dg, cw, cb, tm=1024, tc=512):
    s_len, f2 = u.shape
    f_dim = f2 // 2
    tm = _tile(s_len, tm, 8)
    tc = _tile(f_dim, tc)
    nj = f_dim // tc

    def body(ug_ref, uv_ref, hg_ref, hv_ref, wg_ref, wv_ref, bg_ref, bv_ref, dg_ref, duc_ref, dcw_ref, dcb_ref,
             eg_ref, ev_ref):
        first = pl.program_id(1) == 0

        @pl.when(first)
        def _():
            dcw_ref[...] = jnp.zeros_like(dcw_ref)
            dcb_ref[...] = jnp.zeros_like(dcb_ref)

        _stage_head(eg_ref, ug_ref, hg_ref, first)
        _stage_head(ev_ref, uv_ref, hv_ref, first)
        sums = [[jnp.zeros((8, tc), F32) for _ in range(4)] for _ in range(2)]
        for head, r0, n in _row_chunks(tm):
            ug = _taps(eg_ref, 8, n) if head else _taps(ug_ref, r0, n)
            uv = _taps(ev_ref, 8, n) if head else _taps(uv_ref, r0, n)
            gc = _conv3(ug, wg_ref, bg_ref)
            vc = _conv3(uv, wv_ref, bv_ref)
            sg = _sigmoid(gc)
            dgv = dg_ref[r0:r0 + n, :]
            dvc = dgv * (gc * sg)
            dgc = dgv * vc * (sg * (1.0 + gc * (1.0 - sg)))
            duc_ref[0, r0:r0 + n, :] = dgc
            duc_ref[1, r0:r0 + n, :] = dvc
            for half, (d, us) in enumerate(((dgc, ug), (dvc, uv))):
                sums[half][3] = sums[half][3] + _fold8(d)
                for tap in range(3):
                    sums[half][tap] = sums[half][tap] + _fold8(d * us[2 - tap])
        for half in range(2):
            dcb_ref[half] += jnp.sum(sums[half][3], axis=0, keepdims=True)
            for tap in range(3):
                dcw_ref[half, tap:tap + 1, :] += jnp.sum(sums[half][tap], axis=0, keepdims=True)

    order = lambda j, i: (i, j)
    return pl.pallas_call(
        body, name=name, grid=(nj, s_len // tm),
        in_specs=_conv_specs(tm, tc, nj, order) + [pl.BlockSpec((tm, tc), lambda j, i: (i, j))],
        out_specs=[pl.BlockSpec((2, tm, tc), lambda j, i: (0, i, j)),
                   pl.BlockSpec((2, 3, tc), lambda j, i: (0, 0, j)),
                   pl.BlockSpec((2, 1, tc), lambda j, i: (0, 0, j))],
        out_shape=[jax.ShapeDtypeStruct((2, s_len, f_dim), F32), jax.ShapeDtypeStruct((2, 3, f_dim), F32),
                   jax.ShapeDtypeStruct((2, 1, f_dim), F32)],
        scratch_shapes=[pltpu.VMEM((8 + CONV_HEAD, tc), F32)] * 2,
        compiler_params=_params(("parallel", "arbitrary")))(u, u, u, u, cw, cw, cb, cb, dg)


def _conv_t(name, duc, cw, tm=1024, tc=512):
    _, s_len, f_dim = duc.shape
    tm = _tile(s_len, tm, 8)
    tc = _tile(f_dim, tc)
    nj = f_dim // tc
    nb = s_len // tm
    hb = tm // 8

    def body(d_ref, halo_ref, w_ref, o_ref, ext_ref):
        last = pl.program_id(1) == nb - 1
        tail = tm - CONV_HEAD
        ext_ref[0:CONV_HEAD, :] = d_ref[0, tail:tm, :]
        ext_ref[CONV_HEAD:, :] = jnp.where(last, 0.0, halo_ref[0])

        def out(ref, r0, n):
            return (w_ref[2:3, :] * ref[r0:r0 + n, :] + w_ref[1:2, :] * ref[r0 + 1:r0 + 1 + n, :]
                    + w_ref[0:1, :] * ref[r0 + 2:r0 + 2 + n, :]).astype(BF16)

        for r0 in range(0, tail, CONV_CHUNK):
            n = min(CONV_CHUNK, tail - r0)
            o_ref[r0:r0 + n, :] = out(d_ref.at[0], r0, n)
        o_ref[tail:tm, :] = out(ext_ref, 0, CONV_HEAD)

    return pl.pallas_call(
        body, name=name, grid=(2, nb, nj),
        in_specs=[pl.BlockSpec((1, tm, tc), lambda p, i, j: (p, i, j)),
                  pl.BlockSpec((1, 8, tc), lambda p, i, j: (p, jnp.minimum((i + 1) * hb, nb * hb - 1), j)),
                  pl.BlockSpec((3, tc), lambda p, i, j: (0, p * nj + j))],
        out_specs=pl.BlockSpec((tm, tc), lambda p, i, j: (i, p * nj + j)),
        out_shape=jax.ShapeDtypeStruct((s_len, 2 * f_dim), BF16),
        scratch_shapes=[pltpu.VMEM((CONV_HEAD + 8, tc), F32)],
        compiler_params=_params(("parallel", "parallel", "parallel")))(duc, duc, cw)


def _loss_head(name, y, tgt, tm=512):
    s_len, d = y.shape
    tm = _tile(s_len, tm, 8)

    def body(y_ref, t_ref, dy_ref, l_ref):
        @pl.when(pl.program_id(0) == 0)
        def _():
            l_ref[...] = jnp.zeros_like(l_ref)

        err = y_ref[...] - t_ref[...]
        dy_ref[...] = err * (1.0 / d)
        l_ref[...] += 0.5 * jnp.sum(jnp.sum(err * err, axis=1, keepdims=True) * (1.0 / d), axis=0, keepdims=True)

    blk = pl.BlockSpec((tm, d), lambda i: (i, 0))
    return pl.pallas_call(
        body, name=name, grid=(s_len // tm,), in_specs=[blk, blk],
        out_specs=[blk, pl.BlockSpec((1, LANES), lambda i: (0, 0))],
        out_shape=[jax.ShapeDtypeStruct((s_len, d), F32), jax.ShapeDtypeStruct((1, LANES), F32)],
        compiler_params=_params(("arbitrary",)))(y, tgt)


def _adamw(name, parts_list, w, m, v, tr=256, ride=None):
    n_l = len(parts_list)
    n_parts, rows, cols = parts_list[0].shape
    row_bytes = -(-cols // LANES) * LANES * (2 * n_l * n_parts * parts_list[0].dtype.itemsize + 2 * 7 * 4)
    tr = _tile(rows, min(tr, max(16, ADAMW_VMEM // row_bytes // 16 * 16)), 16)
    nb = rows // tr
    c1 = 1.0 - ADAM_B1 ** ADAM_STEP
    c2 = 1.0 - ADAM_B2 ** ADAM_STEP

    def body(*refs):
        p_refs = refs[:n_l]
        w_ref, m_ref, v_ref, g_ref, d_ref, nm_ref, nv_ref = refs[n_l:]
        for l, p_ref in enumerate(p_refs):
            @pl.when(pl.program_id(0) == l)
            def _(p_ref=p_ref):
                g = p_ref[0].astype(F32)
                for n in range(1, n_parts):
                    g = g + p_ref[n].astype(F32)
                nm = ADAM_B1 * m_ref[...] + (1.0 - ADAM_B1) * g
                nv = ADAM_B2 * v_ref[...] + (1.0 - ADAM_B2) * (g * g)
                g_ref[...] = g
                nm_ref[...] = nm
                nv_ref[...] = nv
                d_ref[...] = -ADAM_LR * ((nm / c1) / (jnp.sqrt(nv / c2) + ADAM_EPS) + ADAM_WD * w_ref[...])

    p_specs = [pl.BlockSpec((n_parts, tr, cols), lambda li, i, l=l: (0, jnp.where(li == l, i, 0), 0))
               for l in range(n_l)]
    blk = pl.BlockSpec((tr, cols), lambda li, i: (li * nb + i, 0))
    res, lands = _hosted_call(
        body, ride, name=name, grid=(n_l, nb), in_specs=p_specs + [blk, blk, blk], out_specs=[blk] * 4,
        out_shape=[jax.ShapeDtypeStruct((n_l * rows, cols), F32)] * 4, sem=("arbitrary", "arbitrary"),
        args=[*parts_list, w, m, v])
    return res if ride is None else (res, lands)


def _peers():
    x, y, c = lax.axis_index("x"), lax.axis_index("y"), lax.axis_index("c")
    out = []
    for k in range(1, N_DEV):
        fx, fy, fc = (k >> 2) & 1, (k >> 1) & 1, k & 1
        px, py, pc = x ^ fx, y ^ fy, c ^ fc
        out.append((k - 1, (px, py, pc), 4 * px + 2 * py + pc))
    return 4 * x + 2 * y + c, out


def _block_of(ref, axis, index, size):
    if axis is None:
        return ref.at[index]
    idx = [slice(None)] * len(ref.shape)
    idx[axis] = pl.ds(pl.multiple_of(index * size, size), size)
    return ref.at[tuple(idx)]


def _land_shape(shape, axis, scatter):
    shape = list(shape)
    if scatter:
        if axis is None:
            return tuple(shape)
        shape[axis] //= N_DEV
        return (N_DEV, *shape)
    if axis is None:
        return (N_DEV, *shape)
    shape[axis] *= N_DEV
    return tuple(shape)


def _copy_ends(axis, scatter, src, land, me, idx):
    if scatter:
        size = None if axis is None else src.shape[axis] // N_DEV
        return _block_of(src, axis, idx, size), land.at[me]
    return src, _block_of(land, axis, me, None if axis is None else src.shape[axis])


def _remote_copies(axes, scatter, in_refs, land_refs, send_sems, recv_sems):
    me, peers = _peers()
    out = []
    for a, (axis, src, land) in enumerate(zip(axes, in_refs, land_refs)):
        for k, pos, idx in peers:
            s, d = _copy_ends(axis, scatter, src, land, me, idx)
            out.append(pltpu.make_async_remote_copy(
                src_ref=s, dst_ref=d, send_sem=send_sems.at[a * (N_DEV - 1) + k],
                recv_sem=recv_sems.at[a * (N_DEV - 1) + k],
                device_id=pos, device_id_type=MESH))
    return out


def _exchange(name, arrays, axes, scatter):
    n = len(arrays)

    def body(*refs):
        copies = _all_copies(axes, scatter, refs[:n], refs[n:2 * n], *refs[2 * n:])
        for cp in copies:
            cp.start()
        for cp in copies:
            cp.wait()

    any_spec = pl.BlockSpec(memory_space=pl.ANY)
    return pl.pallas_call(
        body, name=name, in_specs=[any_spec] * n, out_specs=[any_spec] * n,
        out_shape=[jax.ShapeDtypeStruct(_land_shape(a.shape, ax, scatter), a.dtype) for a, ax in zip(arrays, axes)],
        scratch_shapes=_exchange_sems(n),
        compiler_params=pltpu.CompilerParams(has_side_effects=True))(*arrays)


def _all_copies(axes, scatter, in_refs, land_refs, send_sems, recv_sems, local_sems):
    me, _ = _peers()
    copies = _remote_copies(axes, scatter, in_refs, land_refs, send_sems, recv_sems)
    for a, (axis, src, land) in enumerate(zip(axes, in_refs, land_refs)):
        s, d = _copy_ends(axis, scatter, src, land, me, me)
        copies.append(pltpu.make_async_copy(s, d, local_sems.at[a]))
    return copies


def _exchange_sems(n):
    return [pltpu.SemaphoreType.DMA((n * (N_DEV - 1),)), pltpu.SemaphoreType.DMA((n * (N_DEV - 1),)),
            pltpu.SemaphoreType.DMA((n,))]


def _hosted_call(body, ride, *, name, grid, in_specs, out_specs, out_shape, sem, args, scratch_shapes=()):
    if ride is None:
        return pl.pallas_call(body, name=name, grid=grid, in_specs=in_specs, out_specs=out_specs, out_shape=out_shape,
                              scratch_shapes=list(scratch_shapes), compiler_params=_params(sem))(*args), None
    arrays, axes, scatter = ride
    n, n_in, n_out, n_scr = len(arrays), len(in_specs), len(out_specs), len(scratch_shapes)

    def hosted(*refs):
        main_in, ride_in = refs[:n_in], refs[n_in:n_in + n]
        o0 = n_in + n
        main_out, lands = refs[o0:o0 + n_out], refs[o0 + n_out:o0 + n_out + n]
        s0 = o0 + n_out + n
        main_scr, sems = refs[s0:s0 + n_scr], refs[s0 + n_scr:]
        ids = [pl.program_id(i) for i in range(len(grid))]
        first = functools.reduce(jnp.logical_and, [i == 0 for i in ids])
        last = functools.reduce(jnp.logical_and, [i == g - 1 for i, g in zip(ids, grid)])

        @pl.when(first)
        def _():
            for cp in _all_copies(axes, scatter, ride_in, lands, *sems):
                cp.start()

        body(*main_in, *main_out, *main_scr)

        @pl.when(last)
        def _():
            for cp in _all_copies(axes, scatter, ride_in, lands, *sems):
                cp.wait()

    any_spec = pl.BlockSpec(memory_space=pl.ANY)
    res = pl.pallas_call(
        hosted, name=name, grid=grid, in_specs=list(in_specs) + [any_spec] * n,
        out_specs=list(out_specs) + [any_spec] * n,
        out_shape=list(out_shape) + [jax.ShapeDtypeStruct(_land_shape(a.shape, ax, scatter), a.dtype)
                                     for a, ax in zip(arrays, axes)],
        scratch_shapes=list(scratch_shapes) + _exchange_sems(n),
        compiler_params=pltpu.CompilerParams(dimension_semantics=("arbitrary",) * len(grid),
                                             vmem_limit_bytes=VMEM_LIMIT, has_side_effects=True))(*args, *arrays)
    return res[:n_out], res[n_out:]


def _row(vec, width=None):
    vec = vec.reshape(1, -1)
    if width is not None and vec.shape[1] < width:
        vec = jnp.pad(vec, ((0, 0), (0, width - vec.shape[1])))
    return vec


def _riding(rides, lands, own=None):
    def run(key, fn, *args, **kwargs):
        if rides is None or key not in rides:
            return fn(*args, **kwargs)
        arrays, axes, scatter = rides[key]
        arrays = [own[a] if isinstance(a, str) else a for a in arrays]
        res, lands[key] = fn(*args, ride=(arrays, axes, scatter), **kwargs)
        return res
    return run


def _layer_fwd(l, x, sp, bp, rides=None, late=None):
    n_heads = x.shape[1] // (2 * HEAD_DIM)
    lands = {}
    run = _riding(rides, lands)

    def weight(k):
        if k not in bp:
            bp[k] = lands[late[k][0]][late[k][1]]
        return bp[k]

    proj, h = run("in_proj", _mm_nn, f"l{l}_in_proj", [x], [_row(sp["attn_norm"])], bp["w_in"])
    qg, kg, bf = _row(sp["q_norm"]), _row(sp["k_norm"]), _row(sp["b_forget"], LANES)
    pa, c = _prep_fwd(f"l{l}_prep", proj, qg, kg, bf, n_heads)
    ct = c[:, :n_heads].T
    o_a, lse = run("fox", _fox_fwd, f"l{l}_fox", pa, c, ct, n_heads)
    o_b, tot, kst = run("sb", _sb_fwd, f"l{l}_sb", pa, n_heads)
    gfox, gsb = _row(sp["out_norm_fox"]), _row(sp["out_norm_sb"])
    x1, merged = run("out_proj", _mm_nn, f"l{l}_out_proj", [o_a, o_b], [gfox, gsb], weight("w_out"), resid=x)
    u, h2 = run("up_proj", _mm_nn, f"l{l}_up_proj", [x1], [_row(sp["ffn_norm"])], weight("w_up"))
    cb = _row(sp["conv_b"])
    g = _conv_gate_fwd(f"l{l}_conv_gate", u, bp["conv_w"], cb)
    x2 = run("down_proj", _mm_nn, f"l{l}_down_proj", [g], [], weight("w_down"), resid=x1, tn_cap=512)
    saved = dict(x=x, h=h, proj=proj, pa=pa, c=c, ct=ct, o_a=o_a, lse=lse, o_b=o_b, tot=tot, kst=kst, merged=merged, x1=x1,
                 h2=h2, u=u, g=g, qg=qg, kg=kg, bf=bf, gfox=gfox, gsb=gsb, cb=cb)
    return x2, saved, lands


def _layer_bwd(l, dx, sp, bp, sv, rides=None):
    n_heads = dx.shape[1] // (2 * HEAD_DIM)
    gr, lands = {}, {}
    run = _riding(rides, lands, gr)
    dg = run("d_down_act", _mm_nt, f"l{l}_d_down_act", dx, bp["w_down"])
    gr["w_down"] = _mm_tn(f"l{l}_d_w_down", sv["g"], dx)
    duc, dcw, dcb = _conv_gate_bwd(f"l{l}_d_conv_gate", sv["u"], dg, bp["conv_w"], sv["cb"])
    gr["conv_w"] = dcw.transpose(1, 0, 2).reshape(3, -1)
    gr["conv_b"] = dcb.reshape(-1)
    du = _conv_t(f"l{l}_d_conv", duc, bp["conv_w"])
    gr["w_up"] = run("d_w_up", _mm_tn, f"l{l}_d_w_up", sv["h2"], du)
    dx1, dffn = run("d_up_act", _mm_nt, f"l{l}_d_up_act", du, bp["w_up"], "rms_bwd", [sv["x1"]],
                    [_row(sp["ffn_norm"])], dres=dx)
    gr["ffn_norm"] = dffn.reshape(-1)
    gr["w_out"] = _mm_tn(f"l{l}_d_w_out", sv["merged"], dx1)
    do_a, do_b, dgfox, dgsb = _mm_nt(f"l{l}_d_out_act", dx1, bp["w_out"], "rms2_bwd",
                                     [sv["o_a"], sv["o_b"]], [sv["gfox"], sv["gsb"]])
    gr["out_norm_fox"], gr["out_norm_sb"] = dgfox.reshape(-1), dgsb.reshape(-1)
    dq_a, dk_a, dv_a, dct, dcq = _fox_bwd(f"l{l}_d_fox", sv["pa"], sv["c"], sv["ct"], sv["o_a"], do_a, sv["lse"],
                                          n_heads)
    d_sb = _sb_bwd(f"l{l}_d_sb", sv["pa"], do_b, sv["tot"], sv["kst"], n_heads)
    dc = jnp.pad((dct + dcq[:, :, 0]).T, ((0, 0), (0, LANES - n_heads)))
    dproj, dqg, dkg, dbf = _prep_bwd(f"l{l}_d_prep", sv["proj"], sv["qg"], sv["kg"], sv["bf"],
                                     (dq_a, dk_a, dv_a), d_sb, dc, n_heads)
    gr["q_norm"], gr["k_norm"], gr["b_forget"] = dqg.reshape(-1), dkg.reshape(-1), dbf.reshape(-1)[:n_heads]
    gr["w_in"] = run("d_w_in", _mm_tn, f"l{l}_d_w_in", sv["h"], dproj)
    dx0, dattn = run("d_in_act", _mm_nt, f"l{l}_d_in_act", dproj, bp["w_in"], "rms_bwd", [sv["x"]],
                     [_row(sp["attn_norm"])], dres=dx1)
    gr["attn_norm"] = dattn.reshape(-1)
    return dx0, gr, lands


def _local_step(x, tgt, small, big):
    n_layers = len(big)
    saved = []
    for l in range(n_layers):
        x, sv, _ = _layer_fwd(l, x, small[l], big[l])
        saved.append(sv)
    dx, loss_part = _loss_head("loss_head", x, tgt)
    grads = [None] * n_layers
    for l in reversed(range(n_layers)):
        dx, grads[l], _ = _layer_bwd(l, dx, small[l], big[l], saved[l])
    return loss_part, dx, grads


def _w_in_to_internal(w, n_heads):
    w3 = 3 * n_heads * HEAD_DIM
    pad = jnp.zeros(w.shape[:-1] + (LANES - n_heads,), w.dtype)
    return jnp.concatenate([w[..., :w3], w[..., w3 + n_heads:], w[..., w3:w3 + n_heads], pad], axis=-1)


def _w_in_from_internal(w, n_heads):
    w3 = 3 * n_heads * HEAD_DIM
    return jnp.concatenate([w[..., :w3], w[..., 2 * w3:2 * w3 + n_heads], w[..., w3:2 * w3]], axis=-1)


SMALL = ("attn_norm", "b_forget", "q_norm", "k_norm", "out_norm_fox", "out_norm_sb", "ffn_norm", "conv_b")
BIG = ("w_in", "w_out", "w_up", "w_down")
WEIGHTS = ("attn_norm", "w_in", "b_forget", "q_norm", "k_norm", "out_norm_fox", "out_norm_sb", "w_out", "ffn_norm",
           "w_up", "conv_w", "conv_b", "w_down")


def kernel(x, attn_norm, w_in, b_forget, q_norm, k_norm, out_norm_fox, out_norm_sb, w_out, ffn_norm, w_up, conv_w, conv_b, w_down, loss_target, m_attn_norm, m_w_in, m_b_forget, m_q_norm, m_k_norm, m_out_norm_fox, m_out_norm_sb, m_w_out, m_ffn_norm, m_w_up, m_conv_w, m_conv_b, m_w_down, v_attn_norm, v_w_in, v_b_forget, v_q_norm, v_k_norm, v_out_norm_fox, v_out_norm_sb, v_w_out, v_ffn_norm, v_w_up, v_conv_w, v_conv_b, v_w_down):
    w = dict(attn_norm=attn_norm, w_in=w_in, b_forget=b_forget, q_norm=q_norm, k_norm=k_norm,
             out_norm_fox=out_norm_fox, out_norm_sb=out_norm_sb, w_out=w_out, ffn_norm=ffn_norm, w_up=w_up,
             conv_w=conv_w, conv_b=conv_b, w_down=w_down)
    mom = dict(attn_norm=m_attn_norm, w_in=m_w_in, b_forget=m_b_forget, q_norm=m_q_norm, k_norm=m_k_norm,
               out_norm_fox=m_out_norm_fox, out_norm_sb=m_out_norm_sb, w_out=m_w_out, ffn_norm=m_ffn_norm,
               w_up=m_w_up, conv_w=m_conv_w, conv_b=m_conv_b, w_down=m_w_down)
    var = dict(attn_norm=v_attn_norm, w_in=v_w_in, b_forget=v_b_forget, q_norm=v_q_norm, k_norm=v_k_norm,
               out_norm_fox=v_out_norm_fox, out_norm_sb=v_out_norm_sb, w_out=v_w_out, ffn_norm=v_ffn_norm,
               w_up=v_w_up, conv_w=v_conv_w, conv_b=v_conv_b, w_down=v_w_down)
    n_layers, d = attn_norm.shape
    n_heads = d // (2 * HEAD_DIM)
    me = 4 * lax.axis_index("x") + 2 * lax.axis_index("y") + lax.axis_index("c")

    shard = {k: w[k].astype(BF16) for k in BIG}
    axis_of = dict(w_in=None, w_out=0, w_up=1, w_down=0)
    host_fwd = dict(w_in="in_proj", w_out="out_proj", w_up="up_proj", w_down="down_proj")
    host_bwd = dict(w_in="d_up_act", w_out="d_down_act", w_up="d_w_up", w_down="d_in_act")
    whole_w_in = lambda g: _w_in_to_internal(g.transpose(1, 0, 2).reshape(d, -1), n_heads)
    g_in, full_cw = _exchange("gather_first", [shard["w_in"][0], w["conv_w"]], [None, 2], scatter=False)
    small = [{k: w[k][l] for k in SMALL} for l in range(n_layers)]
    big = [dict(w_in=whole_w_in(g_in), conv_w=full_cw[0])]

    saved = [None] * n_layers
    act = x[0]
    for l in range(n_layers):
        rides, late = {}, None
        if l + 1 < n_layers:
            rides = {host_fwd[k]: ([shard[k][l + 1]], [axis_of[k]], False) for k in BIG}
        if l == 0:
            late = dict(w_out=("in_proj", -1), w_up=("fox", -1), w_down=("sb", -1))
            for k, (host, _) in late.items():
                arrays, axes, _ = rides.get(host, ([], [], False))
                rides[host] = (arrays + [shard[k][0]], axes + [axis_of[k]], False)
        act, saved[l], lands = _layer_fwd(l, act, small[l], big[l], rides, late)
        if l + 1 < n_layers:
            nxt = {k: lands[host_fwd[k]][0] for k in BIG}
            big.append(dict(nxt, w_in=whole_w_in(nxt["w_in"]), conv_w=full_cw[l + 1]))
    dx, loss_part = _loss_head("loss_head", act, loss_target[0])
    grads, parts, sends = [None] * n_layers, [None] * n_layers, None
    for l in reversed(range(n_layers)):
        rides = {}
        if sends is not None:
            rides = {host_bwd[k]: ([sends[k]], [axis_of[k]], True) for k in BIG}
        if l == 0:
            for host, k in (("d_up_act", "w_down"), ("d_in_act", "w_out"), ("d_w_in", "w_up")):
                arrays, axes, _ = rides.get(host, ([], [], True))
                rides[host] = (arrays + [k], axes + [axis_of[k]], True)
        dx, grads[l], lands = _layer_bwd(l, dx, small[l], big[l], saved[l], rides)
        if sends is not None:
            parts[l + 1] = [lands[host_bwd[k]][0] for k in BIG]
        g_in = _w_in_from_internal(grads[l]["w_in"], n_heads)
        sends = dict(grads[l], w_in=g_in.reshape(d, N_DEV, -1).transpose(1, 0, 2))
    grad_x = dx
    parts[0] = [None, lands["d_in_act"][-1], lands["d_w_in"][-1], lands["d_up_act"][-1]]

    stack = lambda k: jnp.stack([grads[l][k] for l in range(n_layers)])
    small_names = SMALL + ("conv_w",)
    flat = jnp.concatenate([loss_part.reshape(-1)] + [
        jnp.pad(stack(k).reshape(-1), (0, (-stack(k).size) % LANES)) for k in small_names])
    flat = flat.reshape(-1, LANES)
    out = {}

    def update(name, parts_list, ride=None):
        rc = (-1, parts_list[0].shape[-1])
        res = _adamw("adamw_" + name, parts_list, w[name].reshape(rc), mom[name].reshape(rc), var[name].reshape(rc),
                     ride=ride)
        res, landed = res if ride is not None else (res, None)
        out[name] = [r.reshape(w[name].shape) for r in res]
        return landed

    layers_of = lambda name: [parts[l][BIG.index(name)] for l in range(n_layers)]
    (parts[0][0],) = update("w_up", layers_of("w_up"), ride=([sends["w_in"]], [None], True))
    (all_small,) = update("w_down", layers_of("w_down"), ride=([flat], [None], False))
    update("w_out", layers_of("w_out"))
    update("w_in", layers_of("w_in"))
    n_rows = flat.shape[0]
    w_flat, m_flat, v_flat = [], [], []
    for src, dst in ((w, w_flat), (mom, m_flat), (var, v_flat)):
        dst.append(jnp.zeros((LANES,), F32))
        for k in small_names:
            a = src[k]
            if k == "conv_w":
                a = jnp.zeros((n_layers, 3, conv_w.shape[2] * N_DEV), F32)
            dst.append(jnp.pad(a.reshape(-1), (0, (-a.size) % LANES)))
    pack = lambda parts: jnp.concatenate(parts).reshape(n_rows, LANES)
    res = _adamw("adamw_small", [all_small], pack(w_flat), pack(m_flat), pack(v_flat))
    res = [r.reshape(-1) for r in res]
    loss = res[0][0]
    off = LANES
    g_cw_full = None
    for k in small_names:
        size = n_layers * 3 * conv_w.shape[2] * N_DEV if k == "conv_w" else w[k].size
        if k == "conv_w":
            g_cw_full = res[0][off:off + size].reshape(n_layers, 3, -1)
        else:
            out[k] = [r[off:off + size].reshape(w[k].shape) for r in res]
        off += size + (-size) % LANES
    c_loc = conv_w.shape[2]
    g_cw_mine = lax.dynamic_slice_in_dim(g_cw_full, me * c_loc, c_loc, axis=2)
    update("conv_w", [g_cw_mine.reshape(1, n_layers * 3, c_loc)])

    outs = [loss, grad_x[None]]
    for n in range(4):
        outs += [out[k][n] for k in WEIGHTS]
    return tuple(outs)
```

```python
import functools

import jax
import jax.numpy as jnp
from jax import lax
from jax.experimental import pallas as pl
from jax.experimental.pallas import tpu as pltpu

F32 = jnp.float32
BF16 = jnp.bfloat16
HEAD_DIM = 128
QK_SCALE = HEAD_DIM ** -0.5
SKIP = 110.0
BOUND_SLACK = 1.0 + 2.0 ** -6
LANES = 128
EPS = 1e-6
N_DEV = 8
ADAM_LR = 0.001
ADAM_B1 = 0.9
ADAM_B2 = 0.999
ADAM_EPS = 1e-08
ADAM_WD = 0.01
ADAM_STEP = 10
VMEM_LIMIT = 56 * 1024 * 1024
ADAMW_VMEM = 24 * 1024 * 1024
MESH = pl.DeviceIdType.MESH

NT_DIMS = (((1,), (1,)), ((), ()))
TN_DIMS = (((0,), (0,)), ((), ()))


def _tile(n, cap, mult=LANES):
    t = (min(cap, n) // mult) * mult
    while t >= mult:
        if n % t == 0:
            return t
        t -= mult
    return n


def _params(sem, vmem=VMEM_LIMIT):
    return pltpu.CompilerParams(dimension_semantics=sem, vmem_limit_bytes=vmem)


def _split_dot(x, u, n_split, x_left=True):
    acc = None
    rest = x
    for s in range(n_split):
        piece = rest.astype(BF16)
        if s + 1 < n_split:
            rest = rest - piece.astype(F32)
        d = (jnp.dot(piece, u, preferred_element_type=F32) if x_left
             else jnp.dot(u, piece, preferred_element_type=F32))
        acc = d if acc is None else acc + d
    return acc


def _tri(n, kind):
    r = lax.broadcasted_iota(jnp.int32, (n, n), 0)
    c = lax.broadcasted_iota(jnp.int32, (n, n), 1)
    return jnp.where(r >= c if kind == "ge" else r <= c, 1.0, 0.0).astype(BF16)


def _mm_nn(name, a_list, g_list, w, resid=None, tm=1024, tn_cap=1024, ride=None):
    s_len = a_list[0].shape[0]
    k_dim, n_dim = w.shape
    tm = _tile(s_len, tm, 8)
    tn = _tile(n_dim, tn_cap)
    normed = bool(g_list)

    def body(*refs):
        refs = list(refs)
        a_refs = [refs.pop(0) for _ in a_list]
        g_refs = [refs.pop(0) for _ in g_list]
        w_ref = refs.pop(0)
        r_ref = refs.pop(0) if resid is not None else None
        o_ref = refs.pop(0)
        if normed:
            h_ref = refs.pop(0)

            @pl.when(pl.program_id(1) == 0)
            def _():
                off = 0
                for a_ref, g_ref in zip(a_refs, g_refs):
                    xv = a_ref[...]
                    kk = xv.shape[1]
                    r = lax.rsqrt(jnp.mean(xv * xv, axis=1, keepdims=True) + EPS)
                    h_ref[:, off:off + kk] = (xv * r * g_ref[...]).astype(BF16)
                    off += kk

            a = h_ref[...]
        else:
            a = a_refs[0][...]
        acc = jnp.dot(a, w_ref[...], preferred_element_type=F32)
        if r_ref is not None:
            acc = acc + r_ref[...]
        o_ref[...] = acc

    in_specs = [pl.BlockSpec((tm, a.shape[1]), lambda i, j: (i, 0)) for a in a_list]
    in_specs += [pl.BlockSpec((1, g.shape[1]), lambda i, j: (0, 0)) for g in g_list]
    in_specs += [pl.BlockSpec((k_dim, tn), lambda i, j: (0, j))]
    args = list(a_list) + list(g_list) + [w]
    if resid is not None:
        in_specs.append(pl.BlockSpec((tm, tn), lambda i, j: (i, j)))
        args.append(resid)
    out_shape = [jax.ShapeDtypeStruct((s_len, n_dim), F32)]
    out_specs = [pl.BlockSpec((tm, tn), lambda i, j: (i, j))]
    if normed:
        out_shape.append(jax.ShapeDtypeStruct((s_len, k_dim), BF16))
        out_specs.append(pl.BlockSpec((tm, k_dim), lambda i, j: (i, 0)))
    res, lands = _hosted_call(body, ride, name=name, grid=(s_len // tm, n_dim // tn), in_specs=in_specs,
                              out_specs=out_specs, out_shape=out_shape, sem=("parallel", "arbitrary"), args=args)
    res = res if normed else res[0]
    return res if ride is None else (res, lands)


def _rms_bwd(dh, xv, gv, r=None):
    if r is None:
        r = lax.rsqrt(jnp.mean(xv * xv, axis=1, keepdims=True) + EPS)
    xhat = xv * r
    dxh = dh * gv
    dx = r * (dxh - xhat * jnp.mean(dxh * xhat, axis=1, keepdims=True))
    return dx, dh * xhat


def _mm_nt(name, da, w, mode="plain", xs=(), gs=(), dres=None, tm=512, tk_cap=1024, ride=None):
    s_len, kc = da.shape
    n_out = w.shape[0]
    tm = _tile(s_len, tm, 8)
    tk = _tile(kc, tk_cap)
    nk = kc // tk

    def body(*refs):
        refs = list(refs)
        da_ref, w_ref = refs.pop(0), refs.pop(0)
        x_refs = [refs.pop(0) for _ in xs]
        g_refs = [refs.pop(0) for _ in gs]
        dres_ref = refs.pop(0) if dres is not None else None
        acc_ref = refs[0] if mode == "plain" else refs.pop()
        out_refs = refs
        i, k = pl.program_id(0), pl.program_id(1)

        @pl.when(k == 0)
        def _():
            acc_ref[...] = jnp.zeros_like(acc_ref)

        acc_ref[...] += lax.dot_general(da_ref[...].astype(BF16), w_ref[...], NT_DIMS,
                                        preferred_element_type=F32)

        if mode == "plain":
            return

        @pl.when(k == nk - 1)
        def _():
            n_x = len(xs)
            dx_refs, dg_refs = out_refs[:n_x], out_refs[n_x:]

            @pl.when(i == 0)
            def _():
                for dg_ref in dg_refs:
                    dg_ref[...] = jnp.zeros_like(dg_ref)

            off = 0
            for x_ref, g_ref, dx_ref, dg_ref in zip(x_refs, g_refs, dx_refs, dg_refs):
                kk = x_ref.shape[1]
                dx, dgp = _rms_bwd(acc_ref[:, off:off + kk], x_ref[...], g_ref[...])
                if dres_ref is not None:
                    dx = dx + dres_ref[...]
                dx_ref[...] = dx
                dg_ref[...] += jnp.sum(dgp, axis=0, keepdims=True)
                off += kk

    in_specs = [pl.BlockSpec((tm, tk), lambda i, k: (i, k)), pl.BlockSpec((n_out, tk), lambda i, k: (0, k))]
    in_specs += [pl.BlockSpec((tm, x.shape[1]), lambda i, k: (i, 0)) for x in xs]
    in_specs += [pl.BlockSpec((1, g.shape[1]), lambda i, k: (0, 0)) for g in gs]
    args = [da, w] + list(xs) + list(gs)
    if dres is not None:
        in_specs.append(pl.BlockSpec((tm, n_out), lambda i, k: (i, 0)))
        args.append(dres)
    if mode == "plain":
        out_shape = [jax.ShapeDtypeStruct((s_len, n_out), F32)]
        out_specs = [pl.BlockSpec((tm, n_out), lambda i, k: (i, 0))]
    else:
        out_shape = [jax.ShapeDtypeStruct((s_len, x.shape[1]), F32) for x in xs]
        out_specs = [pl.BlockSpec((tm, x.shape[1]), lambda i, k: (i, 0)) for x in xs]
        out_shape += [jax.ShapeDtypeStruct((1, x.shape[1]), F32) for x in xs]
        out_specs += [pl.BlockSpec((1, x.shape[1]), lambda i, k: (0, 0)) for x in xs]
    res, lands = _hosted_call(body, ride, name=name, grid=(s_len // tm, nk), in_specs=in_specs, out_specs=out_specs,
                              out_shape=out_shape, sem=("arbitrary", "arbitrary"), args=args,
                              scratch_shapes=[] if mode == "plain" else [pltpu.VMEM((tm, n_out), F32)])
    res = res[0] if mode == "plain" else res
    return res if ride is None else (res, lands)


def _mm_tn(name, a, b, tk_cap=1024, tn_cap=2048, tm=1024, ride=None):
    s_len, k_dim = a.shape
    n_dim = b.shape[1]
    tk = _tile(k_dim, tk_cap)
    tn = _tile(n_dim, tn_cap)
    tm = _tile(s_len, tm, 8)
    nm = s_len // tm

    def body(a_ref, b_ref, o_ref, acc_ref):
        m = pl.program_id(2)

        @pl.when(m == 0)
        def _():
            acc_ref[...] = jnp.zeros_like(acc_ref)

        acc_ref[...] += lax.dot_general(a_ref[...].astype(BF16), b_ref[...].astype(BF16), TN_DIMS,
                                        preferred_element_type=F32)

        @pl.when(m == nm - 1)
        def _():
            o_ref[...] = acc_ref[...].astype(BF16)

    res, lands = _hosted_call(
        body, ride, name=name, grid=(k_dim // tk, n_dim // tn, nm),
        in_specs=[pl.BlockSpec((tm, tk), lambda i, j, m: (m, i)), pl.BlockSpec((tm, tn), lambda i, j, m: (m, j))],
        out_specs=[pl.BlockSpec((tk, tn), lambda i, j, m: (i, j))],
        out_shape=[jax.ShapeDtypeStruct((k_dim, n_dim), BF16)],
        scratch_shapes=[pltpu.VMEM((tk, tn), F32)], sem=("parallel", "parallel", "arbitrary"), args=[a, b])
    return res[0] if ride is None else (res[0], lands)


def _neg_softplus(z):
    e = jnp.exp(-jnp.abs(z))
    return -(jnp.maximum(z, 0.0) + jnp.log(1.0 + e)), e


def _prep_fwd(name, proj, qg, kg, bf, n_heads, tm=256):
    s_len, n_p = proj.shape
    w_dim = n_heads * HEAD_DIM
    tm = _tile(s_len, tm, 8)

    def body(p_ref, qg_ref, kg_ref, bf_ref, pa_ref, c_ref, carry_ref):
        @pl.when(pl.program_id(0) == 0)
        def _():
            carry_ref[...] = jnp.zeros_like(carry_ref)

        for base, g_ref, mul in ((0, qg_ref, QK_SCALE), (w_dim, kg_ref, None)):
            for hh in range(n_heads):
                sl = slice(base + hh * HEAD_DIM, base + (hh + 1) * HEAD_DIM)
                xv = p_ref[:, sl]
                r = lax.rsqrt(jnp.mean(xv * xv, axis=1, keepdims=True) + EPS)
                y = xv * r * g_ref[...]
                pa_ref[:, sl] = (y if mul is None else y * mul).astype(BF16)
        pa_ref[:, 2 * w_dim:3 * w_dim] = p_ref[:, 2 * w_dim:3 * w_dim].astype(BF16)
        pa_ref[:, 3 * w_dim:4 * w_dim] = (p_ref[:, 3 * w_dim:4 * w_dim] * QK_SCALE).astype(BF16)
        pa_ref[:, 4 * w_dim:] = p_ref[:, 4 * w_dim:6 * w_dim].astype(BF16)
        f = p_ref[:, 6 * w_dim:] + bf_ref[...]
        lf, _ = _neg_softplus(-f)
        lane = lax.broadcasted_iota(jnp.int32, lf.shape, 1)
        lf = jnp.where(lane < n_heads, lf, 0.0)
        cb = _split_dot(lf, _tri(tm, "ge"), 3, x_left=False) + carry_ref[...]
        c_ref[...] = cb
        carry_ref[...] = cb[tm - 1:tm, :]

    return pl.pallas_call(
        body, name=name, grid=(s_len // tm,),
        in_specs=[pl.BlockSpec((tm, n_p), lambda i: (i, 0))] + [pl.BlockSpec((1, LANES), lambda i: (0, 0))] * 3,
        out_specs=[pl.BlockSpec((tm, 6 * w_dim), lambda i: (i, 0)), pl.BlockSpec((tm, LANES), lambda i: (i, 0))],
        out_shape=[jax.ShapeDtypeStruct((s_len, 6 * w_dim), BF16), jax.ShapeDtypeStruct((s_len, LANES), F32)],
        scratch_shapes=[pltpu.VMEM((1, LANES), F32)],
        compiler_params=_params(("arbitrary",)))(proj, qg, kg, bf)


def _prep_bwd(name, proj, qg, kg, bf, d_fox, d_sb, dc, n_heads, tm=256):
    s_len, n_p = proj.shape
    w_dim = n_heads * HEAD_DIM
    tm = _tile(s_len, tm, 8)
    nb = s_len // tm

    def body(p_ref, qg_ref, kg_ref, bf_ref, dqa_ref, dka_ref, dva_ref, dqb_ref, dkb_ref, dvb_ref, dc_ref,
             dp_ref, dqg_ref, dkg_ref, dbf_ref, carry_ref):
        @pl.when(pl.program_id(0) == 0)
        def _():
            for ref in (carry_ref, dqg_ref, dkg_ref, dbf_ref):
                ref[...] = jnp.zeros_like(ref)

        for base, g_ref, d_ref, dg_ref in ((0, qg_ref, dqa_ref, dqg_ref), (w_dim, kg_ref, dka_ref, dkg_ref)):
            dg = jnp.zeros((1, HEAD_DIM), F32)
            for hh in range(n_heads):
                sl = slice(base + hh * HEAD_DIM, base + (hh + 1) * HEAD_DIM)
                dx, dgp = _rms_bwd(d_ref[:, hh * HEAD_DIM:(hh + 1) * HEAD_DIM], p_ref[:, sl], g_ref[...])
                dp_ref[:, sl] = dx.astype(BF16)
                dg = dg + jnp.sum(dgp, axis=0, keepdims=True)
            dg_ref[...] += dg
        for n, d_ref in enumerate((dva_ref, dqb_ref, dkb_ref, dvb_ref)):
            dp_ref[:, (2 + n) * w_dim:(3 + n) * w_dim] = d_ref[...].astype(BF16)
        dlf = _split_dot(dc_ref[...], _tri(tm, "le"), 3, x_left=False) + carry_ref[...]
        carry_ref[...] = dlf[0:1, :]
        f = p_ref[:, 6 * w_dim:] + bf_ref[...]
        e = jnp.exp(-jnp.abs(f))
        sig_neg = jnp.where(f >= 0, e, 1.0) / (1.0 + e)
        lane = lax.broadcasted_iota(jnp.int32, f.shape, 1)
        df = jnp.where(lane < n_heads, dlf * sig_neg, 0.0)
        dp_ref[:, 6 * w_dim:] = df.astype(BF16)
        dbf_ref[...] += jnp.sum(df, axis=0, keepdims=True)

    rev = lambda i: (nb - 1 - i, 0)
    vec = pl.BlockSpec((1, LANES), lambda i: (0, 0))
    return pl.pallas_call(
        body, name=name, grid=(nb,),
        in_specs=[pl.BlockSpec((tm, n_p), rev), vec, vec, vec] + [pl.BlockSpec((tm, w_dim), rev)] * 6
        + [pl.BlockSpec((tm, LANES), rev)],
        out_specs=[pl.BlockSpec((tm, n_p), rev), vec, vec, vec],
        out_shape=[jax.ShapeDtypeStruct((s_len, n_p), BF16)] + [jax.ShapeDtypeStruct((1, LANES), F32)] * 3,
        scratch_shapes=[pltpu.VMEM((1, LANES), F32)],
        compiler_params=_params(("arbitrary",)))(proj, qg, kg, bf, *d_fox, *d_sb, dc)


def _head_col(c_blk, h):
    lane = lax.broadcasted_iota(jnp.int32, c_blk.shape, 1)
    return jnp.sum(jnp.where(lane == h, c_blk, 0.0), axis=1, keepdims=True)


def _key_norm_max(k_ref):
    kf = k_ref[...].astype(F32)
    return jnp.sqrt(jnp.max(jnp.sum(kf * kf, axis=1, keepdims=True), axis=0, keepdims=True))


def _logit_bound(q, kmax):
    qf = q.astype(F32)
    return jnp.sqrt(jnp.sum(qf * qf, axis=1, keepdims=True)) * kmax * BOUND_SLACK


def _c_block_end(ct_ref, h, kb, tk):
    return jnp.min(ct_ref[pl.ds(h, 1), pl.ds(pl.multiple_of(kb * tk, tk), tk)])


def _fox_fwd(name, pa, c, ct, n_heads, tq=1024, ride=None):
    s_len = pa.shape[0]
    tq = _tile(s_len, tq, LANES)
    hp = ct.shape[0]

    def body(q_ref, k_ref, v_ref, c_ref, ct_ref, o_ref, lse_ref, kmax_ref):
        h, qi = pl.program_id(0), pl.program_id(1)

        @pl.when(qi == 0)
        def _():
            kmax_ref[...] = _key_norm_max(k_ref)

        q = q_ref[...]
        cq = _head_col(c_ref[...], h)
        top = _logit_bound(q, kmax_ref[...]) + cq
        row = lax.broadcasted_iota(jnp.int32, (tq, tq), 0)
        col = lax.broadcasted_iota(jnp.int32, (tq, tq), 1)

        def step(kb, carry, masked):
            m, l, acc = carry
            ks = pl.multiple_of(kb * tq, tq)
            k = k_ref[pl.ds(ks, tq), :]
            v = v_ref[pl.ds(ks, tq), :]
            ck = ct_ref[pl.ds(h, 1), pl.ds(ks, tq)]
            s = lax.dot_general(q, k, NT_DIMS, preferred_element_type=F32) + (cq - ck)
            if masked:
                s = jnp.where(col <= row, s, -jnp.inf)
            m_new = jnp.maximum(m, jnp.max(s, axis=1, keepdims=True))
            alpha = jnp.exp(m - m_new)
            p = jnp.exp(s - m_new)
            l = alpha * l + jnp.sum(p, axis=1, keepdims=True)
            acc = alpha * acc + jnp.dot(p.astype(BF16), v, preferred_element_type=F32)
            return m_new, l, acc

        def margin(kb, m):
            return jnp.max(top - m) - _c_block_end(ct_ref, h, jnp.maximum(kb, 0), tq)

        def walk(state):
            kb, _, m, l, acc = state
            m, l, acc = step(kb, (m, l, acc), False)
            return kb - 1, margin(kb - 1, m), m, l, acc

        init = (jnp.full((tq, 1), -jnp.inf, F32), jnp.zeros((tq, 1), F32), jnp.zeros((tq, HEAD_DIM), F32))
        carry = step(qi, init, True)
        state = (qi - 1, margin(qi - 1, carry[0])) + carry
        _, _, m, l, acc = lax.while_loop(lambda st: (st[0] >= 0) & (st[1] > -SKIP), walk, state)
        o_ref[...] = acc / l
        lse_ref[0] = m + jnp.log(l)

    res, lands = _hosted_call(
        body, ride, name=name, grid=(n_heads, s_len // tq),
        in_specs=[pl.BlockSpec((tq, HEAD_DIM), lambda h, i: (i, h)),
                  pl.BlockSpec((s_len, HEAD_DIM), lambda h, i: (0, n_heads + h)),
                  pl.BlockSpec((s_len, HEAD_DIM), lambda h, i: (0, 2 * n_heads + h)),
                  pl.BlockSpec((tq, LANES), lambda h, i: (i, 0)),
                  pl.BlockSpec((hp, s_len), lambda h, i: (0, 0))],
        out_specs=[pl.BlockSpec((tq, HEAD_DIM), lambda h, i: (i, h)),
                   pl.BlockSpec((1, tq, 1), lambda h, i: (h, i, 0))],
        out_shape=[jax.ShapeDtypeStruct((s_len, n_heads * HEAD_DIM), F32),
                   jax.ShapeDtypeStruct((n_heads, s_len, 1), F32)],
        scratch_shapes=[pltpu.VMEM((1, 1), F32)], sem=("parallel", "arbitrary"), args=[pa, pa, pa, c, ct])
    return res if ride is None else (res, lands)


def _fox_bwd(name, pa, c, ct, o, do, lse, n_heads, tq=1024):
    s_len = pa.shape[0]
    tq = _tile(s_len, tq, LANES)
    hp = ct.shape[0]
    w_dim = n_heads * HEAD_DIM

    def body(q_ref, k_ref, v_ref, c_ref, ct_ref, o_ref, do_ref, lse_ref, dq_ref, dk_ref, dv_ref, dct_ref, dcq_ref,
             kmax_ref):
        h, qi = pl.program_id(0), pl.program_id(1)

        @pl.when(qi == 0)
        def _():
            dk_ref[...] = jnp.zeros_like(dk_ref)
            dv_ref[...] = jnp.zeros_like(dv_ref)
            kmax_ref[...] = _key_norm_max(k_ref)

        @pl.when((qi == 0) & (h == 0))
        def _():
            dct_ref[...] = jnp.zeros_like(dct_ref)

        q = q_ref[...]
        do32 = do_ref[...]
        dob = do32.astype(BF16)
        dsum = jnp.sum(do32 * o_ref[...], axis=1, keepdims=True)
        lse_v = lse_ref[0]
        cq = _head_col(c_ref[...], h)
        row = lax.broadcasted_iota(jnp.int32, (tq, tq), 0)
        col = lax.broadcasted_iota(jnp.int32, (tq, tq), 1)

        def step(kb, carry, masked):
            dq, dcq = carry
            ks = pl.multiple_of(kb * tq, tq)
            k = k_ref[pl.ds(ks, tq), :]
            v = v_ref[pl.ds(ks, tq), :]
            ck = ct_ref[pl.ds(h, 1), pl.ds(ks, tq)]
            s = lax.dot_general(q, k, NT_DIMS, preferred_element_type=F32) + (cq - ck)
            p = jnp.exp(s - lse_v)
            if masked:
                p = jnp.where(col <= row, p, 0.0)
            dp = lax.dot_general(dob, v, NT_DIMS, preferred_element_type=F32)
            ds = p * (dp - dsum)
            dsb = ds.astype(BF16)
            dk_ref[pl.ds(ks, tq), :] += lax.dot_general(dsb, q, TN_DIMS, preferred_element_type=F32)
            dv_ref[pl.ds(ks, tq), :] += lax.dot_general(p.astype(BF16), dob, TN_DIMS, preferred_element_type=F32)
            dct_ref[pl.ds(h, 1), pl.ds(ks, tq)] -= jnp.sum(ds, axis=0, keepdims=True)
            return dq + jnp.dot(dsb, k, preferred_element_type=F32), dcq + jnp.sum(ds, axis=1, keepdims=True)

        top = jnp.max(_logit_bound(q, kmax_ref[...]) + cq - lse_v)

        def margin(kb):
            return top - _c_block_end(ct_ref, h, jnp.maximum(kb, 0), tq)

        def walk(state):
            kb, _, dq, dcq = state
            dq, dcq = step(kb, (dq, dcq), False)
            return kb - 1, margin(kb - 1), dq, dcq

        init = (jnp.zeros((tq, HEAD_DIM), F32), jnp.zeros((tq, 1), F32))
        state = (qi - 1, margin(qi - 1)) + step(qi, init, True)
        _, _, dq, dcq = lax.while_loop(lambda st: (st[0] >= 0) & (st[1] > -SKIP), walk, state)
        dq_ref[...] = dq * QK_SCALE
        dcq_ref[0] = dcq

    blk = pl.BlockSpec((tq, HEAD_DIM), lambda h, i: (i, h))
    full = pl.BlockSpec((s_len, HEAD_DIM), lambda h, i: (0, h))
    return pl.pallas_call(
        body, name=name, grid=(n_heads, s_len // tq),
        in_specs=[blk,
                  pl.BlockSpec((s_len, HEAD_DIM), lambda h, i: (0, n_heads + h)),
                  pl.BlockSpec((s_len, HEAD_DIM), lambda h, i: (0, 2 * n_heads + h)),
                  pl.BlockSpec((tq, LANES), lambda h, i: (i, 0)),
                  pl.BlockSpec((hp, s_len), lambda h, i: (0, 0)),
                  blk, blk,
                  pl.BlockSpec((1, tq, 1), lambda h, i: (h, i, 0))],
        out_specs=[blk, full, full, pl.BlockSpec((hp, s_len), lambda h, i: (0, 0)),
                   pl.BlockSpec((1, tq, 1), lambda h, i: (h, i, 0))],
        out_shape=[jax.ShapeDtypeStruct((s_len, w_dim), F32)] * 3 + [jax.ShapeDtypeStruct((hp, s_len), F32),
                                                                     jax.ShapeDtypeStruct((n_heads, s_len, 1), F32)],
        scratch_shapes=[pltpu.VMEM((1, 1), F32)],
        compiler_params=_params(("arbitrary", "arbitrary")))(pa, pa, pa, c, ct, o, do, lse)


def _sb_fwd(name, pa, n_heads, tq=512, tk=256, ride=None):
    s_len = pa.shape[0]
    tq = _tile(s_len, tq, LANES)
    tk = _tile(tq, tk, LANES)
    nsub = tq // tk

    def body(q_ref, k_ref, v_ref, o_ref, tot_ref, kst_ref, kmax_ref):
        qi = pl.program_id(1)

        @pl.when(qi == 0)
        def _():
            kmax_ref[...] = _key_norm_max(k_ref)

        q = q_ref[...]
        z_bound = _logit_bound(q, kmax_ref[...])
        u = _tri(tk, "ge")
        row = lax.broadcasted_iota(jnp.int32, (tq, tk), 0)
        col = lax.broadcasted_iota(jnp.int32, (tq, tk), 1)

        def block(ks, carry, mask_off, row0=0):
            r, acc = (c[row0:] for c in carry)
            k = k_ref[pl.ds(ks, tk), :]
            v = v_ref[pl.ds(ks, tk), :]
            z = lax.dot_general(q[row0:], k, NT_DIMS, preferred_element_type=F32)
            a, _ = _neg_softplus(z)
            if mask_off is not None:
                valid = col[row0:] + mask_off < row[row0:]
                a = jnp.where(valid, a, 0.0)
            rin = _split_dot(a, u, 2)
            w = jnp.exp(z + (r + rin))
            if mask_off is not None:
                w = jnp.where(valid, w, 0.0)
            new = (r + rin[:, 0:1], acc + jnp.dot(w.astype(BF16), v, preferred_element_type=F32))
            return tuple(jnp.concatenate([c[:row0], n], axis=0) if row0 else n for c, n in zip(carry, new))

        carry = (jnp.zeros((tq, 1), F32), jnp.zeros((tq, HEAD_DIM), F32))
        q0 = pl.multiple_of(qi * tq, tq)
        for j in reversed(range(nsub)):
            carry = block(q0 + j * tk, carry, j * tk, j * tk)
        def walk(state):
            kb, _, r, acc = state
            r, acc = block(pl.multiple_of(kb * tk, tk), (r, acc), None)
            return kb - 1, jnp.max(r + z_bound), r, acc

        state = (qi * nsub - 1, jnp.max(carry[0] + z_bound)) + carry
        kb, _, r, acc = lax.while_loop(lambda st: (st[0] >= 0) & (st[1] > -SKIP), walk, state)
        o_ref[...] = acc
        tot_ref[0] = r
        kst_ref[...] = jnp.full(kst_ref.shape, (kb + 1).astype(F32))

    nq = s_len // tq
    res, lands = _hosted_call(
        body, ride, name=name, grid=(n_heads, nq),
        in_specs=[pl.BlockSpec((tq, HEAD_DIM), lambda h, i: (i, 3 * n_heads + h)),
                  pl.BlockSpec((s_len, HEAD_DIM), lambda h, i: (0, 4 * n_heads + h)),
                  pl.BlockSpec((s_len, HEAD_DIM), lambda h, i: (0, 5 * n_heads + h))],
        out_specs=[pl.BlockSpec((tq, HEAD_DIM), lambda h, i: (i, h)),
                   pl.BlockSpec((1, tq, 1), lambda h, i: (h, i, 0)),
                   pl.BlockSpec((1, 1, 8, LANES), lambda h, i: (h, i, 0, 0))],
        out_shape=[jax.ShapeDtypeStruct((s_len, n_heads * HEAD_DIM), F32),
                   jax.ShapeDtypeStruct((n_heads, s_len, 1), F32),
                   jax.ShapeDtypeStruct((n_heads, nq, 8, LANES), F32)],
        scratch_shapes=[pltpu.VMEM((1, 1), F32)], sem=("parallel", "arbitrary"), args=[pa, pa, pa])
    return res if ride is None else (res, lands)


def _sb_bwd(name, pa, do, tot, kst, n_heads, tq=512, tk=256):
    s_len = pa.shape[0]
    tq = _tile(s_len, tq, LANES)
    tk = _tile(tq, tk, LANES)
    nsub = tq // tk
    w_dim = n_heads * HEAD_DIM

    def body(q_ref, k_ref, v_ref, do_ref, tot_ref, kst_ref, dq_ref, dk_ref, dv_ref):
        qi = pl.program_id(1)

        @pl.when(qi == 0)
        def _():
            dk_ref[...] = jnp.zeros_like(dk_ref)
            dv_ref[...] = jnp.zeros_like(dv_ref)

        q = q_ref[...]
        dob = do_ref[...].astype(BF16)
        u = _tri(tk, "le")
        row = lax.broadcasted_iota(jnp.int32, (tq, tk), 0)
        col = lax.broadcasted_iota(jnp.int32, (tq, tk), 1)

        def block(ks, carry, mask_off, row0=0):
            rem, cpre, dq = (c[row0:] for c in carry)
            qs, dos = q[row0:], dob[row0:]
            k = k_ref[pl.ds(ks, tk), :]
            v = v_ref[pl.ds(ks, tk), :]
            z = lax.dot_general(qs, k, NT_DIMS, preferred_element_type=F32)
            a, e = _neg_softplus(z)
            if mask_off is not None:
                valid = col[row0:] + mask_off < row[row0:]
                a = jnp.where(valid, a, 0.0)
            pin = _split_dot(a, u, 2)
            w = jnp.exp(z + (rem - (pin - a)))
            if mask_off is not None:
                w = jnp.where(valid, w, 0.0)
            g = w * lax.dot_general(dos, v, NT_DIMS, preferred_element_type=F32)
            cin = _split_dot(g, u, 1)
            beta = jnp.where(z >= 0, 1.0, e) / (1.0 + e)
            dz = g - beta * (cpre + cin)
            if mask_off is not None:
                dz = jnp.where(valid, dz, 0.0)
            dzb = dz.astype(BF16)
            dk_ref[pl.ds(ks, tk), :] += lax.dot_general(dzb, qs, TN_DIMS, preferred_element_type=F32)
            dv_ref[pl.ds(ks, tk), :] += lax.dot_general(w.astype(BF16), dos, TN_DIMS, preferred_element_type=F32)
            new = (rem - pin[:, tk - 1:tk], cpre + cin[:, tk - 1:tk], dq + jnp.dot(dzb, k, preferred_element_type=F32))
            return tuple(jnp.concatenate([c[:row0], n], axis=0) if row0 else n for c, n in zip(carry, new))

        carry = (tot_ref[0], jnp.zeros((tq, 1), F32), jnp.zeros((tq, HEAD_DIM), F32))
        first = jnp.max(kst_ref[0, 0]).astype(jnp.int32)
        carry = lax.fori_loop(first, qi * nsub, lambda n, cr: block(pl.multiple_of(n * tk, tk), cr, None), carry)
        q0 = pl.multiple_of(qi * tq, tq)
        for j in range(nsub):
            carry = block(q0 + j * tk, carry, j * tk, j * tk)
        dq_ref[...] = carry[2] * QK_SCALE

    blk = pl.BlockSpec((tq, HEAD_DIM), lambda h, i: (i, h))
    full = pl.BlockSpec((s_len, HEAD_DIM), lambda h, i: (0, h))
    return pl.pallas_call(
        body, name=name, grid=(n_heads, s_len // tq),
        in_specs=[pl.BlockSpec((tq, HEAD_DIM), lambda h, i: (i, 3 * n_heads + h)),
                  pl.BlockSpec((s_len, HEAD_DIM), lambda h, i: (0, 4 * n_heads + h)),
                  pl.BlockSpec((s_len, HEAD_DIM), lambda h, i: (0, 5 * n_heads + h)),
                  blk,
                  pl.BlockSpec((1, tq, 1), lambda h, i: (h, i, 0)),
                  pl.BlockSpec((1, 1, 8, LANES), lambda h, i: (h, i, 0, 0))],
        out_specs=[blk, full, full],
        out_shape=[jax.ShapeDtypeStruct((s_len, w_dim), F32)] * 3,
        compiler_params=_params(("arbitrary", "arbitrary")))(pa, pa, pa, do, tot, kst)


CONV_HEAD = 16
CONV_CHUNK = 32


def _stage_head(ext_ref, u_ref, halo_ref, first):
    ext_ref[0:8, :] = jnp.where(first, 0.0, halo_ref[...])
    ext_ref[8:8 + CONV_HEAD, :] = u_ref[0:CONV_HEAD, :]


def _taps(ref, r0, n):
    return ref[r0:r0 + n, :], ref[r0 - 1:r0 - 1 + n, :], ref[r0 - 2:r0 - 2 + n, :]


def _row_chunks(tm):
    return [(True, 0, CONV_HEAD)] + [(False, r0, min(CONV_CHUNK, tm - r0)) for r0 in range(CONV_HEAD, tm, CONV_CHUNK)]


def _conv3(us, w_ref, b_ref):
    return w_ref[2:3, :] * us[0] + w_ref[1:2, :] * us[1] + w_ref[0:1, :] * us[2] + b_ref[...]


def _fold8(x):
    return jnp.sum(x.reshape(x.shape[0] // 8, 8, x.shape[1]), axis=0)


def _sigmoid(x):
    return 0.5 * jnp.tanh(0.5 * x) + 0.5


def _conv_specs(tm, tc, nj, order):
    hb = tm // 8
    ij = order

    def at(f):
        return lambda *g: f(*ij(*g))

    return [pl.BlockSpec((tm, tc), at(lambda i, j: (i, j))),
            pl.BlockSpec((tm, tc), at(lambda i, j: (i, j + nj))),
            pl.BlockSpec((8, tc), at(lambda i, j: (jnp.maximum(i * hb - 1, 0), j))),
            pl.BlockSpec((8, tc), at(lambda i, j: (jnp.maximum(i * hb - 1, 0), j + nj))),
            pl.BlockSpec((3, tc), at(lambda i, j: (0, j))),
            pl.BlockSpec((3, tc), at(lambda i, j: (0, j + nj))),
            pl.BlockSpec((1, tc), at(lambda i, j: (0, j))),
            pl.BlockSpec((1, tc), at(lambda i, j: (0, j + nj)))]


def _conv_gate_fwd(name, u, cw, cb, tm=1024, tc=512):
    s_len, f2 = u.shape
    f_dim = f2 // 2
    tm = _tile(s_len, tm, 8)
    tc = _tile(f_dim, tc)
    nj = f_dim // tc

    def body(ug_ref, uv_ref, hg_ref, hv_ref, wg_ref, wv_ref, bg_ref, bv_ref, g_ref, eg_ref, ev_ref):
        first = pl.program_id(0) == 0
        _stage_head(eg_ref, ug_ref, hg_ref, first)
        _stage_head(ev_ref, uv_ref, hv_ref, first)
        for head, r0, n in _row_chunks(tm):
            gc = _conv3(_taps(eg_ref, 8, n) if head else _taps(ug_ref, r0, n), wg_ref, bg_ref)
            vc = _conv3(_taps(ev_ref, 8, n) if head else _taps(uv_ref, r0, n), wv_ref, bv_ref)
            g_ref[r0:r0 + n, :] = (gc * _sigmoid(gc) * vc).astype(BF16)

    return pl.pallas_call(
        body, name=name, grid=(s_len // tm, nj),
        in_specs=_conv_specs(tm, tc, nj, lambda i, j: (i, j)),
        out_specs=pl.BlockSpec((tm, tc), lambda i, j: (i, j)),
        out_shape=jax.ShapeDtypeStruct((s_len, f_dim), BF16),
        scratch_shapes=[pltpu.VMEM((8 + CONV_HEAD, tc), F32)] * 2,
        compiler_params=_params(("parallel", "parallel")))(u, u, u, u, cw, cw, cb, cb)


def _conv_gate_bwd(name, u, dg, cw, cb, tm=1024, tc=512):
    s_len, f2 = u.shape
    f_dim = f2 // 2
    tm = _tile(s_len, tm, 8)
    tc = _tile(f_dim, tc)
    nj = f_dim // tc

    def body(ug_ref, uv_ref, hg_ref, hv_ref, wg_ref, wv_ref, bg_ref, bv_ref, dg_ref, duc_ref, dcw_ref, dcb_ref,
             eg_ref, ev_ref):
        first = pl.program_id(1) == 0

        @pl.when(first)
        def _():
            dcw_ref[...] = jnp.zeros_like(dcw_ref)
            dcb_ref[...] = jnp.zeros_like(dcb_ref)

        _stage_head(eg_ref, ug_ref, hg_ref, first)
        _stage_head(ev_ref, uv_ref, hv_ref, first)
        sums = [[jnp.zeros((8, tc), F32) for _ in range(4)] for _ in range(2)]
        for head, r0, n in _row_chunks(tm):
            ug = _taps(eg_ref, 8, n) if head else _taps(ug_ref, r0, n)
            uv = _taps(ev_ref, 8, n) if head else _taps(uv_ref, r0, n)
            gc = _conv3(ug, wg_ref, bg_ref)
            vc = _conv3(uv, wv_ref, bv_ref)
            sg = _sigmoid(gc)
            dgv = dg_ref[r0:r0 + n, :]
            dvc = dgv * (gc * sg)
            dgc = dgv * vc * (sg * (1.0 + gc * (1.0 - sg)))
            duc_ref[0, r0:r0 + n, :] = dgc
            duc_ref[1, r0:r0 + n, :] = dvc
            for half, (d, us) in enumerate(((dgc, ug), (dvc, uv))):
                sums[half][3] = sums[half][3] + _fold8(d)
                for tap in range(3):
                    sums[half][tap] = sums[half][tap] + _fold8(d * us[2 - tap])
        for half in range(2):
            dcb_ref[half] += jnp.sum(sums[half][3], axis=0, keepdims=True)
            for tap in range(3):
                dcw_ref[half, tap:tap + 1, :] += jnp.sum(sums[half][tap], axis=0, keepdims=True)

    order = lambda j, i: (i, j)
    return pl.pallas_call(
        body, name=name, grid=(nj, s_len // tm),
        in_specs=_conv_specs(tm, tc, nj, order) + [pl.BlockSpec((tm, tc), lambda j, i: (i, j))],
        out_specs=[pl.BlockSpec((2, tm, tc), lambda j, i: (0, i, j)),
                   pl.BlockSpec((2, 3, tc), lambda j, i: (0, 0, j)),
                   pl.BlockSpec((2, 1, tc), lambda j, i: (0, 0, j))],
        out_shape=[jax.ShapeDtypeStruct((2, s_len, f_dim), F32), jax.ShapeDtypeStruct((2, 3, f_dim), F32),
                   jax.ShapeDtypeStruct((2, 1, f_dim), F32)],
        scratch_shapes=[pltpu.VMEM((8 + CONV_HEAD, tc), F32)] * 2,
        compiler_params=_params(("parallel", "arbitrary")))(u, u, u, u, cw, cw, cb, cb, dg)


def _conv_t(name, duc, cw, tm=1024, tc=512):
    _, s_len, f_dim = duc.shape
    tm = _tile(s_len, tm, 8)
    tc = _tile(f_dim, tc)
    nj = f_dim // tc
    nb = s_len // tm
    hb = tm // 8

    def body(d_ref, halo_ref, w_ref, o_ref, ext_ref):
        last = pl.program_id(1) == nb - 1
        tail = tm - CONV_HEAD
        ext_ref[0:CONV_HEAD, :] = d_ref[0, tail:tm, :]
        ext_ref[CONV_HEAD:, :] = jnp.where(last, 0.0, halo_ref[0])

        def out(ref, r0, n):
            return (w_ref[2:3, :] * ref[r0:r0 + n, :] + w_ref[1:2, :] * ref[r0 + 1:r0 + 1 + n, :]
                    + w_ref[0:1, :] * ref[r0 + 2:r0 + 2 + n, :]).astype(BF16)

        for r0 in range(0, tail, CONV_CHUNK):
            n = min(CONV_CHUNK, tail - r0)
            o_ref[r0:r0 + n, :] = out(d_ref.at[0], r0, n)
        o_ref[tail:tm, :] = out(ext_ref, 0, CONV_HEAD)

    return pl.pallas_call(
        body, name=name, grid=(2, nb, nj),
        in_specs=[pl.BlockSpec((1, tm, tc), lambda p, i, j: (p, i, j)),
                  pl.BlockSpec((1, 8, tc), lambda p, i, j: (p, jnp.minimum((i + 1) * hb, nb * hb - 1), j)),
                  pl.BlockSpec((3, tc), lambda p, i, j: (0, p * nj + j))],
        out_specs=pl.BlockSpec((tm, tc), lambda p, i, j: (i, p * nj + j)),
        out_shape=jax.ShapeDtypeStruct((s_len, 2 * f_dim), BF16),
        scratch_shapes=[pltpu.VMEM((CONV_HEAD + 8, tc), F32)],
        compiler_params=_params(("parallel", "parallel", "parallel")))(duc, duc, cw)


def _loss_head(name, y, tgt, tm=512):
    s_len, d = y.shape
    tm = _tile(s_len, tm, 8)

    def body(y_ref, t_ref, dy_ref, l_ref):
        @pl.when(pl.program_id(0) == 0)
        def _():
            l_ref[...] = jnp.zeros_like(l_ref)

        err = y_ref[...] - t_ref[...]
        dy_ref[...] = err * (1.0 / d)
        l_ref[...] += 0.5 * jnp.sum(jnp.sum(err * err, axis=1, keepdims=True) * (1.0 / d), axis=0, keepdims=True)

    blk = pl.BlockSpec((tm, d), lambda i: (i, 0))
    return pl.pallas_call(
        body, name=name, grid=(s_len // tm,), in_specs=[blk, blk],
        out_specs=[blk, pl.BlockSpec((1, LANES), lambda i: (0, 0))],
        out_shape=[jax.ShapeDtypeStruct((s_len, d), F32), jax.ShapeDtypeStruct((1, LANES), F32)],
        compiler_params=_params(("arbitrary",)))(y, tgt)


def _adamw(name, parts_list, w, m, v, tr=256, ride=None):
    n_l = len(parts_list)
    n_parts, rows, cols = parts_list[0].shape
    row_bytes = -(-cols // LANES) * LANES * (2 * n_l * n_parts * parts_list[0].dtype.itemsize + 2 * 7 * 4)
    tr = _tile(rows, min(tr, max(16, ADAMW_VMEM // row_bytes // 16 * 16)), 16)
    nb = rows // tr
    c1 = 1.0 - ADAM_B1 ** ADAM_STEP
    c2 = 1.0 - ADAM_B2 ** ADAM_STEP

    def body(*refs):
        p_refs = refs[:n_l]
        w_ref, m_ref, v_ref, g_ref, d_ref, nm_ref, nv_ref = refs[n_l:]
        for l, p_ref in enumerate(p_refs):
            @pl.when(pl.program_id(0) == l)
            def _(p_ref=p_ref):
                g = p_ref[0].astype(F32)
                for n in range(1, n_parts):
                    g = g + p_ref[n].astype(F32)
                nm = ADAM_B1 * m_ref[...] + (1.0 - ADAM_B1) * g
                nv = ADAM_B2 * v_ref[...] + (1.0 - ADAM_B2) * (g * g)
                g_ref[...] = g
                nm_ref[...] = nm
                nv_ref[...] = nv
                d_ref[...] = -ADAM_LR * ((nm / c1) / (jnp.sqrt(nv / c2) + ADAM_EPS) + ADAM_WD * w_ref[...])

    p_specs = [pl.BlockSpec((n_parts, tr, cols), lambda li, i, l=l: (0, jnp.where(li == l, i, 0), 0))
               for l in range(n_l)]
    blk = pl.BlockSpec((tr, cols), lambda li, i: (li * nb + i, 0))
    res, lands = _hosted_call(
        body, ride, name=name, grid=(n_l, nb), in_specs=p_specs + [blk, blk, blk], out_specs=[blk] * 4,
        out_shape=[jax.ShapeDtypeStruct((n_l * rows, cols), F32)] * 4, sem=("arbitrary", "arbitrary"),
        args=[*parts_list, w, m, v])
    return res if ride is None else (res, lands)


def _peers():
    x, y, c = lax.axis_index("x"), lax.axis_index("y"), lax.axis_index("c")
    out = []
    for k in range(1, N_DEV):
        fx, fy, fc = (k >> 2) & 1, (k >> 1) & 1, k & 1
        px, py, pc = x ^ fx, y ^ fy, c ^ fc
        out.append((k - 1, (px, py, pc), 4 * px + 2 * py + pc))
    return 4 * x + 2 * y + c, out


def _block_of(ref, axis, index, size):
    if axis is None:
        return ref.at[index]
    idx = [slice(None)] * len(ref.shape)
    idx[axis] = pl.ds(pl.multiple_of(index * size, size), size)
    return ref.at[tuple(idx)]


def _land_shape(shape, axis, scatter):
    shape = list(shape)
    if scatter:
        if axis is None:
            return tuple(shape)
        shape[axis] //= N_DEV
        return (N_DEV, *shape)
    if axis is None:
        return (N_DEV, *shape)
    shape[axis] *= N_DEV
    return tuple(shape)


def _copy_ends(axis, scatter, src, land, me, idx):
    if scatter:
        size = None if axis is None else src.shape[axis] // N_DEV
        return _block_of(src, axis, idx, size), land.at[me]
    return src, _block_of(land, axis, me, None if axis is None else src.shape[axis])


def _remote_copies(axes, scatter, in_refs, land_refs, send_sems, recv_sems):
    me, peers = _peers()
    out = []
    for a, (axis, src, land) in enumerate(zip(axes, in_refs, land_refs)):
        for k, pos, idx in peers:
            s, d = _copy_ends(axis, scatter, src, land, me, idx)
            out.append(pltpu.make_async_remote_copy(
                src_ref=s, dst_ref=d, send_sem=send_sems.at[a * (N_DEV - 1) + k],
                recv_sem=recv_sems.at[a * (N_DEV - 1) + k],
                device_id=pos, device_id_type=MESH))
    return out


def _exchange(name, arrays, axes, scatter):
    n = len(arrays)

    def body(*refs):
        copies = _all_copies(axes, scatter, refs[:n], refs[n:2 * n], *refs[2 * n:])
        for cp in copies:
            cp.start()
        for cp in copies:
            cp.wait()

    any_spec = pl.BlockSpec(memory_space=pl.ANY)
    return pl.pallas_call(
        body, name=name, in_specs=[any_spec] * n, out_specs=[any_spec] * n,
        out_shape=[jax.ShapeDtypeStruct(_land_shape(a.shape, ax, scatter), a.dtype) for a, ax in zip(arrays, axes)],
        scratch_shapes=_exchange_sems(n),
        compiler_params=pltpu.CompilerParams(has_side_effects=True))(*arrays)


def _all_copies(axes, scatter, in_refs, land_refs, send_sems, recv_sems, local_sems):
    me, _ = _peers()
    copies = _remote_copies(axes, scatter, in_refs, land_refs, send_sems, recv_sems)
    for a, (axis, src, land) in enumerate(zip(axes, in_refs, land_refs)):
        s, d = _copy_ends(axis, scatter, src, land, me, me)
        copies.append(pltpu.make_async_copy(s, d, local_sems.at[a]))
    return copies


def _exchange_sems(n):
    return [pltpu.SemaphoreType.DMA((n * (N_DEV - 1),)), pltpu.SemaphoreType.DMA((n * (N_DEV - 1),)),
            pltpu.SemaphoreType.DMA((n,))]


def _hosted_call(body, ride, *, name, grid, in_specs, out_specs, out_shape, sem, args, scratch_shapes=()):
    if ride is None:
        return pl.pallas_call(body, name=name, grid=grid, in_specs=in_specs, out_specs=out_specs, out_shape=out_shape,
                              scratch_shapes=list(scratch_shapes), compiler_params=_params(sem))(*args), None
    arrays, axes, scatter = ride
    n, n_in, n_out, n_scr = len(arrays), len(in_specs), len(out_specs), len(scratch_shapes)

    def hosted(*refs):
        main_in, ride_in = refs[:n_in], refs[n_in:n_in + n]
        o0 = n_in + n
        main_out, lands = refs[o0:o0 + n_out], refs[o0 + n_out:o0 + n_out + n]
        s0 = o0 + n_out + n
        main_scr, sems = refs[s0:s0 + n_scr], refs[s0 + n_scr:]
        ids = [pl.program_id(i) for i in range(len(grid))]
        first = functools.reduce(jnp.logical_and, [i == 0 for i in ids])
        last = functools.reduce(jnp.logical_and, [i == g - 1 for i, g in zip(ids, grid)])

        @pl.when(first)
        def _():
            for cp in _all_copies(axes, scatter, ride_in, lands, *sems):
                cp.start()

        body(*main_in, *main_out, *main_scr)

        @pl.when(last)
        def _():
            for cp in _all_copies(axes, scatter, ride_in, lands, *sems):
                cp.wait()

    any_spec = pl.BlockSpec(memory_space=pl.ANY)
    res = pl.pallas_call(
        hosted, name=name, grid=grid, in_specs=list(in_specs) + [any_spec] * n,
        out_specs=list(out_specs) + [any_spec] * n,
        out_shape=list(out_shape) + [jax.ShapeDtypeStruct(_land_shape(a.shape, ax, scatter), a.dtype)
                                     for a, ax in zip(arrays, axes)],
        scratch_shapes=list(scratch_shapes) + _exchange_sems(n),
        compiler_params=pltpu.CompilerParams(dimension_semantics=("arbitrary",) * len(grid),
                                             vmem_limit_bytes=VMEM_LIMIT, has_side_effects=True))(*args, *arrays)
    return res[:n_out], res[n_out:]


def _row(vec, width=None):
    vec = vec.reshape(1, -1)
    if width is not None and vec.shape[1] < width:
        vec = jnp.pad(vec, ((0, 0), (0, width - vec.shape[1])))
    return vec


def _riding(rides, lands, own=None):
    def run(key, fn, *args, **kwargs):
        if rides is None or key not in rides:
            return fn(*args, **kwargs)
        arrays, axes, scatter = rides[key]
        arrays = [own[a] if isinstance(a, str) else a for a in arrays]
        res, lands[key] = fn(*args, ride=(arrays, axes, scatter), **kwargs)
        return res
    return run


def _layer_fwd(l, x, sp, bp, rides=None, late=None):
    n_heads = x.shape[1] // (2 * HEAD_DIM)
    lands = {}
    run = _riding(rides, lands)

    def weight(k):
        if k not in bp:
            bp[k] = lands[late[k][0]][late[k][1]]
        return bp[k]

    proj, h = run("in_proj", _mm_nn, f"l{l}_in_proj", [x], [_row(sp["attn_norm"])], bp["w_in"])
    qg, kg, bf = _row(sp["q_norm"]), _row(sp["k_norm"]), _row(sp["b_forget"], LANES)
    pa, c = _prep_fwd(f"l{l}_prep", proj, qg, kg, bf, n_heads)
    ct = c[:, :n_heads].T
    o_a, lse = run("fox", _fox_fwd, f"l{l}_fox", pa, c, ct, n_heads)
    o_b, tot, kst = run("sb", _sb_fwd, f"l{l}_sb", pa, n_heads)
    gfox, gsb = _row(sp["out_norm_fox"]), _row(sp["out_norm_sb"])
    x1, merged = run("out_proj", _mm_nn, f"l{l}_out_proj", [o_a, o_b], [gfox, gsb], weight("w_out"), resid=x)
    u, h2 = run("up_proj", _mm_nn, f"l{l}_up_proj", [x1], [_row(sp["ffn_norm"])], weight("w_up"))
    cb = _row(sp["conv_b"])
    g = _conv_gate_fwd(f"l{l}_conv_gate", u, bp["conv_w"], cb)
    x2 = run("down_proj", _mm_nn, f"l{l}_down_proj", [g], [], weight("w_down"), resid=x1, tn_cap=512)
    saved = dict(x=x, h=h, proj=proj, pa=pa, c=c, ct=ct, o_a=o_a, lse=lse, o_b=o_b, tot=tot, kst=kst, merged=merged, x1=x1,
                 h2=h2, u=u, g=g, qg=qg, kg=kg, bf=bf, gfox=gfox, gsb=gsb, cb=cb)
    return x2, saved, lands


def _layer_bwd(l, dx, sp, bp, sv, rides=None):
    n_heads = dx.shape[1] // (2 * HEAD_DIM)
    gr, lands = {}, {}
    run = _riding(rides, lands, gr)
    dg = run("d_down_act", _mm_nt, f"l{l}_d_down_act", dx, bp["w_down"])
    gr["w_down"] = _mm_tn(f"l{l}_d_w_down", sv["g"], dx)
    duc, dcw, dcb = _conv_gate_bwd(f"l{l}_d_conv_gate", sv["u"], dg, bp["conv_w"], sv["cb"])
    gr["conv_w"] = dcw.transpose(1, 0, 2).reshape(3, -1)
    gr["conv_b"] = dcb.reshape(-1)
    du = _conv_t(f"l{l}_d_conv", duc, bp["conv_w"])
    gr["w_up"] = run("d_w_up", _mm_tn, f"l{l}_d_w_up", sv["h2"], du)
    dx1, dffn = run("d_up_act", _mm_nt, f"l{l}_d_up_act", du, bp["w_up"], "rms_bwd", [sv["x1"]],
                    [_row(sp["ffn_norm"])], dres=dx)
    gr["ffn_norm"] = dffn.reshape(-1)
    gr["w_out"] = _mm_tn(f"l{l}_d_w_out", sv["merged"], dx1)
    do_a, do_b, dgfox, dgsb = _mm_nt(f"l{l}_d_out_act", dx1, bp["w_out"], "rms2_bwd",
                                     [sv["o_a"], sv["o_b"]], [sv["gfox"], sv["gsb"]])
    gr["out_norm_fox"], gr["out_norm_sb"] = dgfox.reshape(-1), dgsb.reshape(-1)
    dq_a, dk_a, dv_a, dct, dcq = _fox_bwd(f"l{l}_d_fox", sv["pa"], sv["c"], sv["ct"], sv["o_a"], do_a, sv["lse"],
                                          n_heads)
    d_sb = _sb_bwd(f"l{l}_d_sb", sv["pa"], do_b, sv["tot"], sv["kst"], n_heads)
    dc = jnp.pad((dct + dcq[:, :, 0]).T, ((0, 0), (0, LANES - n_heads)))
    dproj, dqg, dkg, dbf = _prep_bwd(f"l{l}_d_prep", sv["proj"], sv["qg"], sv["kg"], sv["bf"],
                                     (dq_a, dk_a, dv_a), d_sb, dc, n_heads)
    gr["q_norm"], gr["k_norm"], gr["b_forget"] = dqg.reshape(-1), dkg.reshape(-1), dbf.reshape(-1)[:n_heads]
    gr["w_in"] = run("d_w_in", _mm_tn, f"l{l}_d_w_in", sv["h"], dproj)
    dx0, dattn = run("d_in_act", _mm_nt, f"l{l}_d_in_act", dproj, bp["w_in"], "rms_bwd", [sv["x"]],
                     [_row(sp["attn_norm"])], dres=dx1)
    gr["attn_norm"] = dattn.reshape(-1)
    return dx0, gr, lands


def _local_step(x, tgt, small, big):
    n_layers = len(big)
    saved = []
    for l in range(n_layers):
        x, sv, _ = _layer_fwd(l, x, small[l], big[l])
        saved.append(sv)
    dx, loss_part = _loss_head("loss_head", x, tgt)
    grads = [None] * n_layers
    for l in reversed(range(n_layers)):
        dx, grads[l], _ = _layer_bwd(l, dx, small[l], big[l], saved[l])
    return loss_part, dx, grads


def _w_in_to_internal(w, n_heads):
    w3 = 3 * n_heads * HEAD_DIM
    pad = jnp.zeros(w.shape[:-1] + (LANES - n_heads,), w.dtype)
    return jnp.concatenate([w[..., :w3], w[..., w3 + n_heads:], w[..., w3:w3 + n_heads], pad], axis=-1)


def _w_in_from_internal(w, n_heads):
    w3 = 3 * n_heads * HEAD_DIM
    return jnp.concatenate([w[..., :w3], w[..., 2 * w3:2 * w3 + n_heads], w[..., w3:2 * w3]], axis=-1)


SMALL = ("attn_norm", "b_forget", "q_norm", "k_norm", "out_norm_fox", "out_norm_sb", "ffn_norm", "conv_b")
BIG = ("w_in", "w_out", "w_up", "w_down")
WEIGHTS = ("attn_norm", "w_in", "b_forget", "q_norm", "k_norm", "out_norm_fox", "out_norm_sb", "w_out", "ffn_norm",
           "w_up", "conv_w", "conv_b", "w_down")


def kernel(x, attn_norm, w_in, b_forget, q_norm, k_norm, out_norm_fox, out_norm_sb, w_out, ffn_norm, w_up, conv_w, conv_b, w_down, loss_target, m_attn_norm, m_w_in, m_b_forget, m_q_norm, m_k_norm, m_out_norm_fox, m_out_norm_sb, m_w_out, m_ffn_norm, m_w_up, m_conv_w, m_conv_b, m_w_down, v_attn_norm, v_w_in, v_b_forget, v_q_norm, v_k_norm, v_out_norm_fox, v_out_norm_sb, v_w_out, v_ffn_norm, v_w_up, v_conv_w, v_conv_b, v_w_down):
    w = dict(attn_norm=attn_norm, w_in=w_in, b_forget=b_forget, q_norm=q_norm, k_norm=k_norm,
             out_norm_fox=out_norm_fox, out_norm_sb=out_norm_sb, w_out=w_out, ffn_norm=ffn_norm, w_up=w_up,
             conv_w=conv_w, conv_b=conv_b, w_down=w_down)
    mom = dict(attn_norm=m_attn_norm, w_in=m_w_in, b_forget=m_b_forget, q_norm=m_q_norm, k_norm=m_k_norm,
               out_norm_fox=m_out_norm_fox, out_norm_sb=m_out_norm_sb, w_out=m_w_out, ffn_norm=m_ffn_norm,
               w_up=m_w_up, conv_w=m_conv_w, conv_b=m_conv_b, w_down=m_w_down)
    var = dict(attn_norm=v_attn_norm, w_in=v_w_in, b_forget=v_b_forget, q_norm=v_q_norm, k_norm=v_k_norm,
               out_norm_fox=v_out_norm_fox, out_norm_sb=v_out_norm_sb, w_out=v_w_out, ffn_norm=v_ffn_norm,
               w_up=v_w_up, conv_w=v_conv_w, conv_b=v_conv_b, w_down=v_w_down)
    n_layers, d = attn_norm.shape
    n_heads = d // (2 * HEAD_DIM)
    me = 4 * lax.axis_index("x") + 2 * lax.axis_index("y") + lax.axis_index("c")

    shard = {k: w[k].astype(BF16) for k in BIG}
    axis_of = dict(w_in=None, w_out=0, w_up=1, w_down=0)
    host_fwd = dict(w_in="in_proj", w_out="out_proj", w_up="up_proj", w_down="down_proj")
    host_bwd = dict(w_in="d_up_act", w_out="d_down_act", w_up="d_w_up", w_down="d_in_act")
    whole_w_in = lambda g: _w_in_to_internal(g.transpose(1, 0, 2).reshape(d, -1), n_heads)
    g_in, full_cw = _exchange("gather_first", [shard["w_in"][0], w["conv_w"]], [None, 2], scatter=False)
    small = [{k: w[k][l] for k in SMALL} for l in range(n_layers)]
    big = [dict(w_in=whole_w_in(g_in), conv_w=full_cw[0])]

    saved = [None] * n_layers
    act = x[0]
    for l in range(n_layers):
        rides, late = {}, None
        if l + 1 < n_layers:
            rides = {host_fwd[k]: ([shard[k][l + 1]], [axis_of[k]], False) for k in BIG}
        if l == 0:
            late = dict(w_out=("in_proj", -1), w_up=("fox", -1), w_down=("sb", -1))
            for k, (host, _) in late.items():
                arrays, axes, _ = rides.get(host, ([], [], False))
                rides[host] = (arrays + [shard[k][0]], axes + [axis_of[k]], False)
        act, saved[l], lands = _layer_fwd(l, act, small[l], big[l], rides, late)
        if l + 1 < n_layers:
            nxt = {k: lands[host_fwd[k]][0] for k in BIG}
            big.append(dict(nxt, w_in=whole_w_in(nxt["w_in"]), conv_w=full_cw[l + 1]))
    dx, loss_part = _loss_head("loss_head", act, loss_target[0])
    grads, parts, sends = [None] * n_layers, [None] * n_layers, None
    for l in reversed(range(n_layers)):
        rides = {}
        if sends is not None:
            rides = {host_bwd[k]: ([sends[k]], [axis_of[k]], True) for k in BIG}
        if l == 0:
            for host, k in (("d_up_act", "w_down"), ("d_in_act", "w_out"), ("d_w_in", "w_up")):
                arrays, axes, _ = rides.get(host, ([], [], True))
                rides[host] = (arrays + [k], axes + [axis_of[k]], True)
        dx, grads[l], lands = _layer_bwd(l, dx, small[l], big[l], saved[l], rides)
        if sends is not None:
            parts[l + 1] = [lands[host_bwd[k]][0] for k in BIG]
        g_in = _w_in_from_internal(grads[l]["w_in"], n_heads)
        sends = dict(grads[l], w_in=g_in.reshape(d, N_DEV, -1).transpose(1, 0, 2))
    grad_x = dx
    parts[0] = [None, lands["d_in_act"][-1], lands["d_w_in"][-1], lands["d_up_act"][-1]]

    stack = lambda k: jnp.stack([grads[l][k] for l in range(n_layers)])
    small_names = SMALL + ("conv_w",)
    flat = jnp.concatenate([loss_part.reshape(-1)] + [
        jnp.pad(stack(k).reshape(-1), (0, (-stack(k).size) % LANES)) for k in small_names])
    flat = flat.reshape(-1, LANES)
    out = {}

    def update(name, parts_list, ride=None):
        rc = (-1, parts_list[0].shape[-1])
        res = _adamw("adamw_" + name, parts_list, w[name].reshape(rc), mom[name].reshape(rc), var[name].reshape(rc),
                     ride=ride)
        res, landed = res if ride is not None else (res, None)
        out[name] = [r.reshape(w[name].shape) for r in res]
        return landed

    layers_of = lambda name: [parts[l][BIG.index(name)] for l in range(n_layers)]
    (parts[0][0],) = update("w_up", layers_of("w_up"), ride=([sends["w_in"]], [None], True))
    (all_small,) = update("w_down", layers_of("w_down"), ride=([flat], [None], False))
    update("w_out", layers_of("w_out"))
    update("w_in", layers_of("w_in"))
    n_rows = flat.shape[0]
    w_flat, m_flat, v_flat = [], [], []
    for src, dst in ((w, w_flat), (mom, m_flat), (var, v_flat)):
        dst.append(jnp.zeros((LANES,), F32))
        for k in small_names:
            a = src[k]
            if k == "conv_w":
                a = jnp.zeros((n_layers, 3, conv_w.shape[2] * N_DEV), F32)
            dst.append(jnp.pad(a.reshape(-1), (0, (-a.size) % LANES)))
    pack = lambda parts: jnp.concatenate(parts).reshape(n_rows, LANES)
    res = _adamw("adamw_small", [all_small], pack(w_flat), pack(m_flat), pack(v_flat))
    res = [r.reshape(-1) for r in res]
    loss = res[0][0]
    off = LANES
    g_cw_full = None
    for k in small_names:
        size = n_layers * 3 * conv_w.shape[2] * N_DEV if k == "conv_w" else w[k].size
        if k == "conv_w":
            g_cw_full = res[0][off:off + size].reshape(n_layers, 3, -1)
        else:
            out[k] = [r[off:off + size].reshape(w[k].shape) for r in res]
        off += size + (-size) % LANES
    c_loc = conv_w.shape[2]
    g_cw_mine = lax.dynamic_slice_in_dim(g_cw_full, me * c_loc, c_loc, axis=2)
    update("conv_w", [g_cw_mine.reshape(1, n_layers * 3, c_loc)])

    outs = [loss, grad_x[None]]
    for n in range(4):
        outs += [out[k][n] for k in WEIGHTS]
    return tuple(outs)
```

```python
import functools

import jax
import jax.numpy as jnp
from jax import lax
from jax.experimental import pallas as pl
from jax.experimental.pallas import tpu as pltpu

F32 = jnp.float32
BF16 = jnp.bfloat16
HEAD_DIM = 128
QK_SCALE = HEAD_DIM ** -0.5
SKIP = 110.0
BOUND_SLACK = 1.0 + 2.0 ** -6
LANES = 128
EPS = 1e-6
N_DEV = 8
ADAM_LR = 0.001
ADAM_B1 = 0.9
ADAM_B2 = 0.999
ADAM_EPS = 1e-08
ADAM_WD = 0.01
ADAM_STEP = 10
VMEM_LIMIT = 56 * 1024 * 1024
ADAMW_VMEM = 24 * 1024 * 1024
MESH = pl.DeviceIdType.MESH

NT_DIMS = (((1,), (1,)), ((), ()))
TN_DIMS = (((0,), (0,)), ((), ()))


def _tile(n, cap, mult=LANES):
    t = (min(cap, n) // mult) * mult
    while t >= mult:
        if n % t == 0:
            return t
        t -= mult
    return n


def _params(sem, vmem=VMEM_LIMIT):
    return pltpu.CompilerParams(dimension_semantics=sem, vmem_limit_bytes=vmem)


def _split_dot(x, u, n_split, x_left=True):
    acc = None
    rest = x
    for s in range(n_split):
        piece = rest.astype(BF16)
        if s + 1 < n_split:
            rest = rest - piece.astype(F32)
        d = (jnp.dot(piece, u, preferred_element_type=F32) if x_left
             else jnp.dot(u, piece, preferred_element_type=F32))
        acc = d if acc is None else acc + d
    return acc


def _tri(n, kind):
    r = lax.broadcasted_iota(jnp.int32, (n, n), 0)
    c = lax.broadcasted_iota(jnp.int32, (n, n), 1)
    return jnp.where(r >= c if kind == "ge" else r <= c, 1.0, 0.0).astype(BF16)


def _mm_nn(name, a_list, g_list, w, resid=None, tm=1024, tn_cap=1024, ride=None):
    s_len = a_list[0].shape[0]
    k_dim, n_dim = w.shape
    tm = _tile(s_len, tm, 8)
    tn = _tile(n_dim, tn_cap)
    normed = bool(g_list)

    def body(*refs):
        refs = list(refs)
        a_refs = [refs.pop(0) for _ in a_list]
        g_refs = [refs.pop(0) for _ in g_list]
        w_ref = refs.pop(0)
        r_ref = refs.pop(0) if resid is not None else None
        o_ref = refs.pop(0)
        if normed:
            h_ref = refs.pop(0)

            @pl.when(pl.program_id(1) == 0)
            def _():
                off = 0
                for a_ref, g_ref in zip(a_refs, g_refs):
                    xv = a_ref[...]
                    kk = xv.shape[1]
                    r = lax.rsqrt(jnp.mean(xv * xv, axis=1, keepdims=True) + EPS)
                    h_ref[:, off:off + kk] = (xv * r * g_ref[...]).astype(BF16)
                    off += kk

            a = h_ref[...]
        else:
            a = a_refs[0][...]
        acc = jnp.dot(a, w_ref[...], preferred_element_type=F32)
        if r_ref is not None:
            acc = acc + r_ref[...]
        o_ref[...] = acc

    in_specs = [pl.BlockSpec((tm, a.shape[1]), lambda i, j: (i, 0)) for a in a_list]
    in_specs += [pl.BlockSpec((1, g.shape[1]), lambda i, j: (0, 0)) for g in g_list]
    in_specs += [pl.BlockSpec((k_dim, tn), lambda i, j: (0, j))]
    args = list(a_list) + list(g_list) + [w]
    if resid is not None:
        in_specs.append(pl.BlockSpec((tm, tn), lambda i, j: (i, j)))
        args.append(resid)
    out_shape = [jax.ShapeDtypeStruct((s_len, n_dim), F32)]
    out_specs = [pl.BlockSpec((tm, tn), lambda i, j: (i, j))]
    if normed:
        out_shape.append(jax.ShapeDtypeStruct((s_len, k_dim), BF16))
        out_specs.append(pl.BlockSpec((tm, k_dim), lambda i, j: (i, 0)))
    res, lands = _hosted_call(body, ride, name=name, grid=(s_len // tm, n_dim // tn), in_specs=in_specs,
                              out_specs=out_specs, out_shape=out_shape, sem=("parallel", "arbitrary"), args=args)
    res = res if normed else res[0]
    return res if ride is None else (res, lands)


def _rms_bwd(dh, xv, gv, r=None):
    if r is None:
        r = lax.rsqrt(jnp.mean(xv * xv, axis=1, keepdims=True) + EPS)
    xhat = xv * r
    dxh = dh * gv
    dx = r * (dxh - xhat * jnp.mean(dxh * xhat, axis=1, keepdims=True))
    return dx, dh * xhat


def _mm_nt(name, da, w, mode="plain", xs=(), gs=(), dres=None, tm=512, tk_cap=1024, ride=None):
    s_len, kc = da.shape
    n_out = w.shape[0]
    tm = _tile(s_len, tm, 8)
    tk = _tile(kc, tk_cap)
    nk = kc // tk

    def body(*refs):
        refs = list(refs)
        da_ref, w_ref = refs.pop(0), refs.pop(0)
        x_refs = [refs.pop(0) for _ in xs]
        g_refs = [refs.pop(0) for _ in gs]
        dres_ref = refs.pop(0) if dres is not None else None
        acc_ref = refs[0] if mode == "plain" else refs.pop()
        out_refs = refs
        i, k = pl.program_id(0), pl.program_id(1)

        @pl.when(k == 0)
        def _():
            acc_ref[...] = jnp.zeros_like(acc_ref)

        acc_ref[...] += lax.dot_general(da_ref[...].astype(BF16), w_ref[...], NT_DIMS,
                                        preferred_element_type=F32)

        if mode == "plain":
            return

        @pl.when(k == nk - 1)
        def _():
            n_x = len(xs)
            dx_refs, dg_refs = out_refs[:n_x], out_refs[n_x:]

            @pl.when(i == 0)
            def _():
                for dg_ref in dg_refs:
                    dg_ref[...] = jnp.zeros_like(dg_ref)

            off = 0
            for x_ref, g_ref, dx_ref, dg_ref in zip(x_refs, g_refs, dx_refs, dg_refs):
                kk = x_ref.shape[1]
                dx, dgp = _rms_bwd(acc_ref[:, off:off + kk], x_ref[...], g_ref[...])
                if dres_ref is not None:
                    dx = dx + dres_ref[...]
                dx_ref[...] = dx
                dg_ref[...] += jnp.sum(dgp, axis=0, keepdims=True)
                off += kk

    in_specs = [pl.BlockSpec((tm, tk), lambda i, k: (i, k)), pl.BlockSpec((n_out, tk), lambda i, k: (0, k))]
    in_specs += [pl.BlockSpec((tm, x.shape[1]), lambda i, k: (i, 0)) for x in xs]
    in_specs += [pl.BlockSpec((1, g.shape[1]), lambda i, k: (0, 0)) for g in gs]
    args = [da, w] + list(xs) + list(gs)
    if dres is not None:
        in_specs.append(pl.BlockSpec((tm, n_out), lambda i, k: (i, 0)))
        args.append(dres)
    if mode == "plain":
        out_shape = [jax.ShapeDtypeStruct((s_len, n_out), F32)]
        out_specs = [pl.BlockSpec((tm, n_out), lambda i, k: (i, 0))]
    else:
        out_shape = [jax.ShapeDtypeStruct((s_len, x.shape[1]), F32) for x in xs]
        out_specs = [pl.BlockSpec((tm, x.shape[1]), lambda i, k: (i, 0)) for x in xs]
        out_shape += [jax.ShapeDtypeStruct((1, x.shape[1]), F32) for x in xs]
        out_specs += [pl.BlockSpec((1, x.shape[1]), lambda i, k: (0, 0)) for x in xs]
    res, lands = _hosted_call(body, ride, name=name, grid=(s_len // tm, nk), in_specs=in_specs, out_specs=out_specs,
                              out_shape=out_shape, sem=("arbitrary", "arbitrary"), args=args,
                              scratch_shapes=[] if mode == "plain" else [pltpu.VMEM((tm, n_out), F32)])
    res = res[0] if mode == "plain" else res
    return res if ride is None else (res, lands)


def _mm_tn(name, a, b, tk_cap=1024, tn_cap=2048, tm=1024, ride=None):
    s_len, k_dim = a.shape
    n_dim = b.shape[1]
    tk = _tile(k_dim, tk_cap)
    tn = _tile(n_dim, tn_cap)
    tm = _tile(s_len, tm, 8)
    nm = s_len // tm

    def body(a_ref, b_ref, o_ref, acc_ref):
        m = pl.program_id(2)

        @pl.when(m == 0)
        def _():
            acc_ref[...] = jnp.zeros_like(acc_ref)

        acc_ref[...] += lax.dot_general(a_ref[...].astype(BF16), b_ref[...].astype(BF16), TN_DIMS,
                                        preferred_element_type=F32)

        @pl.when(m == nm - 1)
        def _():
            o_ref[...] = acc_ref[...].astype(BF16)

    res, lands = _hosted_call(
        body, ride, name=name, grid=(k_dim // tk, n_dim // tn, nm),
        in_specs=[pl.BlockSpec((tm, tk), lambda i, j, m: (m, i)), pl.BlockSpec((tm, tn), lambda i, j, m: (m, j))],
        out_specs=[pl.BlockSpec((tk, tn), lambda i, j, m: (i, j))],
        out_shape=[jax.ShapeDtypeStruct((k_dim, n_dim), BF16)],
        scratch_shapes=[pltpu.VMEM((tk, tn), F32)], sem=("parallel", "parallel", "arbitrary"), args=[a, b])
    return res[0] if ride is None else (res[0], lands)


def _neg_softplus(z):
    e = jnp.exp(-jnp.abs(z))
    return -(jnp.maximum(z, 0.0) + jnp.log(1.0 + e)), e


def _prep_fwd(name, proj, qg, kg, bf, n_heads, tm=256):
    s_len, n_p = proj.shape
    w_dim = n_heads * HEAD_DIM
    tm = _tile(s_len, tm, 8)

    def body(p_ref, qg_ref, kg_ref, bf_ref, pa_ref, c_ref, carry_ref):
        @pl.when(pl.program_id(0) == 0)
        def _():
            carry_ref[...] = jnp.zeros_like(carry_ref)

        for base, g_ref, mul in ((0, qg_ref, QK_SCALE), (w_dim, kg_ref, None)):
            for hh in range(n_heads):
                sl = slice(base + hh * HEAD_DIM, base + (hh + 1) * HEAD_DIM)
                xv = p_ref[:, sl]
                r = lax.rsqrt(jnp.mean(xv * xv, axis=1, keepdims=True) + EPS)
                y = xv * r * g_ref[...]
                pa_ref[:, sl] = (y if mul is None else y * mul).astype(BF16)
        pa_ref[:, 2 * w_dim:3 * w_dim] = p_ref[:, 2 * w_dim:3 * w_dim].astype(BF16)
        pa_ref[:, 3 * w_dim:4 * w_dim] = (p_ref[:, 3 * w_dim:4 * w_dim] * QK_SCALE).astype(BF16)
        pa_ref[:, 4 * w_dim:] = p_ref[:, 4 * w_dim:6 * w_dim].astype(BF16)
        f = p_ref[:, 6 * w_dim:] + bf_ref[...]
        lf, _ = _neg_softplus(-f)
        lane = lax.broadcasted_iota(jnp.int32, lf.shape, 1)
        lf = jnp.where(lane < n_heads, lf, 0.0)
        cb = _split_dot(lf, _tri(tm, "ge"), 3, x_left=False) + carry_ref[...]
        c_ref[...] = cb
        carry_ref[...] = cb[tm - 1:tm, :]

    return pl.pallas_call(
        body, name=name, grid=(s_len // tm,),
        in_specs=[pl.BlockSpec((tm, n_p), lambda i: (i, 0))] + [pl.BlockSpec((1, LANES), lambda i: (0, 0))] * 3,
        out_specs=[pl.BlockSpec((tm, 6 * w_dim), lambda i: (i, 0)), pl.BlockSpec((tm, LANES), lambda i: (i, 0))],
        out_shape=[jax.ShapeDtypeStruct((s_len, 6 * w_dim), BF16), jax.ShapeDtypeStruct((s_len, LANES), F32)],
        scratch_shapes=[pltpu.VMEM((1, LANES), F32)],
        compiler_params=_params(("arbitrary",)))(proj, qg, kg, bf)


def _prep_bwd(name, proj, qg, kg, bf, d_fox, d_sb, dc, n_heads, tm=256):
    s_len, n_p = proj.shape
    w_dim = n_heads * HEAD_DIM
    tm = _tile(s_len, tm, 8)
    nb = s_len // tm

    def body(p_ref, qg_ref, kg_ref, bf_ref, dqa_ref, dka_ref, dva_ref, dqb_ref, dkb_ref, dvb_ref, dc_ref,
             dp_ref, dqg_ref, dkg_ref, dbf_ref, carry_ref):
        @pl.when(pl.program_id(0) == 0)
        def _():
            for ref in (carry_ref, dqg_ref, dkg_ref, dbf_ref):
                ref[...] = jnp.zeros_like(ref)

        for base, g_ref, d_ref, dg_ref in ((0, qg_ref, dqa_ref, dqg_ref), (w_dim, kg_ref, dka_ref, dkg_ref)):
            dg = jnp.zeros((1, HEAD_DIM), F32)
            for hh in range(n_heads):
                sl = slice(base + hh * HEAD_DIM, base + (hh + 1) * HEAD_DIM)
                dx, dgp = _rms_bwd(d_ref[:, hh * HEAD_DIM:(hh + 1) * HEAD_DIM], p_ref[:, sl], g_ref[...])
                dp_ref[:, sl] = dx.astype(BF16)
                dg = dg + jnp.sum(dgp, axis=0, keepdims=True)
            dg_ref[...] += dg
        for n, d_ref in enumerate((dva_ref, dqb_ref, dkb_ref, dvb_ref)):
            dp_ref[:, (2 + n) * w_dim:(3 + n) * w_dim] = d_ref[...].astype(BF16)
        dlf = _split_dot(dc_ref[...], _tri(tm, "le"), 3, x_left=False) + carry_ref[...]
        carry_ref[...] = dlf[0:1, :]
        f = p_ref[:, 6 * w_dim:] + bf_ref[...]
        e = jnp.exp(-jnp.abs(f))
        sig_neg = jnp.where(f >= 0, e, 1.0) / (1.0 + e)
        lane = lax.broadcasted_iota(jnp.int32, f.shape, 1)
        df = jnp.where(lane < n_heads, dlf * sig_neg, 0.0)
        dp_ref[:, 6 * w_dim:] = df.astype(BF16)
        dbf_ref[...] += jnp.sum(df, axis=0, keepdims=True)

    rev = lambda i: (nb - 1 - i, 0)
    vec = pl.BlockSpec((1, LANES), lambda i: (0, 0))
    return pl.pallas_call(
        body, name=name, grid=(nb,),
        in_specs=[pl.BlockSpec((tm, n_p), rev), vec, vec, vec] + [pl.BlockSpec((tm, w_dim), rev)] * 6
        + [pl.BlockSpec((tm, LANES), rev)],
        out_specs=[pl.BlockSpec((tm, n_p), rev), vec, vec, vec],
        out_shape=[jax.ShapeDtypeStruct((s_len, n_p), BF16)] + [jax.ShapeDtypeStruct((1, LANES), F32)] * 3,
        scratch_shapes=[pltpu.VMEM((1, LANES), F32)],
        compiler_params=_params(("arbitrary",)))(proj, qg, kg, bf, *d_fox, *d_sb, dc)


def _head_col(c_blk, h):
    lane = lax.broadcasted_iota(jnp.int32, c_blk.shape, 1)
    return jnp.sum(jnp.where(lane == h, c_blk, 0.0), axis=1, keepdims=True)


def _key_norm_max(k_ref):
    kf = k_ref[...].astype(F32)
    return jnp.sqrt(jnp.max(jnp.sum(kf * kf, axis=1, keepdims=True), axis=0, keepdims=True))


def _logit_bound(q, kmax):
    qf = q.astype(F32)
    return jnp.sqrt(jnp.sum(qf * qf, axis=1, keepdims=True)) * kmax * BOUND_SLACK


def _c_block_end(ct_ref, h, kb, tk):
    return jnp.min(ct_ref[pl.ds(h, 1), pl.ds(pl.multiple_of(kb * tk, tk), tk)])


def _fox_fwd(name, pa, c, ct, n_heads, tq=1024, ride=None):
    s_len = pa.shape[0]
    tq = _tile(s_len, tq, LANES)
    hp = ct.shape[0]

    def body(q_ref, k_ref, v_ref, c_ref, ct_ref, o_ref, lse_ref, kmax_ref):
        h, qi = pl.program_id(0), pl.program_id(1)

        @pl.when(qi == 0)
        def _():
            kmax_ref[...] = _key_norm_max(k_ref)

        q = q_ref[...]
        cq = _head_col(c_ref[...], h)
        top = _logit_bound(q, kmax_ref[...]) + cq
        row = lax.broadcasted_iota(jnp.int32, (tq, tq), 0)
        col = lax.broadcasted_iota(jnp.int32, (tq, tq), 1)

        def step(kb, carry, masked):
            m, l, acc = carry
            ks = pl.multiple_of(kb * tq, tq)
            k = k_ref[pl.ds(ks, tq), :]
            v = v_ref[pl.ds(ks, tq), :]
            ck = ct_ref[pl.ds(h, 1), pl.ds(ks, tq)]
            s = lax.dot_general(q, k, NT_DIMS, preferred_element_type=F32) + (cq - ck)
            if masked:
                s = jnp.where(col <= row, s, -jnp.inf)
            m_new = jnp.maximum(m, jnp.max(s, axis=1, keepdims=True))
            alpha = jnp.exp(m - m_new)
            p = jnp.exp(s - m_new)
            l = alpha * l + jnp.sum(p, axis=1, keepdims=True)
            acc = alpha * acc + jnp.dot(p.astype(BF16), v, preferred_element_type=F32)
            return m_new, l, acc

        def margin(kb, m):
            return jnp.max(top - m) - _c_block_end(ct_ref, h, jnp.maximum(kb, 0), tq)

        def walk(state):
            kb, _, m, l, acc = state
            m, l, acc = step(kb, (m, l, acc), False)
            return kb - 1, margin(kb - 1, m), m, l, acc

        init = (jnp.full((tq, 1), -jnp.inf, F32), jnp.zeros((tq, 1), F32), jnp.zeros((tq, HEAD_DIM), F32))
        carry = step(qi, init, True)
        state = (qi - 1, margin(qi - 1, carry[0])) + carry
        _, _, m, l, acc = lax.while_loop(lambda st: (st[0] >= 0) & (st[1] > -SKIP), walk, state)
        o_ref[...] = acc / l
        lse_ref[0] = m + jnp.log(l)

    res, lands = _hosted_call(
        body, ride, name=name, grid=(n_heads, s_len // tq),
        in_specs=[pl.BlockSpec((tq, HEAD_DIM), lambda h, i: (i, h)),
                  pl.BlockSpec((s_len, HEAD_DIM), lambda h, i: (0, n_heads + h)),
                  pl.BlockSpec((s_len, HEAD_DIM), lambda h, i: (0, 2 * n_heads + h)),
                  pl.BlockSpec((tq, LANES), lambda h, i: (i, 0)),
                  pl.BlockSpec((hp, s_len), lambda h, i: (0, 0))],
        out_specs=[pl.BlockSpec((tq, HEAD_DIM), lambda h, i: (i, h)),
                   pl.BlockSpec((1, tq, 1), lambda h, i: (h, i, 0))],
        out_shape=[jax.ShapeDtypeStruct((s_len, n_heads * HEAD_DIM), F32),
                   jax.ShapeDtypeStruct((n_heads, s_len, 1), F32)],
        scratch_shapes=[pltpu.VMEM((1, 1), F32)], sem=("parallel", "arbitrary"), args=[pa, pa, pa, c, ct])
    return res if ride is None else (res, lands)


def _fox_bwd(name, pa, c, ct, o, do, lse, n_heads, tq=1024):
    s_len = pa.shape[0]
    tq = _tile(s_len, tq, LANES)
    hp = ct.shape[0]
    w_dim = n_heads * HEAD_DIM

    def body(q_ref, k_ref, v_ref, c_ref, ct_ref, o_ref, do_ref, lse_ref, dq_ref, dk_ref, dv_ref, dct_ref, dcq_ref,
             kmax_ref):
        h, qi = pl.program_id(0), pl.program_id(1)

        @pl.when(qi == 0)
        def _():
            dk_ref[...] = jnp.zeros_like(dk_ref)
            dv_ref[...] = jnp.zeros_like(dv_ref)
            kmax_ref[...] = _key_norm_max(k_ref)

        @pl.when((qi == 0) & (h == 0))
        def _():
            dct_ref[...] = jnp.zeros_like(dct_ref)

        q = q_ref[...]
        do32 = do_ref[...]
        dob = do32.astype(BF16)
        dsum = jnp.sum(do32 * o_ref[...], axis=1, keepdims=True)
        lse_v = lse_ref[0]
        cq = _head_col(c_ref[...], h)
        row = lax.broadcasted_iota(jnp.int32, (tq, tq), 0)
        col = lax.broadcasted_iota(jnp.int32, (tq, tq), 1)

        def step(kb, carry, masked):
            dq, dcq = carry
            ks = pl.multiple_of(kb * tq, tq)
            k = k_ref[pl.ds(ks, tq), :]
            v = v_ref[pl.ds(ks, tq), :]
            ck = ct_ref[pl.ds(h, 1), pl.ds(ks, tq)]
            s = lax.dot_general(q, k, NT_DIMS, preferred_element_type=F32) + (cq - ck)
            p = jnp.exp(s - lse_v)
            if masked:
                p = jnp.where(col <= row, p, 0.0)
            dp = lax.dot_general(dob, v, NT_DIMS, preferred_element_type=F32)
            ds = p * (dp - dsum)
            dsb = ds.astype(BF16)
            dk_ref[pl.ds(ks, tq), :] += lax.dot_general(dsb, q, TN_DIMS, preferred_element_type=F32)
            dv_ref[pl.ds(ks, tq), :] += lax.dot_general(p.astype(BF16), dob, TN_DIMS, preferred_element_type=F32)
            dct_ref[pl.ds(h, 1), pl.ds(ks, tq)] -= jnp.sum(ds, axis=0, keepdims=True)
            return dq + jnp.dot(dsb, k, preferred_element_type=F32), dcq + jnp.sum(ds, axis=1, keepdims=True)

        top = jnp.max(_logit_bound(q, kmax_ref[...]) + cq - lse_v)

        def margin(kb):
            return top - _c_block_end(ct_ref, h, jnp.maximum(kb, 0), tq)

        def walk(state):
            kb, _, dq, dcq = state
            dq, dcq = step(kb, (dq, dcq), False)
            return kb - 1, margin(kb - 1), dq, dcq

        init = (jnp.zeros((tq, HEAD_DIM), F32), jnp.zeros((tq, 1), F32))
        state = (qi - 1, margin(qi - 1)) + step(qi, init, True)
        _, _, dq, dcq = lax.while_loop(lambda st: (st[0] >= 0) & (st[1] > -SKIP), walk, state)
        dq_ref[...] = dq * QK_SCALE
        dcq_ref[0] = dcq

    blk = pl.BlockSpec((tq, HEAD_DIM), lambda h, i: (i, h))
    full = pl.BlockSpec((s_len, HEAD_DIM), lambda h, i: (0, h))
    return pl.pallas_call(
        body, name=name, grid=(n_heads, s_len // tq),
        in_specs=[blk,
                  pl.BlockSpec((s_len, HEAD_DIM), lambda h, i: (0, n_heads + h)),
                  pl.BlockSpec((s_len, HEAD_DIM), lambda h, i: (0, 2 * n_heads + h)),
                  pl.BlockSpec((tq, LANES), lambda h, i: (i, 0)),
                  pl.BlockSpec((hp, s_len), lambda h, i: (0, 0)),
                  blk, blk,
                  pl.BlockSpec((1, tq, 1), lambda h, i: (h, i, 0))],
        out_specs=[blk, full, full, pl.BlockSpec((hp, s_len), lambda h, i: (0, 0)),
                   pl.BlockSpec((1, tq, 1), lambda h, i: (h, i, 0))],
        out_shape=[jax.ShapeDtypeStruct((s_len, w_dim), F32)] * 3 + [jax.ShapeDtypeStruct((hp, s_len), F32),
                                                                     jax.ShapeDtypeStruct((n_heads, s_len, 1), F32)],
        scratch_shapes=[pltpu.VMEM((1, 1), F32)],
        compiler_params=_params(("arbitrary", "arbitrary")))(pa, pa, pa, c, ct, o, do, lse)


def _sb_fwd(name, pa, n_heads, tq=512, tk=256, ride=None):
    s_len = pa.shape[0]
    tq = _tile(s_len, tq, LANES)
    tk = _tile(tq, tk, LANES)
    nsub = tq // tk

    def body(q_ref, k_ref, v_ref, o_ref, tot_ref, kst_ref, kmax_ref):
        qi = pl.program_id(1)

        @pl.when(qi == 0)
        def _():
            kmax_ref[...] = _key_norm_max(k_ref)

        q = q_ref[...]
        z_bound = _logit_bound(q, kmax_ref[...])
        u = _tri(tk, "ge")
        row = lax.broadcasted_iota(jnp.int32, (tq, tk), 0)
        col = lax.broadcasted_iota(jnp.int32, (tq, tk), 1)

        def block(ks, carry, mask_off, row0=0):
            r, acc = (c[row0:] for c in carry)
            k = k_ref[pl.ds(ks, tk), :]
            v = v_ref[pl.ds(ks, tk), :]
            z = lax.dot_general(q[row0:], k, NT_DIMS, preferred_element_type=F32)
            a, _ = _neg_softplus(z)
            if mask_off is not None:
                valid = col[row0:] + mask_off < row[row0:]
                a = jnp.where(valid, a, 0.0)
            rin = _split_dot(a, u, 2)
            w = jnp.exp(z + (r + rin))
            if mask_off is not None:
                w = jnp.where(valid, w, 0.0)
            new = (r + rin[:, 0:1], acc + jnp.dot(w.astype(BF16), v, preferred_element_type=F32))
            return tuple(jnp.concatenate([c[:row0], n], axis=0) if row0 else n for c, n in zip(carry, new))

        carry = (jnp.zeros((tq, 1), F32), jnp.zeros((tq, HEAD_DIM), F32))
        q0 = pl.multiple_of(qi * tq, tq)
        for j in reversed(range(nsub)):
            carry = block(q0 + j * tk, carry, j * tk, j * tk)
        def walk(state):
            kb, _, r, acc = state
            r, acc = block(pl.multiple_of(kb * tk, tk), (r, acc), None)
            return kb - 1, jnp.max(r + z_bound), r, acc

        state = (qi * nsub - 1, jnp.max(carry[0] + z_bound)) + carry
        kb, _, r, acc = lax.while_loop(lambda st: (st[0] >= 0) & (st[1] > -SKIP), walk, state)
        o_ref[...] = acc
        tot_ref[0] = r
        kst_ref[...] = jnp.full(kst_ref.shape, (kb + 1).astype(F32))

    nq = s_len // tq
    res, lands = _hosted_call(
        body, ride, name=name, grid=(n_heads, nq),
        in_specs=[pl.BlockSpec((tq, HEAD_DIM), lambda h, i: (i, 3 * n_heads + h)),
                  pl.BlockSpec((s_len, HEAD_DIM), lambda h, i: (0, 4 * n_heads + h)),
                  pl.BlockSpec((s_len, HEAD_DIM), lambda h, i: (0, 5 * n_heads + h))],
        out_specs=[pl.BlockSpec((tq, HEAD_DIM), lambda h, i: (i, h)),
                   pl.BlockSpec((1, tq, 1), lambda h, i: (h, i, 0)),
                   pl.BlockSpec((1, 1, 8, LANES), lambda h, i: (h, i, 0, 0))],
        out_shape=[jax.ShapeDtypeStruct((s_len, n_heads * HEAD_DIM), F32),
                   jax.ShapeDtypeStruct((n_heads, s_len, 1), F32),
                   jax.ShapeDtypeStruct((n_heads, nq, 8, LANES), F32)],
        scratch_shapes=[pltpu.VMEM((1, 1), F32)], sem=("parallel", "arbitrary"), args=[pa, pa, pa])
    return res if ride is None else (res, lands)


def _sb_bwd(name, pa, do, tot, kst, n_heads, tq=512, tk=256):
    s_len = pa.shape[0]
    tq = _tile(s_len, tq, LANES)
    tk = _tile(tq, tk, LANES)
    nsub = tq // tk
    w_dim = n_heads * HEAD_DIM

    def body(q_ref, k_ref, v_ref, do_ref, tot_ref, kst_ref, dq_ref, dk_ref, dv_ref):
        qi = pl.program_id(1)

        @pl.when(qi == 0)
        def _():
            dk_ref[...] = jnp.zeros_like(dk_ref)
            dv_ref[...] = jnp.zeros_like(dv_ref)

        q = q_ref[...]
        dob = do_ref[...].astype(BF16)
        u = _tri(tk, "le")
        row = lax.broadcasted_iota(jnp.int32, (tq, tk), 0)
        col = lax.broadcasted_iota(jnp.int32, (tq, tk), 1)

        def block(ks, carry, mask_off, row0=0):
            rem, cpre, dq = (c[row0:] for c in carry)
            qs, dos = q[row0:], dob[row0:]
            k = k_ref[pl.ds(ks, tk), :]
            v = v_ref[pl.ds(ks, tk), :]
            z = lax.dot_general(qs, k, NT_DIMS, preferred_element_type=F32)
            a, e = _neg_softplus(z)
            if mask_off is not None:
                valid = col[row0:] + mask_off < row[row0:]
                a = jnp.where(valid, a, 0.0)
            pin = _split_dot(a, u, 2)
            w = jnp.exp(z + (rem - (pin - a)))
            if mask_off is not None:
                w = jnp.where(valid, w, 0.0)
            g = w * lax.dot_general(dos, v, NT_DIMS, preferred_element_type=F32)
            cin = _split_dot(g, u, 1)
            beta = jnp.where(z >= 0, 1.0, e) / (1.0 + e)
            dz = g - beta * (cpre + cin)
            if mask_off is not None:
                dz = jnp.where(valid, dz, 0.0)
            dzb = dz.astype(BF16)
            dk_ref[pl.ds(ks, tk), :] += lax.dot_general(dzb, qs, TN_DIMS, preferred_element_type=F32)
            dv_ref[pl.ds(ks, tk), :] += lax.dot_general(w.astype(BF16), dos, TN_DIMS, preferred_element_type=F32)
            new = (rem - pin[:, tk - 1:tk], cpre + cin[:, tk - 1:tk], dq + jnp.dot(dzb, k, preferred_element_type=F32))
            return tuple(jnp.concatenate([c[:row0], n], axis=0) if row0 else n for c, n in zip(carry, new))

        carry = (tot_ref[0], jnp.zeros((tq, 1), F32), jnp.zeros((tq, HEAD_DIM), F32))
        first = jnp.max(kst_ref[0, 0]).astype(jnp.int32)
        carry = lax.fori_loop(first, qi * nsub, lambda n, cr: block(pl.multiple_of(n * tk, tk), cr, None), carry)
        q0 = pl.multiple_of(qi * tq, tq)
        for j in range(nsub):
            carry = block(q0 + j * tk, carry, j * tk, j * tk)
        dq_ref[...] = carry[2] * QK_SCALE

    blk = pl.BlockSpec((tq, HEAD_DIM), lambda h, i: (i, h))
    full = pl.BlockSpec((s_len, HEAD_DIM), lambda h, i: (0, h))
    return pl.pallas_call(
        body, name=name, grid=(n_heads, s_len // tq),
        in_specs=[pl.BlockSpec((tq, HEAD_DIM), lambda h, i: (i, 3 * n_heads + h)),
                  pl.BlockSpec((s_len, HEAD_DIM), lambda h, i: (0, 4 * n_heads + h)),
                  pl.BlockSpec((s_len, HEAD_DIM), lambda h, i: (0, 5 * n_heads + h)),
                  blk,
                  pl.BlockSpec((1, tq, 1), lambda h, i: (h, i, 0)),
                  pl.BlockSpec((1, 1, 8, LANES), lambda h, i: (h, i, 0, 0))],
        out_specs=[blk, full, full],
        out_shape=[jax.ShapeDtypeStruct((s_len, w_dim), F32)] * 3,
        compiler_params=_params(("arbitrary", "arbitrary")))(pa, pa, pa, do, tot, kst)


CONV_HEAD = 16
CONV_CHUNK = 32


def _stage_head(ext_ref, u_ref, halo_ref, first):
    ext_ref[0:8, :] = jnp.where(first, 0.0, halo_ref[...])
    ext_ref[8:8 + CONV_HEAD, :] = u_ref[0:CONV_HEAD, :]


def _taps(ref, r0, n):
    return ref[r0:r0 + n, :], ref[r0 - 1:r0 - 1 + n, :], ref[r0 - 2:r0 - 2 + n, :]


def _row_chunks(tm):
    return [(True, 0, CONV_HEAD)] + [(False, r0, min(CONV_CHUNK, tm - r0)) for r0 in range(CONV_HEAD, tm, CONV_CHUNK)]


def _conv3(us, w_ref, b_ref):
    return w_ref[2:3, :] * us[0] + w_ref[1:2, :] * us[1] + w_ref[0:1, :] * us[2] + b_ref[...]


def _fold8(x):
    return jnp.sum(x.reshape(x.shape[0] // 8, 8, x.shape[1]), axis=0)


def _sigmoid(x):
    return 0.5 * jnp.tanh(0.5 * x) + 0.5


def _conv_specs(tm, tc, nj, order):
    hb = tm // 8
    ij = order

    def at(f):
        return lambda *g: f(*ij(*g))

    return [pl.BlockSpec((tm, tc), at(lambda i, j: (i, j))),
            pl.BlockSpec((tm, tc), at(lambda i, j: (i, j + nj))),
            pl.BlockSpec((8, tc), at(lambda i, j: (jnp.maximum(i * hb - 1, 0), j))),
            pl.BlockSpec((8, tc), at(lambda i, j: (jnp.maximum(i * hb - 1, 0), j + nj))),
            pl.BlockSpec((3, tc), at(lambda i, j: (0, j))),
            pl.BlockSpec((3, tc), at(lambda i, j: (0, j + nj))),
            pl.BlockSpec((1, tc), at(lambda i, j: (0, j))),
            pl.BlockSpec((1, tc), at(lambda i, j: (0, j + nj)))]


def _conv_gate_fwd(name, u, cw, cb, tm=1024, tc=512):
    s_len, f2 = u.shape
    f_dim = f2 // 2
    tm = _tile(s_len, tm, 8)
    tc = _tile(f_dim, tc)
    nj = f_dim // tc

    def body(ug_ref, uv_ref, hg_ref, hv_ref, wg_ref, wv_ref, bg_ref, bv_ref, g_ref, eg_ref, ev_ref):
        first = pl.program_id(0) == 0
        _stage_head(eg_ref, ug_ref, hg_ref, first)
        _stage_head(ev_ref, uv_ref, hv_ref, first)
        for head, r0, n in _row_chunks(tm):
            gc = _conv3(_taps(eg_ref, 8, n) if head else _taps(ug_ref, r0, n), wg_ref, bg_ref)
            vc = _conv3(_taps(ev_ref, 8, n) if head else _taps(uv_ref, r0, n), wv_ref, bv_ref)
            g_ref[r0:r0 + n, :] = (gc * _sigmoid(gc) * vc).astype(BF16)

    return pl.pallas_call(
        body, name=name, grid=(s_len // tm, nj),
        in_specs=_conv_specs(tm, tc, nj, lambda i, j: (i, j)),
        out_specs=pl.BlockSpec((tm, tc), lambda i, j: (i, j)),
        out_shape=jax.ShapeDtypeStruct((s_len, f_dim), BF16),
        scratch_shapes=[pltpu.VMEM((8 + CONV_HEAD, tc), F32)] * 2,
        compiler_params=_params(("parallel", "parallel")))(u, u, u, u, cw, cw, cb, cb)


def _conv_gate_bwd(name, u, dg, cw, cb, tm=1024, tc=512):
    s_len, f2 = u.shape
    f_dim = f2 // 2
    tm = _tile(s_len, tm, 8)
    tc = _tile(f_dim, tc)
    nj = f_dim // tc

    def body(ug_ref, uv_ref, hg_ref, hv_ref, wg_ref, wv_ref, bg_ref, bv_ref, dg_ref, duc_ref, dcw_ref, dcb_ref,
             eg_ref, ev_ref):
        first = pl.program_id(1) == 0

        @pl.when(first)
        def _():
            dcw_ref[...] = jnp.zeros_like(dcw_ref)
            dcb_ref[...] = jnp.zeros_like(dcb_ref)

        _stage_head(eg_ref, ug_ref, hg_ref, first)
        _stage_head(ev_ref, uv_ref, hv_ref, first)
        sums = [[jnp.zeros((8, tc), F32) for _ in range(4)] for _ in range(2)]
        for head, r0, n in _row_chunks(tm):
            ug = _taps(eg_ref, 8, n) if head else _taps(ug_ref, r0, n)
            uv = _taps(ev_ref, 8, n) if head else _taps(uv_ref, r0, n)
            gc = _conv3(ug, wg_ref, bg_ref)
            vc = _conv3(uv, wv_ref, bv_ref)
            sg = _sigmoid(gc)
            dgv = dg_ref[r0:r0 + n, :]
            dvc = dgv * (gc * sg)
            dgc = dgv * vc * (sg * (1.0 + gc * (1.0 - sg)))
            duc_ref[0, r0:r0 + n, :] = dgc
            duc_ref[1, r0:r0 + n, :] = dvc
            for half, (d, us) in enumerate(((dgc, ug), (dvc, uv))):
                sums[half][3] = sums[half][3] + _fold8(d)
                for tap in range(3):
                    sums[half][tap] = sums[half][tap] + _fold8(d * us[2 - tap])
        for half in range(2):
            dcb_ref[half] += jnp.sum(sums[half][3], axis=0, keepdims=True)
            for tap in range(3):
                dcw_ref[half, tap:tap + 1, :] += jnp.sum(sums[half][tap], axis=0, keepdims=True)

    order = lambda j, i: (i, j)
    return pl.pallas_call(
        body, name=name, grid=(nj, s_len // tm),
        in_specs=_conv_specs(tm, tc, nj, order) + [pl.BlockSpec((tm, tc), lambda j, i: (i, j))],
        out_specs=[pl.BlockSpec((2, tm, tc), lambda j, i: (0, i, j)),
                   pl.BlockSpec((2, 3, tc), lambda j, i: (0, 0, j)),
                   pl.BlockSpec((2, 1, tc), lambda j, i: (0, 0, j))],
        out_shape=[jax.ShapeDtypeStruct((2, s_len, f_dim), F32), jax.ShapeDtypeStruct((2, 3, f_dim), F32),
                   jax.ShapeDtypeStruct((2, 1, f_dim), F32)],
        scratch_shapes=[pltpu.VMEM((8 + CONV_HEAD, tc), F32)] * 2,
        compiler_params=_params(("parallel", "arbitrary")))(u, u, u, u, cw, cw, cb, cb, dg)


def _conv_t(name, duc, cw, tm=1024, tc=512):
    _, s_len, f_dim = duc.shape
    tm = _tile(s_len, tm, 8)
    tc = _tile(f_dim, tc)
    nj = f_dim // tc
    nb = s_len // tm
    hb = tm // 8

    def body(d_ref, halo_ref, w_ref, o_ref, ext_ref):
        last = pl.program_id(1) == nb - 1
        tail = tm - CONV_HEAD
        ext_ref[0:CONV_HEAD, :] = d_ref[0, tail:tm, :]
        ext_ref[CONV_HEAD:, :] = jnp.where(last, 0.0, halo_ref[0])

        def out(ref, r0, n):
            return (w_ref[2:3, :] * ref[r0:r0 + n, :] + w_ref[1:2, :] * ref[r0 + 1:r0 + 1 + n, :]
                    + w_ref[0:1, :] * ref[r0 + 2:r0 + 2 + n, :]).astype(BF16)

        for r0 in range(0, tail, CONV_CHUNK):
            n = min(CONV_CHUNK, tail - r0)
            o_ref[r0:r0 + n, :] = out(d_ref.at[0], r0, n)
        o_ref[tail:tm, :] = out(ext_ref, 0, CONV_HEAD)

    return pl.pallas_call(
        body, name=name, grid=(2, nb, nj),
        in_specs=[pl.BlockSpec((1, tm, tc), lambda p, i, j: (p, i, j)),
                  pl.BlockSpec((1, 8, tc), lambda p, i, j: (p, jnp.minimum((i + 1) * hb, nb * hb - 1), j)),
                  pl.BlockSpec((3, tc), lambda p, i, j: (0, p * nj + j))],
        out_specs=pl.BlockSpec((tm, tc), lambda p, i, j: (i, p * nj + j)),
        out_shape=jax.ShapeDtypeStruct((s_len, 2 * f_dim), BF16),
        scratch_shapes=[pltpu.VMEM((CONV_HEAD + 8, tc), F32)],
        compiler_params=_params(("parallel", "parallel", "parallel")))(duc, duc, cw)


def _loss_head(name, y, tgt, tm=512):
    s_len, d = y.shape
    tm = _tile(s_len, tm, 8)

    def body(y_ref, t_ref, dy_ref, l_ref):
        @pl.when(pl.program_id(0) == 0)
        def _():
            l_ref[...] = jnp.zeros_like(l_ref)

        err = y_ref[...] - t_ref[...]
        dy_ref[...] = err * (1.0 / d)
        l_ref[...] += 0.5 * jnp.sum(jnp.sum(err * err, axis=1, keepdims=True) * (1.0 / d), axis=0, keepdims=True)

    blk = pl.BlockSpec((tm, d), lambda i: (i, 0))
    return pl.pallas_call(
        body, name=name, grid=(s_len // tm,), in_specs=[blk, blk],
        out_specs=[blk, pl.BlockSpec((1, LANES), lambda i: (0, 0))],
        out_shape=[jax.ShapeDtypeStruct((s_len, d), F32), jax.ShapeDtypeStruct((1, LANES), F32)],
        compiler_params=_params(("arbitrary",)))(y, tgt)


def _adamw(name, parts_list, w, m, v, tr=256, ride=None):
    n_l = len(parts_list)
    n_parts, rows, cols = parts_list[0].shape
    row_bytes = -(-cols // LANES) * LANES * (2 * n_l * n_parts * parts_list[0].dtype.itemsize + 2 * 7 * 4)
    tr = _tile(rows, min(tr, max(16, ADAMW_VMEM // row_bytes // 16 * 16)), 16)
    nb = rows // tr
    c1 = 1.0 - ADAM_B1 ** ADAM_STEP
    c2 = 1.0 - ADAM_B2 ** ADAM_STEP

    def body(*refs):
        p_refs = refs[:n_l]
        w_ref, m_ref, v_ref, g_ref, d_ref, nm_ref, nv_ref = refs[n_l:]
        for l, p_ref in enumerate(p_refs):
            @pl.when(pl.program_id(0) == l)
            def _(p_ref=p_ref):
                g = p_ref[0].astype(F32)
                for n in range(1, n_parts):
                    g = g + p_ref[n].astype(F32)
                nm = ADAM_B1 * m_ref[...] + (1.0 - ADAM_B1) * g
                nv = ADAM_B2 * v_ref[...] + (1.0 - ADAM_B2) * (g * g)
                g_ref[...] = g
                nm_ref[...] = nm
                nv_ref[...] = nv
                d_ref[...] = -ADAM_LR * ((nm / c1) / (jnp.sqrt(nv / c2) + ADAM_EPS) + ADAM_WD * w_ref[...])

    p_specs = [pl.BlockSpec((n_parts, tr, cols), lambda li, i, l=l: (0, jnp.where(li == l, i, 0), 0))
               for l in range(n_l)]
    blk = pl.BlockSpec((tr, cols), lambda li, i: (li * nb + i, 0))
    res, lands = _hosted_call(
        body, ride, name=name, grid=(n_l, nb), in_specs=p_specs + [blk, blk, blk], out_specs=[blk] * 4,
        out_shape=[jax.ShapeDtypeStruct((n_l * rows, cols), F32)] * 4, sem=("arbitrary", "arbitrary"),
        args=[*parts_list, w, m, v])
    return res if ride is None else (res, lands)


def _peers():
    x, y, c = lax.axis_index("x"), lax.axis_index("y"), lax.axis_index("c")
    out = []
    for k in range(1, N_DEV):
        fx, fy, fc = (k >> 2) & 1, (k >> 1) & 1, k & 1
        px, py, pc = x ^ fx, y ^ fy, c ^ fc
        out.append((k - 1, (px, py, pc), 4 * px + 2 * py + pc))
    return 4 * x + 2 * y + c, out


def _block_of(ref, axis, index, size):
    if axis is None:
        return ref.at[index]
    idx = [slice(None)] * len(ref.shape)
    idx[axis] = pl.ds(pl.multiple_of(index * size, size), size)
    return ref.at[tuple(idx)]


def _land_shape(shape, axis, scatter):
    shape = list(shape)
    if scatter:
        if axis is None:
            return tuple(shape)
        shape[axis] //= N_DEV
        return (N_DEV, *shape)
    if axis is None:
        return (N_DEV, *shape)
    shape[axis] *= N_DEV
    return tuple(shape)


def _copy_ends(axis, scatter, src, land, me, idx):
    if scatter:
        size = None if axis is None else src.shape[axis] // N_DEV
        return _block_of(src, axis, idx, size), land.at[me]
    return src, _block_of(land, axis, me, None if axis is None else src.shape[axis])


def _remote_copies(axes, scatter, in_refs, land_refs, send_sems, recv_sems):
    me, peers = _peers()
    out = []
    for a, (axis, src, land) in enumerate(zip(axes, in_refs, land_refs)):
        for k, pos, idx in peers:
            s, d = _copy_ends(axis, scatter, src, land, me, idx)
            out.append(pltpu.make_async_remote_copy(
                src_ref=s, dst_ref=d, send_sem=send_sems.at[a * (N_DEV - 1) + k],
                recv_sem=recv_sems.at[a * (N_DEV - 1) + k],
                device_id=pos, device_id_type=MESH))
    return out


def _exchange(name, arrays, axes, scatter):
    n = len(arrays)

    def body(*refs):
        copies = _all_copies(axes, scatter, refs[:n], refs[n:2 * n], *refs[2 * n:])
        for cp in copies:
            cp.start()
        for cp in copies:
            cp.wait()

    any_spec = pl.BlockSpec(memory_space=pl.ANY)
    return pl.pallas_call(
        body, name=name, in_specs=[any_spec] * n, out_specs=[any_spec] * n,
        out_shape=[jax.ShapeDtypeStruct(_land_shape(a.shape, ax, scatter), a.dtype) for a, ax in zip(arrays, axes)],
        scratch_shapes=_exchange_sems(n),
        compiler_params=pltpu.CompilerParams(has_side_effects=True))(*arrays)


def _all_copies(axes, scatter, in_refs, land_refs, send_sems, recv_sems, local_sems):
    me, _ = _peers()
    copies = _remote_copies(axes, scatter, in_refs, land_refs, send_sems, recv_sems)
    for a, (axis, src, land) in enumerate(zip(axes, in_refs, land_refs)):
        s, d = _copy_ends(axis, scatter, src, land, me, me)
        copies.append(pltpu.make_async_copy(s, d, local_sems.at[a]))
    return copies


def _exchange_sems(n):
    return [pltpu.SemaphoreType.DMA((n * (N_DEV - 1),)), pltpu.SemaphoreType.DMA((n * (N_DEV - 1),)),
            pltpu.SemaphoreType.DMA((n,))]


def _hosted_call(body, ride, *, name, grid, in_specs, out_specs, out_shape, sem, args, scratch_shapes=()):
    if ride is None:
        return pl.pallas_call(body, name=name, grid=grid, in_specs=in_specs, out_specs=out_specs, out_shape=out_shape,
                              scratch_shapes=list(scratch_shapes), compiler_params=_params(sem))(*args), None
    arrays, axes, scatter = ride
    n, n_in, n_out, n_scr = len(arrays), len(in_specs), len(out_specs), len(scratch_shapes)

    def hosted(*refs):
        main_in, ride_in = refs[:n_in], refs[n_in:n_in + n]
        o0 = n_in + n
        main_out, lands = refs[o0:o0 + n_out], refs[o0 + n_out:o0 + n_out + n]
        s0 = o0 + n_out + n
        main_scr, sems = refs[s0:s0 + n_scr], refs[s0 + n_scr:]
        ids = [pl.program_id(i) for i in range(len(grid))]
        first = functools.reduce(jnp.logical_and, [i == 0 for i in ids])
        last = functools.reduce(jnp.logical_and, [i == g - 1 for i, g in zip(ids, grid)])

        @pl.when(first)
        def _():
            for cp in _all_copies(axes, scatter, ride_in, lands, *sems):
                cp.start()

        body(*main_in, *main_out, *main_scr)

        @pl.when(last)
        def _():
            for cp in _all_copies(axes, scatter, ride_in, lands, *sems):
                cp.wait()

    any_spec = pl.BlockSpec(memory_space=pl.ANY)
    res = pl.pallas_call(
        hosted, name=name, grid=grid, in_specs=list(in_specs) + [any_spec] * n,
        out_specs=list(out_specs) + [any_spec] * n,
        out_shape=list(out_shape) + [jax.ShapeDtypeStruct(_land_shape(a.shape, ax, scatter), a.dtype)
                                     for a, ax in zip(arrays, axes)],
        scratch_shapes=list(scratch_shapes) + _exchange_sems(n),
        compiler_params=pltpu.CompilerParams(dimension_semantics=("arbitrary",) * len(grid),
                                             vmem_limit_bytes=VMEM_LIMIT, has_side_effects=True))(*args, *arrays)
    return res[:n_out], res[n_out:]


def _row(vec, width=None):
    vec = vec.reshape(1, -1)
    if width is not None and vec.shape[1] < width:
        vec = jnp.pad(vec, ((0, 0), (0, width - vec.shape[1])))
    return vec


def _riding(rides, lands, own=None):
    def run(key, fn, *args, **kwargs):
        if rides is None or key not in rides:
            return fn(*args, **kwargs)
        arrays, axes, scatter = rides[key]
        arrays = [own[a] if isinstance(a, str) else a for a in arrays]
        res, lands[key] = fn(*args, ride=(arrays, axes, scatter), **kwargs)
        return res
    return run


def _layer_fwd(l, x, sp, bp, rides=None, late=None):
    n_heads = x.shape[1] // (2 * HEAD_DIM)
    lands = {}
    run = _riding(rides, lands)

    def weight(k):
        if k not in bp:
            bp[k] = lands[late[k][0]][late[k][1]]
        return bp[k]

    proj, h = run("in_proj", _mm_nn, f"l{l}_in_proj", [x], [_row(sp["attn_norm"])], bp["w_in"])
    qg, kg, bf = _row(sp["q_norm"]), _row(sp["k_norm"]), _row(sp["b_forget"], LANES)
    pa, c = _prep_fwd(f"l{l}_prep", proj, qg, kg, bf, n_heads)
    ct = c[:, :n_heads].T
    o_a, lse = run("fox", _fox_fwd, f"l{l}_fox", pa, c, ct, n_heads)
    o_b, tot, kst = run("sb", _sb_fwd, f"l{l}_sb", pa, n_heads)
    gfox, gsb = _row(sp["out_norm_fox"]), _row(sp["out_norm_sb"])
    x1, merged = run("out_proj", _mm_nn, f"l{l}_out_proj", [o_a, o_b], [gfox, gsb], weight("w_out"), resid=x)
    u, h2 = run("up_proj", _mm_nn, f"l{l}_up_proj", [x1], [_row(sp["ffn_norm"])], weight("w_up"))
    cb = _row(sp["conv_b"])
    g = _conv_gate_fwd(f"l{l}_conv_gate", u, bp["conv_w"], cb)
    x2 = run("down_proj", _mm_nn, f"l{l}_down_proj", [g], [], weight("w_down"), resid=x1, tn_cap=512)
    saved = dict(x=x, h=h, proj=proj, pa=pa, c=c, ct=ct, o_a=o_a, lse=lse, o_b=o_b, tot=tot, kst=kst, merged=merged, x1=x1,
                 h2=h2, u=u, g=g, qg=qg, kg=kg, bf=bf, gfox=gfox, gsb=gsb, cb=cb)
    return x2, saved, lands


def _layer_bwd(l, dx, sp, bp, sv, rides=None):
    n_heads = dx.shape[1] // (2 * HEAD_DIM)
    gr, lands = {}, {}
    run = _riding(rides, lands, gr)
    dg = run("d_down_act", _mm_nt, f"l{l}_d_down_act", dx, bp["w_down"])
    gr["w_down"] = _mm_tn(f"l{l}_d_w_down", sv["g"], dx)
    duc, dcw, dcb = _conv_gate_bwd(f"l{l}_d_conv_gate", sv["u"], dg, bp["conv_w"], sv["cb"])
    gr["conv_w"] = dcw.transpose(1, 0, 2).reshape(3, -1)
    gr["conv_b"] = dcb.reshape(-1)
    du = _conv_t(f"l{l}_d_conv", duc, bp["conv_w"])
    gr["w_up"] = run("d_w_up", _mm_tn, f"l{l}_d_w_up", sv["h2"], du)
    dx1, dffn = run("d_up_act", _mm_nt, f"l{l}_d_up_act", du, bp["w_up"], "rms_bwd", [sv["x1"]],
                    [_row(sp["ffn_norm"])], dres=dx, tk_cap=1408)
    gr["ffn_norm"] = dffn.reshape(-1)
    gr["w_out"] = _mm_tn(f"l{l}_d_w_out", sv["merged"], dx1)
    do_a, do_b, dgfox, dgsb = _mm_nt(f"l{l}_d_out_act", dx1, bp["w_out"], "rms2_bwd",
                                     [sv["o_a"], sv["o_b"]], [sv["gfox"], sv["gsb"]], tk_cap=2048)
    gr["out_norm_fox"], gr["out_norm_sb"] = dgfox.reshape(-1), dgsb.reshape(-1)
    dq_a, dk_a, dv_a, dct, dcq = _fox_bwd(f"l{l}_d_fox", sv["pa"], sv["c"], sv["ct"], sv["o_a"], do_a, sv["lse"],
                                          n_heads)
    d_sb = _sb_bwd(f"l{l}_d_sb", sv["pa"], do_b, sv["tot"], sv["kst"], n_heads)
    dc = jnp.pad((dct + dcq[:, :, 0]).T, ((0, 0), (0, LANES - n_heads)))
    dproj, dqg, dkg, dbf = _prep_bwd(f"l{l}_d_prep", sv["proj"], sv["qg"], sv["kg"], sv["bf"],
                                     (dq_a, dk_a, dv_a), d_sb, dc, n_heads)
    gr["q_norm"], gr["k_norm"], gr["b_forget"] = dqg.reshape(-1), dkg.reshape(-1), dbf.reshape(-1)[:n_heads]
    gr["w_in"] = run("d_w_in", _mm_tn, f"l{l}_d_w_in", sv["h"], dproj)
    dx0, dattn = run("d_in_act", _mm_nt, f"l{l}_d_in_act", dproj, bp["w_in"], "rms_bwd", [sv["x"]],
                     [_row(sp["attn_norm"])], dres=dx1)
    gr["attn_norm"] = dattn.reshape(-1)
    return dx0, gr, lands


def _local_step(x, tgt, small, big):
    n_layers = len(big)
    saved = []
    for l in range(n_layers):
        x, sv, _ = _layer_fwd(l, x, small[l], big[l])
        saved.append(sv)
    dx, loss_part = _loss_head("loss_head", x, tgt)
    grads = [None] * n_layers
    for l in reversed(range(n_layers)):
        dx, grads[l], _ = _layer_bwd(l, dx, small[l], big[l], saved[l])
    return loss_part, dx, grads


def _w_in_to_internal(w, n_heads):
    w3 = 3 * n_heads * HEAD_DIM
    pad = jnp.zeros(w.shape[:-1] + (LANES - n_heads,), w.dtype)
    return jnp.concatenate([w[..., :w3], w[..., w3 + n_heads:], w[..., w3:w3 + n_heads], pad], axis=-1)


def _w_in_from_internal(w, n_heads):
    w3 = 3 * n_heads * HEAD_DIM
    return jnp.concatenate([w[..., :w3], w[..., 2 * w3:2 * w3 + n_heads], w[..., w3:2 * w3]], axis=-1)


SMALL = ("attn_norm", "b_forget", "q_norm", "k_norm", "out_norm_fox", "out_norm_sb", "ffn_norm", "conv_b")
BIG = ("w_in", "w_out", "w_up", "w_down")
WEIGHTS = ("attn_norm", "w_in", "b_forget", "q_norm", "k_norm", "out_norm_fox", "out_norm_sb", "w_out", "ffn_norm",
           "w_up", "conv_w", "conv_b", "w_down")


def kernel(x, attn_norm, w_in, b_forget, q_norm, k_norm, out_norm_fox, out_norm_sb, w_out, ffn_norm, w_up, conv_w, conv_b, w_down, loss_target, m_attn_norm, m_w_in, m_b_forget, m_q_norm, m_k_norm, m_out_norm_fox, m_out_norm_sb, m_w_out, m_ffn_norm, m_w_up, m_conv_w, m_conv_b, m_w_down, v_attn_norm, v_w_in, v_b_forget, v_q_norm, v_k_norm, v_out_norm_fox, v_out_norm_sb, v_w_out, v_ffn_norm, v_w_up, v_conv_w, v_conv_b, v_w_down):
    w = dict(attn_norm=attn_norm, w_in=w_in, b_forget=b_forget, q_norm=q_norm, k_norm=k_norm,
             out_norm_fox=out_norm_fox, out_norm_sb=out_norm_sb, w_out=w_out, ffn_norm=ffn_norm, w_up=w_up,
             conv_w=conv_w, conv_b=conv_b, w_down=w_down)
    mom = dict(attn_norm=m_attn_norm, w_in=m_w_in, b_forget=m_b_forget, q_norm=m_q_norm, k_norm=m_k_norm,
               out_norm_fox=m_out_norm_fox, out_norm_sb=m_out_norm_sb, w_out=m_w_out, ffn_norm=m_ffn_norm,
               w_up=m_w_up, conv_w=m_conv_w, conv_b=m_conv_b, w_down=m_w_down)
    var = dict(attn_norm=v_attn_norm, w_in=v_w_in, b_forget=v_b_forget, q_norm=v_q_norm, k_norm=v_k_norm,
               out_norm_fox=v_out_norm_fox, out_norm_sb=v_out_norm_sb, w_out=v_w_out, ffn_norm=v_ffn_norm,
               w_up=v_w_up, conv_w=v_conv_w, conv_b=v_conv_b, w_down=v_w_down)
    n_layers, d = attn_norm.shape
    n_heads = d // (2 * HEAD_DIM)
    me = 4 * lax.axis_index("x") + 2 * lax.axis_index("y") + lax.axis_index("c")

    shard = {k: w[k].astype(BF16) for k in BIG}
    axis_of = dict(w_in=None, w_out=0, w_up=1, w_down=0)
    host_fwd = dict(w_in="in_proj", w_out="out_proj", w_up="up_proj", w_down="down_proj")
    host_bwd = dict(w_in="d_up_act", w_out="d_down_act", w_up="d_w_up", w_down="d_in_act")
    whole_w_in = lambda g: _w_in_to_internal(g.transpose(1, 0, 2).reshape(d, -1), n_heads)
    g_in, full_cw = _exchange("gather_first", [shard["w_in"][0], w["conv_w"]], [None, 2], scatter=False)
    small = [{k: w[k][l] for k in SMALL} for l in range(n_layers)]
    big = [dict(w_in=whole_w_in(g_in), conv_w=full_cw[0])]

    saved = [None] * n_layers
    act = x[0]
    for l in range(n_layers):
        rides, late = {}, None
        if l + 1 < n_layers:
            rides = {host_fwd[k]: ([shard[k][l + 1]], [axis_of[k]], False) for k in BIG}
        if l == 0:
            late = dict(w_out=("in_proj", -1), w_up=("fox", -1), w_down=("sb", -1))
            for k, (host, _) in late.items():
                arrays, axes, _ = rides.get(host, ([], [], False))
                rides[host] = (arrays + [shard[k][0]], axes + [axis_of[k]], False)
        act, saved[l], lands = _layer_fwd(l, act, small[l], big[l], rides, late)
        if l + 1 < n_layers:
            nxt = {k: lands[host_fwd[k]][0] for k in BIG}
            big.append(dict(nxt, w_in=whole_w_in(nxt["w_in"]), conv_w=full_cw[l + 1]))
    dx, loss_part = _loss_head("loss_head", act, loss_target[0])
    grads, parts, sends = [None] * n_layers, [None] * n_layers, None
    for l in reversed(range(n_layers)):
        rides = {}
        if sends is not None:
            rides = {host_bwd[k]: ([sends[k]], [axis_of[k]], True) for k in BIG}
        if l == 0:
            for host, k in (("d_up_act", "w_down"), ("d_in_act", "w_out"), ("d_w_in", "w_up")):
                arrays, axes, _ = rides.get(host, ([], [], True))
                rides[host] = (arrays + [k], axes + [axis_of[k]], True)
        dx, grads[l], lands = _layer_bwd(l, dx, small[l], big[l], saved[l], rides)
        if sends is not None:
            parts[l + 1] = [lands[host_bwd[k]][0] for k in BIG]
        g_in = _w_in_from_internal(grads[l]["w_in"], n_heads)
        sends = dict(grads[l], w_in=g_in.reshape(d, N_DEV, -1).transpose(1, 0, 2))
    grad_x = dx
    parts[0] = [None, lands["d_in_act"][-1], lands["d_w_in"][-1], lands["d_up_act"][-1]]

    stack = lambda k: jnp.stack([grads[l][k] for l in range(n_layers)])
    small_names = SMALL + ("conv_w",)
    flat = jnp.concatenate([loss_part.reshape(-1)] + [
        jnp.pad(stack(k).reshape(-1), (0, (-stack(k).size) % LANES)) for k in small_names])
    flat = flat.reshape(-1, LANES)
    out = {}

    def update(name, parts_list, ride=None):
        rc = (-1, parts_list[0].shape[-1])
        res = _adamw("adamw_" + name, parts_list, w[name].reshape(rc), mom[name].reshape(rc), var[name].reshape(rc),
                     ride=ride)
        res, landed = res if ride is not None else (res, None)
        out[name] = [r.reshape(w[name].shape) for r in res]
        return landed

    layers_of = lambda name: [parts[l][BIG.index(name)] for l in range(n_layers)]
    (parts[0][0],) = update("w_up", layers_of("w_up"), ride=([sends["w_in"]], [None], True))
    (all_small,) = update("w_down", layers_of("w_down"), ride=([flat], [None], False))
    update("w_out", layers_of("w_out"))
    update("w_in", layers_of("w_in"))
    n_rows = flat.shape[0]
    w_flat, m_flat, v_flat = [], [], []
    for src, dst in ((w, w_flat), (mom, m_flat), (var, v_flat)):
        dst.append(jnp.zeros((LANES,), F32))
        for k in small_names:
            a = src[k]
            if k == "conv_w":
                a = jnp.zeros((n_layers, 3, conv_w.shape[2] * N_DEV), F32)
            dst.append(jnp.pad(a.reshape(-1), (0, (-a.size) % LANES)))
    pack = lambda parts: jnp.concatenate(parts).reshape(n_rows, LANES)
    res = _adamw("adamw_small", [all_small], pack(w_flat), pack(m_flat), pack(v_flat))
    res = [r.reshape(-1) for r in res]
    loss = res[0][0]
    off = LANES
    g_cw_full = None
    for k in small_names:
        size = n_layers * 3 * conv_w.shape[2] * N_DEV if k == "conv_w" else w[k].size
        if k == "conv_w":
            g_cw_full = res[0][off:off + size].reshape(n_layers, 3, -1)
        else:
            out[k] = [r[off:off + size].reshape(w[k].shape) for r in res]
        off += size + (-size) % LANES
    c_loc = conv_w.shape[2]
    g_cw_mine = lax.dynamic_slice_in_dim(g_cw_full, me * c_loc, c_loc, axis=2)
    update("conv_w", [g_cw_mine.reshape(1, n_layers * 3, c_loc)])

    outs = [loss, grad_x[None]]
    for n in range(4):
        outs += [out[k][n] for k in WEIGHTS]
    return tuple(outs)
```
